```python
import jax, jax.numpy as jnp
from jax import lax
import numpy as np

D_MODEL = 1024
BATCH = 16
SEQ = 2048
DEPTH = 1

D_PLE = 256
D_CONV = 512
CONV_WIDTH = 31
N_HEADS = 8
HEAD_DIM = 64
D_ATTN = N_HEADS * HEAD_DIM
D_KV = HEAD_DIM
N_IDX_HEADS = 8
IDX_DIM = 64
TOPK_MAX = 256
Q_BLOCK = 128
EPS = 1e-6

COL_SIZES = (
    ("conv_in", 2 * D_CONV),
    ("conv_z", D_CONV),
    ("q", D_ATTN),
    ("k", D_KV),
    ("v", D_KV),
    ("attn_z", D_ATTN),
    ("idx_q", N_IDX_HEADS * IDX_DIM),
    ("idx_k", IDX_DIM),
    ("idx_w", N_IDX_HEADS),
    ("gate_a", D_MODEL),
    ("gate_b", D_MODEL),
)
D_IN_TOTAL = 2 * D_CONV + D_CONV + D_ATTN + 2 * D_KV + D_ATTN + N_IDX_HEADS * IDX_DIM + IDX_DIM + N_IDX_HEADS + 2 * D_MODEL

kernel_name = "hybrid_conformer_dsa_gated_block"


def rms_norm(x, g):
    xf = x.astype(jnp.float32)
    y = xf * lax.rsqrt(jnp.mean(xf * xf, axis=-1, keepdims=True) + EPS) * g.astype(jnp.float32)
    return y.astype(x.dtype)


def layer_norm(x, g, b):
    xf = x.astype(jnp.float32)
    mu = jnp.mean(xf, axis=-1, keepdims=True)
    var = jnp.mean(jnp.square(xf - mu), axis=-1, keepdims=True)
    y = (xf - mu) * lax.rsqrt(var + EPS) * g.astype(jnp.float32) + b.astype(jnp.float32)
    return y.astype(x.dtype)


def split_cols(u):
    out, off = {}, 0
    for name, size in COL_SIZES:
        out[name] = u[..., off:off + size]
        off += size
    return out


def causal_depthwise_conv(u, w, b):
    C = u.shape[-1]
    y = lax.conv_general_dilated(
        u.astype(w.dtype), w.reshape(CONV_WIDTH, 1, C),
        window_strides=(1,), padding=[(CONV_WIDTH - 1, 0)],
        dimension_numbers=("NWC", "WIO", "NWC"), feature_group_count=C)
    return y + b


def conformer_branch(cols, conv_w, conv_b, ln_g, ln_b):
    val, gate = jnp.split(cols["conv_in"], 2, axis=-1)
    u = val * jax.nn.sigmoid(gate)
    c = causal_depthwise_conv(u, conv_w, conv_b)
    c = jax.nn.silu(layer_norm(c, ln_g, ln_b))
    return c * jax.nn.silu(cols["conv_z"])


def dsa_sparse_attention(q, k, v, q_idx, k_idx, w_idx):
    B, L = q.shape[0], q.shape[1]
    n_keep = min(TOPK_MAX, L // 4)
    nb = L // Q_BLOCK
    slopes = 2.0 ** (-8.0 * jnp.arange(1, N_HEADS + 1, dtype=jnp.float32) / N_HEADS)
    key_pos = jnp.arange(L, dtype=jnp.int32)

    def to_blocks(a):
        return jnp.moveaxis(a.reshape((B, nb, Q_BLOCK) + a.shape[2:]), 1, 0)

    def block(args):
        qb, qib, wb, t0 = args
        tq = t0 + jnp.arange(Q_BLOCK, dtype=jnp.int32)
        dots = jnp.einsum("bqhd,bsd->bqhs", qib, k_idx).astype(jnp.float32) * (IDX_DIM ** -0.5)
        w_f = wb.astype(jnp.float32) * (N_IDX_HEADS ** -0.5)
        score = jnp.einsum("bqh,bqhs->bqs", w_f, jax.nn.relu(dots))
        causal = key_pos[None, :] <= tq[:, None]
        score = jnp.where(causal[None], score, -jnp.inf)
        _, sel = lax.top_k(score, n_keep)
        gather = jax.vmap(lambda kv_b, idx_b: kv_b[idx_b])
        k_sel = gather(k, sel)
        v_sel = gather(v, sel)
        logits = jnp.einsum("bqhd,bqkd->bqhk", qb, k_sel).astype(jnp.float32) * (HEAD_DIM ** -0.5)
        dist = (tq[None, :, None] - sel).astype(jnp.float32)
        logits = logits - slopes[None, None, :, None] * dist[:, :, None, :]
        valid = sel <= tq[None, :, None]
        logits = jnp.where(valid[:, :, None, :], logits, -jnp.inf)
        probs = jax.nn.softmax(logits, axis=-1)
        o = jnp.einsum("bqhk,bqkd->bqhd", probs, v_sel.astype(jnp.float32))
        return o.astype(qb.dtype)

    t0s = jnp.arange(nb, dtype=jnp.int32) * Q_BLOCK
    out = lax.map(block, (to_blocks(q), to_blocks(q_idx), to_blocks(w_idx), t0s))
    return jnp.moveaxis(out, 0, 1).reshape(B, L, N_HEADS * HEAD_DIM)


def attention_branch(cols):
    B, L = cols["q"].shape[0], cols["q"].shape[1]
    q = cols["q"].reshape(B, L, N_HEADS, HEAD_DIM)
    q_idx = cols["idx_q"].reshape(B, L, N_IDX_HEADS, IDX_DIM)
    o = dsa_sparse_attention(q, cols["k"], cols["v"], q_idx, cols["idx_k"], cols["idx_w"])
    return o * jax.nn.silu(cols["attn_z"])


def setup_inputs(seed: int = 0) -> dict:
    key = jax.random.key(seed)
    ks = jax.random.split(key, 16)
    f32 = jnp.float32
    nrm = lambda k, shape, scale: jax.random.normal(k, shape, f32) * scale
    return {
        "x": nrm(ks[0], (BATCH, SEQ, D_MODEL), 1.0),
        "p": nrm(ks[1], (DEPTH, BATCH, SEQ, D_PLE), 1.0),
        "norm_g": 1.0 + nrm(ks[2], (DEPTH, D_MODEL), 0.02),
        "w_in": nrm(ks[3], (DEPTH, D_MODEL, D_IN_TOTAL), D_MODEL ** -0.5),
        "conv_w": nrm(ks[4], (DEPTH, CONV_WIDTH, D_CONV), CONV_WIDTH ** -0.5),
        "conv_b": nrm(ks[5], (DEPTH, D_CONV), 0.01),
        "conv_ln_g": 1.0 + nrm(ks[6], (DEPTH, D_CONV), 0.02),
        "conv_ln_b": nrm(ks[7], (DEPTH, D_CONV), 0.01),
        "w_a_out": nrm(ks[8], (DEPTH, D_CONV, D_MODEL), D_CONV ** -0.5),
        "w_b_out": nrm(ks[9], (DEPTH, D_ATTN, D_MODEL), D_ATTN ** -0.5),
        "w_o": nrm(ks[10], (DEPTH, D_MODEL, D_MODEL), D_MODEL ** -0.5),
        "ple_norm_g": 1.0 + nrm(ks[11], (DEPTH, D_MODEL), 0.02),
        "w_ple_gate": nrm(ks[12], (DEPTH, D_MODEL, D_MODEL), D_MODEL ** -0.5),
        "w_ple_proj": nrm(ks[13], (DEPTH, D_PLE, D_MODEL), D_PLE ** -0.5),
        "final_g": 1.0 + nrm(ks[14], (D_MODEL,), 0.02),
    }


def reference(x, p, norm_g, w_in, conv_w, conv_b, conv_ln_g, conv_ln_b, w_a_out, w_b_out,
              w_o, ple_norm_g, w_ple_gate, w_ple_proj, final_g):
    for i in range(DEPTH):
        h = rms_norm(x, norm_g[i])
        cols = split_cols(jnp.einsum("bld,de->ble", h, w_in[i]))
        y_a = conformer_branch(cols, conv_w[i], conv_b[i], conv_ln_g[i], conv_ln_b[i])
        y_b = attention_branch(cols)
        merged = (jax.nn.sigmoid(cols["gate_a"]) * jnp.einsum("blc,cd->bld", y_a, w_a_out[i])
                  + jax.nn.sigmoid(cols["gate_b"]) * jnp.einsum("blc,cd->bld", y_b, w_b_out[i]))
        x = x + jnp.einsum("bld,de->ble", merged, w_o[i])
        ple_gate = jax.nn.sigmoid(jnp.einsum("bld,de->ble", rms_norm(x, ple_norm_g[i]), w_ple_gate[i]))
        x = x + ple_gate * jnp.einsum("blp,pd->bld", p[i], w_ple_proj[i])
    return rms_norm(x, final_g)
```

```python
import functools

import jax
import jax.numpy as jnp
from jax import lax
from jax.experimental import pallas as pl
from jax.experimental.pallas import tpu as pltpu

D_MODEL = 1024
D_PLE = 256
D_CONV = 512
CONV_WIDTH = 31
N_HEADS = 8
HEAD_DIM = 64
D_ATTN = N_HEADS * HEAD_DIM
N_IDX_HEADS = 8
IDX_DIM = 64
TOPK_MAX = 256
EPS = 1e-6

LANES = 128
TQ = 128
KC = 256
HALO = 32
NEG = -1e30
INT_MIN = -(2 ** 31)
VMEM_LIMIT = 56 * 1024 * 1024

F32 = jnp.float32
BF16 = jnp.bfloat16

_C_VAL, _C_GATE, _C_Z, _C_Q, _C_AZ, _C_IQ, _C_GA, _C_GB, _C_KV, _C_IKW, _C_END = (
    0, 512, 1024, 1536, 2048, 2560, 3072, 4096, 5120, 5248, 5376)


def _sigmoid(x):
    return 1.0 / (1.0 + jnp.exp(-x))


def _silu(x):
    return x * _sigmoid(x)


def _rms(x, g):
    return x * lax.rsqrt(jnp.mean(x * x, axis=-1, keepdims=True) + EPS) * g


def _proj_kernel(x_ref, g_ref, w_ref,
                 u_ref, sz_ref, saz_ref, ga_ref, gb_ref,
                 qp_ref, kp_ref, vp_ref, iq_ref, ik_ref, iwt_ref, *, tm):
    i = pl.program_id(1)
    h = _rms(x_ref[0], g_ref[...]).astype(BF16)

    def mm(a, b):
        return jnp.dot(h, w_ref[:, a:b], preferred_element_type=F32)

    u_ref[0] = mm(_C_VAL, _C_GATE) * _sigmoid(mm(_C_GATE, _C_Z))
    sz_ref[0] = _silu(mm(_C_Z, _C_Q))
    saz_ref[0] = _silu(mm(_C_AZ, _C_IQ))
    ga_ref[0] = _sigmoid(mm(_C_GA, _C_GB))
    gb_ref[0] = _sigmoid(mm(_C_GB, _C_KV))

    lane = lax.broadcasted_iota(jnp.int32, (tm, LANES), 1)
    pos = i * tm + lax.broadcasted_iota(jnp.int32, (tm, LANES), 0)
    pos_hi = (pos >> 6).astype(F32)
    pos_lo = (pos & 63).astype(F32)

    q = mm(_C_Q, _C_AZ) * (HEAD_DIM ** -0.5)
    for hd in range(N_HEADS):
        slope = 2.0 ** (-8.0 * (hd + 1) / N_HEADS)
        pair = q[:, (hd // 2) * LANES:(hd // 2 + 1) * LANES]
        if hd % 2 == 1:
            pair = pltpu.roll(pair, 64, axis=1)
        ext = jnp.where(lane == 64, -slope * 64.0 * pos_hi,
              jnp.where(lane == 65, -slope * pos_lo,
              jnp.where(lane == 66, slope * 64.0,
              jnp.where(lane == 67, slope, 0.0))))
        qp_ref[0, hd] = jnp.where(lane < 64, pair, ext).astype(BF16)

    kv = mm(_C_KV, _C_IKW)
    kext = jnp.where(lane < 66, 1.0,
           jnp.where(lane == 66, pos_hi,
           jnp.where(lane == 67, pos_lo, 0.0)))
    kp_ref[0] = jnp.where(lane < 64, kv, kext).astype(BF16)
    vp_ref[0] = jnp.where(lane < 64, 0.0, kv).astype(BF16)

    iq = mm(_C_IQ, _C_GA) * (IDX_DIM ** -0.5)
    for hd in range(N_IDX_HEADS):
        iq_ref[0, hd] = iq[:, hd * IDX_DIM:(hd + 1) * IDX_DIM].astype(BF16)
    ikw = mm(_C_IKW, _C_END)
    ik_ref[0] = ikw[:, :IDX_DIM].astype(BF16)
    iwt_ref[0] = ikw.T[IDX_DIM:IDX_DIM + N_IDX_HEADS, :] * (N_IDX_HEADS ** -0.5)


def _key_to_float(k):
    bits = jnp.where(k >= 0, k, k ^ jnp.int32(0x7FFFFFFF))
    f = lax.bitcast_convert_type(bits, F32)
    return jnp.where(k < 0, jnp.where(f != f, -jnp.inf, f), f)


def _attn_kernel(iq_ref, ik_ref, iwt_ref, qp_ref, kp_ref, vp_ref, o_ref,
                 st_ref, bias_ref, logit_ref, m_ref, l_ref, acc_ref, *, n_keep):
    j = pl.program_id(1)
    t0 = j * TQ
    nkc = (j + 2) // 2

    iq2 = iq_ref[0].reshape(N_IDX_HEADS * TQ, IDX_DIM)
    iwt = iwt_ref[0]
    tcol = t0 + lax.broadcasted_iota(jnp.int32, (KC, TQ), 1)
    srow0 = lax.broadcasted_iota(jnp.int32, (KC, TQ), 0)

    def score_body(c, carry):
        ks = pl.multiple_of(c * KC, KC)
        kic = ik_ref[0, pl.ds(ks, KC), :]
        d = lax.dot_general(kic, iq2, (((1,), (1,)), ((), ())),
                            preferred_element_type=F32)
        acc = jnp.zeros((KC, TQ), F32)
        for hd in range(N_IDX_HEADS):
            acc = acc + iwt[hd:hd + 1, :] * jnp.maximum(d[:, hd * TQ:(hd + 1) * TQ], 0.0)
        causal = (ks + srow0) <= tcol
        st_ref[pl.ds(ks, KC), :] = jnp.where(causal, acc, -jnp.inf)
        return carry

    lax.fori_loop(0, nkc, score_body, 0)

    def count_ge(thr8):
        def body(c, cnt):
            ks = pl.multiple_of(c * KC, KC)
            s = st_ref[pl.ds(ks, KC), :].reshape(KC // 8, 8, TQ)
            return cnt + jnp.sum((s >= thr8[None]).astype(jnp.int32), axis=0)
        cnt8 = lax.fori_loop(0, nkc, body, jnp.zeros((8, TQ), jnp.int32))
        return jnp.broadcast_to(jnp.sum(cnt8, axis=0, keepdims=True), (8, TQ))

    key0 = jnp.where(count_ge(jnp.zeros((8, TQ), F32)) >= n_keep,
                     jnp.int32(0), jnp.int32(INT_MIN))

    def bit_body(it, key):
        cand = key | jnp.left_shift(jnp.int32(1), 30 - it)
        return jnp.where(count_ge(_key_to_float(cand)) >= n_keep, cand, key)

    key = lax.fori_loop(0, 31, bit_body, key0)
    lo8 = _key_to_float(key)
    hi8 = _key_to_float(key + 1)
    need8 = (n_keep - count_ge(hi8)).astype(F32)

    tri = (lax.broadcasted_iota(jnp.int32, (KC, KC), 0)
           >= lax.broadcasted_iota(jnp.int32, (KC, KC), 1)).astype(BF16)
    lo = lo8[0:1]
    hi = hi8[0:1]
    need = need8[0:1]

    def bias_body(c, taken):
        ks = pl.multiple_of(c * KC, KC)
        s = st_ref[pl.ds(ks, KC), :]
        above = s >= hi
        tied = jnp.where(above, 0.0, jnp.where(s >= lo, 1.0, 0.0))
        rank = taken + jnp.dot(tri, tied.astype(BF16), preferred_element_type=F32)
        pick = jnp.where(above, 1.0, jnp.where(rank <= need, tied, 0.0))
        causal = (ks + srow0) <= tcol
        bias_t = jnp.where(causal, jnp.where(pick > 0.5, 0.0, NEG), NEG)
        bias_ref[c] = bias_t.T
        return rank[KC - 1:KC, :]

    lax.fori_loop(0, nkc, bias_body, jnp.zeros((1, TQ), F32))

    q2 = qp_ref[0].reshape(N_HEADS * TQ, LANES)
    m_ref[...] = jnp.full(m_ref.shape, NEG, F32)

    def logit_body(c, carry):
        ks = pl.multiple_of(c * KC, KC)
        kc = kp_ref[0, pl.ds(ks, KC), :]
        lg = lax.dot_general(q2, kc, (((1,), (1,)), ((), ())),
                             preferred_element_type=F32)
        lg = (lg.reshape(N_HEADS, TQ, KC) + bias_ref[c][None]).reshape(N_HEADS * TQ, KC)
        logit_ref[c] = lg
        m_ref[...] = jnp.maximum(m_ref[...], jnp.maximum(lg[:, :LANES], lg[:, LANES:]))
        return carry

    lax.fori_loop(0, nkc, logit_body, 0)

    m_ref[...] = jnp.broadcast_to(jnp.max(m_ref[...], axis=-1, keepdims=True), m_ref.shape)
    l_ref[...] = jnp.zeros(l_ref.shape, F32)
    acc_ref[...] = jnp.zeros(acc_ref.shape, F32)

    def pv_body(c, carry):
        ks = pl.multiple_of(c * KC, KC)
        mb = m_ref[...]
        lg = logit_ref[c]
        p0 = jnp.exp(lg[:, :LANES] - mb)
        p1 = jnp.exp(lg[:, LANES:] - mb)
        l_ref[...] += p0 + p1
        p = jnp.concatenate([p0, p1], axis=1).astype(BF16)
        acc_ref[...] += jnp.dot(p, vp_ref[0, pl.ds(ks, KC), :], preferred_element_type=F32)
        return carry

    lax.fori_loop(0, nkc, pv_body, 0)

    l = jnp.sum(l_ref[...], axis=-1, keepdims=True)
    o = acc_ref[...] / l
    for hd in range(N_HEADS):
        o_ref[0, :, hd * HEAD_DIM:(hd + 1) * HEAD_DIM] = o[hd * TQ:(hd + 1) * TQ, HEAD_DIM:]


def _post_kernel(x_ref, u_ref, uh_ref, sz_ref, o_ref, saz_ref, ga_ref, gb_ref, p_ref,
                 cw_ref, cb_ref, lg_ref, lb_ref, wa_ref, wb_ref, wo_ref,
                 pg_ref, wpg_ref, wpp_ref, fg_ref, out_ref, ext_ref, *, tm, final_norm):
    i = pl.program_id(1)
    halo = uh_ref[0]
    ext_ref[0:HALO, :] = jnp.where(i > 0, halo, jnp.zeros_like(halo))
    ext_ref[HALO:, :] = u_ref[0]

    base = HALO - (CONV_WIDTH - 1)
    c = jnp.zeros((tm, D_CONV), F32)
    for jj in range(CONV_WIDTH):
        c = c + cw_ref[jj:jj + 1, :] * ext_ref[base + jj:base + jj + tm, :]
    c = c + cb_ref[...]

    mu = jnp.mean(c, axis=-1, keepdims=True)
    cc = c - mu
    var = jnp.mean(cc * cc, axis=-1, keepdims=True)
    ln = cc * lax.rsqrt(var + EPS) * lg_ref[...] + lb_ref[...]
    y_a = (_silu(ln) * sz_ref[0]).astype(BF16)
    y_b = (o_ref[0] * saz_ref[0]).astype(BF16)

    merged = (ga_ref[0] * jnp.dot(y_a, wa_ref[...], preferred_element_type=F32)
              + gb_ref[0] * jnp.dot(y_b, wb_ref[...], preferred_element_type=F32))
    x1 = x_ref[0] + jnp.dot(merged.astype(BF16), wo_ref[...], preferred_element_type=F32)

    gate = _sigmoid(jnp.dot(_rms(x1, pg_ref[...]).astype(BF16), wpg_ref[...],
                            preferred_element_type=F32))
    x2 = x1 + gate * jnp.dot(p_ref[0].astype(BF16), wpp_ref[...], preferred_element_type=F32)
    out_ref[0] = _rms(x2, fg_ref[...]) if final_norm else x2


def _pack_w_in(w):
    o_in, o_z, o_q, o_k, o_v, o_az, o_iq, o_ik, o_iw, o_ga, o_gb = (
        0, 1024, 1536, 2048, 2112, 2176, 2688, 3200, 3264, 3272, 4296)
    pad = jnp.zeros((D_MODEL, LANES - IDX_DIM - N_IDX_HEADS), w.dtype)
    cols = [w[:, o_in:o_z], w[:, o_z:o_q], w[:, o_q:o_k], w[:, o_az:o_iq], w[:, o_iq:o_ik],
            w[:, o_ga:o_gb], w[:, o_gb:o_gb + D_MODEL], w[:, o_k:o_az],
            w[:, o_ik:o_ga], pad]
    return jnp.concatenate(cols, axis=1).astype(BF16)


def _layer(x, p, norm_g, w_in, conv_w, conv_b, ln_g, ln_b, w_a, w_b, w_o,
           ple_g, w_pg, w_pp, out_g, final_norm):
    B, L, D = x.shape
    n_keep = min(TOPK_MAX, L // 4)
    nb = L // TQ
    cp = functools.partial(pltpu.CompilerParams, vmem_limit_bytes=VMEM_LIMIT)
    row = lambda a: a.reshape(1, -1)

    tm1 = 512
    wp = _pack_w_in(w_in)
    tok = lambda w: pl.BlockSpec((1, tm1, w), lambda b, i: (b, i, 0))
    hd4 = lambda w: pl.BlockSpec((1, N_HEADS, tm1, w), lambda b, i: (b, 0, i, 0))
    f32s = lambda w: jax.ShapeDtypeStruct((B, L, w), F32)
    u, sz, saz, ga, gb, qp, kp, vp, iq, ik, iwt = pl.pallas_call(
        functools.partial(_proj_kernel, tm=tm1),
        grid=(B, L // tm1),
        in_specs=[tok(D),
                  pl.BlockSpec((1, D), lambda b, i: (0, 0)),
                  pl.BlockSpec((D, _C_END), lambda b, i: (0, 0))],
        out_specs=[tok(D_CONV), tok(D_CONV), tok(D_ATTN), tok(D), tok(D),
                   hd4(LANES), tok(LANES), tok(LANES), hd4(IDX_DIM), tok(IDX_DIM),
                   pl.BlockSpec((1, N_IDX_HEADS, tm1), lambda b, i: (b, 0, i))],
        out_shape=[f32s(D_CONV), f32s(D_CONV), f32s(D_ATTN), f32s(D), f32s(D),
                   jax.ShapeDtypeStruct((B, N_HEADS, L, LANES), BF16),
                   jax.ShapeDtypeStruct((B, L, LANES), BF16),
                   jax.ShapeDtypeStruct((B, L, LANES), BF16),
                   jax.ShapeDtypeStruct((B, N_IDX_HEADS, L, IDX_DIM), BF16),
                   jax.ShapeDtypeStruct((B, L, IDX_DIM), BF16),
                   jax.ShapeDtypeStruct((B, N_IDX_HEADS, L), F32)],
        compiler_params=cp(dimension_semantics=("arbitrary", "arbitrary")),
        name="proj",
    )(x, row(norm_g), wp)

    nc = L // KC
    o = pl.pallas_call(
        functools.partial(_attn_kernel, n_keep=n_keep),
        grid=(B, nb),
        in_specs=[pl.BlockSpec((1, N_IDX_HEADS, TQ, IDX_DIM), lambda b, j: (b, 0, j, 0)),
                  pl.BlockSpec((1, L, IDX_DIM), lambda b, j: (b, 0, 0)),
                  pl.BlockSpec((1, N_IDX_HEADS, TQ), lambda b, j: (b, 0, j)),
                  pl.BlockSpec((1, N_HEADS, TQ, LANES), lambda b, j: (b, 0, j, 0)),
                  pl.BlockSpec((1, L, LANES), lambda b, j: (b, 0, 0)),
                  pl.BlockSpec((1, L, LANES), lambda b, j: (b, 0, 0))],
        out_specs=pl.BlockSpec((1, TQ, D_ATTN), lambda b, j: (b, j, 0)),
        out_shape=jax.ShapeDtypeStruct((B, L, D_ATTN), F32),
        scratch_shapes=[pltpu.VMEM((L, TQ), F32),
                        pltpu.VMEM((nc, TQ, KC), F32),
                        pltpu.VMEM((nc, N_HEADS * TQ, KC), F32),
                        pltpu.VMEM((N_HEADS * TQ, LANES), F32),
                        pltpu.VMEM((N_HEADS * TQ, LANES), F32),
                        pltpu.VMEM((N_HEADS * TQ, LANES), F32)],
        compiler_params=cp(dimension_semantics=("arbitrary", "arbitrary")),
        name="attn",
    )(iq, ik, iwt, qp, kp, vp)

    tm3 = 256
    hb = tm3 // HALO
    tok = lambda w: pl.BlockSpec((1, tm3, w), lambda b, i: (b, i, 0))
    full = lambda a: pl.BlockSpec(a.shape, lambda b, i: (0,) * a.ndim)
    consts = [conv_w, row(conv_b), row(ln_g), row(ln_b), w_a.astype(BF16), w_b.astype(BF16),
              w_o.astype(BF16), row(ple_g), w_pg.astype(BF16), w_pp.astype(BF16), row(out_g)]
    return pl.pallas_call(
        functools.partial(_post_kernel, tm=tm3, final_norm=final_norm),
        grid=(B, L // tm3),
        in_specs=[tok(D), tok(D_CONV),
                  pl.BlockSpec((1, HALO, D_CONV), lambda b, i: (b, jnp.maximum(i * hb - 1, 0), 0)),
                  tok(D_CONV), tok(D_ATTN), tok(D_ATTN), tok(D), tok(D), tok(D_PLE)]
                 + [full(a) for a in consts],
        out_specs=tok(D),
        out_shape=jax.ShapeDtypeStruct((B, L, D), F32),
        scratch_shapes=[pltpu.VMEM((HALO + tm3, D_CONV), F32)],
        compiler_params=cp(dimension_semantics=("arbitrary", "arbitrary")),
        name="post",
    )(x, u, u, sz, o, saz, ga, gb, p, *consts)


def kernel(x, p, norm_g, w_in, conv_w, conv_b, conv_ln_g, conv_ln_b, w_a_out, w_b_out,
           w_o, ple_norm_g, w_ple_gate, w_ple_proj, final_g):
    depth = w_in.shape[0]
    for i in range(depth):
        x = _layer(x, p[i], norm_g[i], w_in[i], conv_w[i], conv_b[i], conv_ln_g[i],
                   conv_ln_b[i], w_a_out[i], w_b_out[i], w_o[i], ple_norm_g[i],
                   w_ple_gate[i], w_ple_proj[i], final_g, i == depth - 1)
    return x
```

```python
import functools

import jax
import jax.numpy as jnp
from jax import lax
from jax.experimental import pallas as pl
from jax.experimental.pallas import tpu as pltpu

D_MODEL = 1024
D_PLE = 256
D_CONV = 512
CONV_WIDTH = 31
N_HEADS = 8
HEAD_DIM = 64
D_ATTN = N_HEADS * HEAD_DIM
N_IDX_HEADS = 8
IDX_DIM = 64
TOPK_MAX = 256
EPS = 1e-6

LANES = 128
SUBLANES = 8
TQ = 256
KC = 256
TM_PROJ = 512
TM_POST = 256
HALO = 32
NEG = -1e30
INT_MIN = -(2 ** 31)
VMEM_LIMIT = 56 * 1024 * 1024

F32 = jnp.float32
BF16 = jnp.bfloat16

_C_VAL, _C_GATE, _C_Z, _C_AZ, _C_GA, _C_GB, _C_KV, _C_IKW, _C_END = (
    0, 512, 1024, 1536, 2048, 3072, 4096, 4224, 4352)
_R_Q, _R_IQ, _R_V, _R_END = 0, 512, 1024, 1088

_NT = (((1,), (1,)), ((), ()))


def _sigmoid(x):
    return 1.0 / (1.0 + jnp.exp(-x))


def _silu(x):
    return x * _sigmoid(x)


def _rms(x, g):
    return x * lax.rsqrt(jnp.mean(x * x, axis=-1, keepdims=True) + EPS) * g


def _proj_kernel(x_ref, g_ref, w_ref, wt_ref,
                 u_ref, sz_ref, saz_ref, ga_ref, gb_ref,
                 qpt_ref, kp_ref, vt_ref, iqt_ref, ik_ref, iwt_ref, *, tm):
    i = pl.program_id(1)
    h = _rms(x_ref[0], g_ref[...]).astype(BF16)

    def mm(a, b):
        return jnp.dot(h, w_ref[:, a:b], preferred_element_type=F32)

    def mm_t(a, b):
        return lax.dot_general(wt_ref[a:b, :], h, _NT, preferred_element_type=F32)

    u_ref[0] = mm(_C_VAL, _C_GATE) * _sigmoid(mm(_C_GATE, _C_Z))
    sz_ref[0] = _silu(mm(_C_Z, _C_AZ))
    saz_ref[0] = _silu(mm(_C_AZ, _C_GA))
    ga_ref[0] = _sigmoid(mm(_C_GA, _C_GB))
    gb_ref[0] = _sigmoid(mm(_C_GB, _C_KV))

    lane = lax.broadcasted_iota(jnp.int32, (tm, LANES), 1)
    pos = i * tm + lax.broadcasted_iota(jnp.int32, (tm, LANES), 0)
    kext = jnp.where(lane < 66, 1.0,
           jnp.where(lane == 66, (pos >> 6).astype(F32),
           jnp.where(lane == 67, (pos & 63).astype(F32), 0.0)))
    kp_ref[0] = jnp.where(lane < 64, mm(_C_KV, _C_IKW), kext).astype(BF16)

    ikw = mm(_C_IKW, _C_END)
    ik_ref[0] = ikw[:, :IDX_DIM].astype(BF16)
    iwt_ref[0] = ikw.T[IDX_DIM:IDX_DIM + N_IDX_HEADS, :] * (N_IDX_HEADS ** -0.5)

    sub = lax.broadcasted_iota(jnp.int32, (HEAD_DIM, tm), 0)
    tpos = i * tm + lax.broadcasted_iota(jnp.int32, (HEAD_DIM, tm), 1)
    t_hi = (tpos >> 6).astype(F32)
    t_lo = (tpos & 63).astype(F32)
    qt = mm_t(_R_Q, _R_IQ) * (HEAD_DIM ** -0.5)
    for hd in range(N_HEADS):
        slope = 2.0 ** (-8.0 * (hd + 1) / N_HEADS)
        ext = jnp.where(sub == 0, -slope * 64.0 * t_hi,
              jnp.where(sub == 1, -slope * t_lo,
              jnp.where(sub == 2, slope * 64.0,
              jnp.where(sub == 3, slope, 0.0))))
        qpt_ref[0, hd, 0:HEAD_DIM, :] = qt[hd * HEAD_DIM:(hd + 1) * HEAD_DIM, :].astype(BF16)
        qpt_ref[0, hd, HEAD_DIM:, :] = ext.astype(BF16)

    iqt = mm_t(_R_IQ, _R_V) * (IDX_DIM ** -0.5)
    for hd in range(N_IDX_HEADS):
        iqt_ref[0, hd] = iqt[hd * IDX_DIM:(hd + 1) * IDX_DIM, :].astype(BF16)

    vt = mm_t(_R_V, _R_END).astype(BF16)
    for c in range(tm // KC):
        vt_ref[0, c] = vt[:, c * KC:(c + 1) * KC]


def _key_to_float(k):
    bits = jnp.where(k >= 0, k, k ^ jnp.int32(0x7FFFFFFF))
    f = lax.bitcast_convert_type(bits, F32)
    return jnp.where(k < 0, jnp.where(f != f, -jnp.inf, f), f)


def _attn_kernel(iqt_ref, ik_ref, iwt_ref, qpt_ref, kp_ref, vt_ref, o_ref,
                 st_ref, bias_ref, ot_ref, lga_ref, lgb_ref, m_ref, l_ref, mx_ref,
                 *acc_refs, n_keep):
    j = pl.program_id(1)
    t0 = j * TQ
    nkc = j + 1

    tcol = t0 + lax.broadcasted_iota(jnp.int32, (KC, TQ), 1)
    srow0 = lax.broadcasted_iota(jnp.int32, (KC, TQ), 0)

    def score_body(c, carry):
        ks = pl.multiple_of(c * KC, KC)
        kic = ik_ref[0, pl.ds(ks, KC), :]
        acc = jnp.zeros((KC, TQ), F32)
        for hd in range(N_IDX_HEADS):
            d = jnp.dot(kic, iqt_ref[0, hd], preferred_element_type=F32)
            acc = acc + iwt_ref[0, hd:hd + 1, :] * jnp.maximum(d, 0.0)
        st_ref[pl.ds(ks, KC), :] = jnp.where((ks + srow0) <= tcol, acc, -jnp.inf)
        return carry

    lax.fori_loop(0, nkc, score_body, 0)

    def count_ge(thr8):
        def body(c, cnt):
            ks = pl.multiple_of(c * KC, KC)
            s = st_ref[pl.ds(ks, KC), :].reshape(KC // SUBLANES, SUBLANES, TQ)
            return cnt + jnp.sum((s >= thr8[None]).astype(jnp.int32), axis=0)
        cnt8 = lax.fori_loop(0, nkc, body, jnp.zeros((SUBLANES, TQ), jnp.int32))
        return jnp.broadcast_to(jnp.sum(cnt8, axis=0, keepdims=True), (SUBLANES, TQ))

    key0 = jnp.where(count_ge(jnp.zeros((SUBLANES, TQ), F32)) >= n_keep,
                     jnp.int32(0), jnp.int32(INT_MIN))

    def bit_body(it, key):
        cand = key | jnp.left_shift(jnp.int32(1), 30 - it)
        return jnp.where(count_ge(_key_to_float(cand)) >= n_keep, cand, key)

    key = lax.fori_loop(0, 31, bit_body, key0)
    lo8 = _key_to_float(key)
    hi8 = _key_to_float(key + 1)
    need8 = (n_keep - count_ge(hi8)).astype(F32)

    tri = (lax.broadcasted_iota(jnp.int32, (KC, KC), 0)
           >= lax.broadcasted_iota(jnp.int32, (KC, KC), 1)).astype(BF16)
    lo = lo8[0:1]
    hi = hi8[0:1]
    need = need8[0:1]

    def bias_body(c, taken):
        ks = pl.multiple_of(c * KC, KC)
        s = st_ref[pl.ds(ks, KC), :]
        above = s >= hi
        tied = jnp.where(above, 0.0, jnp.where(s >= lo, 1.0, 0.0))
        rank = taken + jnp.dot(tri, tied.astype(BF16), preferred_element_type=F32)
        pick = jnp.where(above, 1.0, jnp.where(rank <= need, tied, 0.0))
        causal = (ks + srow0) <= tcol
        bias_ref[pl.ds(ks, KC), :] = jnp.where(causal, jnp.where(pick > 0.5, 0.0, NEG), NEG)
        return rank[KC - 1:KC, :]

    lax.fori_loop(0, nkc, bias_body, jnp.zeros((1, TQ), F32))

    for acc_ref in acc_refs:
        acc_ref[...] = jnp.zeros(acc_ref.shape, F32)

    m_ref[...] = jnp.full(m_ref.shape, NEG, F32)
    l_ref[...] = jnp.zeros(l_ref.shape, F32)

    def heads_step(c_qk, qk_ref, c_pv, pv_ref):
        if c_qk is not None:
            ks = pl.multiple_of(c_qk * KC, KC)
            kpc = kp_ref[0, pl.ds(ks, KC), :]
            bias = bias_ref[pl.ds(ks, KC), :]
        if c_pv is not None:
            vtc = vt_ref[0, c_pv]
            m_all, l_all, mx_cur = m_ref[...], l_ref[...], mx_ref[...]
        m_rows, l_rows, mx_rows = [], [], []
        for hd in range(N_HEADS):
            if c_qk is not None:
                lg = jnp.dot(kpc, qpt_ref[0, hd], preferred_element_type=F32) + bias
                qk_ref[hd] = lg
                mx_rows.append(jnp.max(lg, axis=0, keepdims=True))
            if c_pv is not None:
                m_old = m_all[hd:hd + 1, :]
                m_new = jnp.maximum(m_old, mx_cur[hd:hd + 1, :])
                alpha = jnp.exp(m_old - m_new)
                p = jnp.exp(pv_ref[hd] - m_new)
                l_rows.append(alpha * l_all[hd:hd + 1, :] + jnp.sum(p, axis=0, keepdims=True))
                acc_refs[hd][...] = alpha * acc_refs[hd][...] + jnp.dot(
                    vtc, p.astype(BF16), preferred_element_type=F32)
                m_rows.append(m_new)
        if c_pv is not None:
            m_ref[...] = jnp.concatenate(m_rows, axis=0)
            l_ref[...] = jnp.concatenate(l_rows, axis=0)
        if c_qk is not None:
            mx_ref[...] = jnp.concatenate(mx_rows, axis=0)

    n_steps = nkc - 1
    heads_step(0, lga_ref, None, None)

    def pair_body(i, carry):
        heads_step(2 * i + 1, lgb_ref, 2 * i, lga_ref)
        heads_step(2 * i + 2, lga_ref, 2 * i + 1, lgb_ref)
        return carry

    lax.fori_loop(0, n_steps // 2, pair_body, 0)

    @pl.when(n_steps % 2 == 1)
    def _():
        heads_step(n_steps, lgb_ref, n_steps - 1, lga_ref)
        heads_step(None, None, n_steps, lgb_ref)

    @pl.when(n_steps % 2 == 0)
    def _():
        heads_step(None, None, n_steps, lga_ref)

    l_all = l_ref[...]
    for hd in range(N_HEADS):
        ot_ref[hd * HEAD_DIM:(hd + 1) * HEAD_DIM, :] = acc_refs[hd][...] / l_all[hd:hd + 1, :]
    o_ref[0] = ot_ref[...].T


def _post_kernel(x_ref, u_ref, uh_ref, sz_ref, o_ref, saz_ref, ga_ref, gb_ref, p_ref,
                 cw_ref, cb_ref, lg_ref, lb_ref, wa_ref, wb_ref, wo_ref,
                 pg_ref, wpg_ref, wpp_ref, fg_ref, out_ref, ext_ref, *, tm, final_norm):
    i = pl.program_id(1)
    halo = uh_ref[0]
    ext_ref[0:HALO, :] = jnp.where(i > 0, halo, jnp.zeros_like(halo))
    ext_ref[HALO:, :] = u_ref[0]

    base = HALO - (CONV_WIDTH - 1)
    c = jnp.zeros((tm, D_CONV), F32)
    for jj in range(CONV_WIDTH):
        c = c + cw_ref[jj:jj + 1, :] * ext_ref[base + jj:base + jj + tm, :]
    c = c + cb_ref[...]

    mu = jnp.mean(c, axis=-1, keepdims=True)
    cc = c - mu
    var = jnp.mean(cc * cc, axis=-1, keepdims=True)
    ln = cc * lax.rsqrt(var + EPS) * lg_ref[...] + lb_ref[...]
    y_a = (_silu(ln) * sz_ref[0]).astype(BF16)
    y_b = (o_ref[0] * saz_ref[0]).astype(BF16)

    merged = (ga_ref[0] * jnp.dot(y_a, wa_ref[...], preferred_element_type=F32)
              + gb_ref[0] * jnp.dot(y_b, wb_ref[...], preferred_element_type=F32))
    x1 = x_ref[0] + jnp.dot(merged.astype(BF16), wo_ref[...], preferred_element_type=F32)

    gate = _sigmoid(jnp.dot(_rms(x1, pg_ref[...]).astype(BF16), wpg_ref[...],
                            preferred_element_type=F32))
    x2 = x1 + gate * jnp.dot(p_ref[0].astype(BF16), wpp_ref[...], preferred_element_type=F32)
    out_ref[0] = _rms(x2, fg_ref[...]) if final_norm else x2


def _pack_w_in(w):
    o_in, o_z, o_q, o_k, o_v, o_az, o_iq, o_ik, o_iw, o_ga, o_gb = (
        0, 1024, 1536, 2048, 2112, 2176, 2688, 3200, 3264, 3272, 4296)
    pad = jnp.zeros((D_MODEL, LANES - IDX_DIM - N_IDX_HEADS), w.dtype)
    cols = [w[:, o_in:o_z], w[:, o_z:o_q], w[:, o_az:o_iq],
            w[:, o_ga:o_gb], w[:, o_gb:o_gb + D_MODEL], w[:, o_k:o_az],
            w[:, o_ik:o_ga], pad]
    w_tok = jnp.concatenate(cols, axis=1).astype(BF16)
    rows = [w[:, o_q:o_k], w[:, o_iq:o_ik], w[:, o_v:o_az]]
    w_feat = jnp.concatenate(rows, axis=1).T.astype(BF16)
    return w_tok, w_feat


def _layer(x, p, norm_g, w_in, conv_w, conv_b, ln_g, ln_b, w_a, w_b, w_o,
           ple_g, w_pg, w_pp, out_g, final_norm):
    B, L, D = x.shape
    n_keep = min(TOPK_MAX, L // 4)
    cp = functools.partial(pltpu.CompilerParams, vmem_limit_bytes=VMEM_LIMIT)
    row = lambda a: a.reshape(1, -1)

    tm1 = TM_PROJ
    w_tok, w_feat = _pack_w_in(w_in)
    tok = lambda w: pl.BlockSpec((1, tm1, w), lambda b, i: (b, i, 0))
    featm = lambda r: pl.BlockSpec((1, N_HEADS, r, tm1), lambda b, i: (b, 0, 0, i))
    f32s = lambda w: jax.ShapeDtypeStruct((B, L, w), F32)
    u, sz, saz, ga, gb, qpt, kp, vt, iqt, ik, iwt = pl.pallas_call(
        functools.partial(_proj_kernel, tm=tm1),
        grid=(B, L // tm1),
        in_specs=[tok(D),
                  pl.BlockSpec((1, D), lambda b, i: (0, 0)),
                  pl.BlockSpec((D, _C_END), lambda b, i: (0, 0)),
                  pl.BlockSpec((_R_END, D), lambda b, i: (0, 0))],
        out_specs=[tok(D_CONV), tok(D_CONV), tok(D_ATTN), tok(D), tok(D),
                   featm(LANES), tok(LANES),
                   pl.BlockSpec((1, tm1 // KC, HEAD_DIM, KC), lambda b, i: (b, i, 0, 0)),
                   featm(IDX_DIM), tok(IDX_DIM),
                   pl.BlockSpec((1, N_IDX_HEADS, tm1), lambda b, i: (b, 0, i))],
        out_shape=[f32s(D_CONV), f32s(D_CONV), f32s(D_ATTN), f32s(D), f32s(D),
                   jax.ShapeDtypeStruct((B, N_HEADS, LANES, L), BF16),
                   jax.ShapeDtypeStruct((B, L, LANES), BF16),
                   jax.ShapeDtypeStruct((B, L // KC, HEAD_DIM, KC), BF16),
                   jax.ShapeDtypeStruct((B, N_IDX_HEADS, IDX_DIM, L), BF16),
                   jax.ShapeDtypeStruct((B, L, IDX_DIM), BF16),
                   jax.ShapeDtypeStruct((B, N_IDX_HEADS, L), F32)],
        compiler_params=cp(dimension_semantics=("arbitrary", "arbitrary")),
        name="proj",
    )(x, row(norm_g), w_tok, w_feat)

    o = pl.pallas_call(
        functools.partial(_attn_kernel, n_keep=n_keep),
        grid=(B, L // TQ),
        in_specs=[pl.BlockSpec((1, N_IDX_HEADS, IDX_DIM, TQ), lambda b, j: (b, 0, 0, j)),
                  pl.BlockSpec((1, L, IDX_DIM), lambda b, j: (b, 0, 0)),
                  pl.BlockSpec((1, N_IDX_HEADS, TQ), lambda b, j: (b, 0, j)),
                  pl.BlockSpec((1, N_HEADS, LANES, TQ), lambda b, j: (b, 0, 0, j)),
                  pl.BlockSpec((1, L, LANES), lambda b, j: (b, 0, 0)),
                  pl.BlockSpec((1, L // KC, HEAD_DIM, KC), lambda b, j: (b, 0, 0, 0))],
        out_specs=pl.BlockSpec((1, TQ, D_ATTN), lambda b, j: (b, j, 0)),
        out_shape=jax.ShapeDtypeStruct((B, L, D_ATTN), F32),
        scratch_shapes=[pltpu.VMEM((L, TQ), F32),
                        pltpu.VMEM((L, TQ), F32),
                        pltpu.VMEM((D_ATTN, TQ), F32),
                        pltpu.VMEM((N_HEADS, KC, TQ), F32),
                        pltpu.VMEM((N_HEADS, KC, TQ), F32),
                        pltpu.VMEM((N_HEADS, TQ), F32),
                        pltpu.VMEM((N_HEADS, TQ), F32),
                        pltpu.VMEM((N_HEADS, TQ), F32)]
                       + [pltpu.VMEM((HEAD_DIM, TQ), F32)] * N_HEADS,
        compiler_params=cp(dimension_semantics=("arbitrary", "arbitrary")),
        name="attn",
    )(iqt, ik, iwt, qpt, kp, vt)

    tm3 = TM_POST
    hb = tm3 // HALO
    tok = lambda w: pl.BlockSpec((1, tm3, w), lambda b, i: (b, i, 0))
    full = lambda a: pl.BlockSpec(a.shape, lambda b, i: (0,) * a.ndim)
    consts = [conv_w, row(conv_b), row(ln_g), row(ln_b), w_a.astype(BF16), w_b.astype(BF16),
              w_o.astype(BF16), row(ple_g), w_pg.astype(BF16), w_pp.astype(BF16), row(out_g)]
    return pl.pallas_call(
        functools.partial(_post_kernel, tm=tm3, final_norm=final_norm),
        grid=(B, L // tm3),
        in_specs=[tok(D), tok(D_CONV),
                  pl.BlockSpec((1, HALO, D_CONV), lambda b, i: (b, jnp.maximum(i * hb - 1, 0), 0)),
                  tok(D_CONV), tok(D_ATTN), tok(D_ATTN), tok(D), tok(D), tok(D_PLE)]
                 + [full(a) for a in consts],
        out_specs=tok(D),
        out_shape=jax.ShapeDtypeStruct((B, L, D), F32),
        scratch_shapes=[pltpu.VMEM((HALO + tm3, D_CONV), F32)],
        compiler_params=cp(dimension_semantics=("arbitrary", "arbitrary")),
        name="post",
    )(x, u, u, sz, o, saz, ga, gb, p, *consts)


def kernel(x, p, norm_g, w_in, conv_w, conv_b, conv_ln_g, conv_ln_b, w_a_out, w_b_out,
           w_o, ple_norm_g, w_ple_gate, w_ple_proj, final_g):
    depth = w_in.shape[0]
    for i in range(depth):
        x = _layer(x, p[i], norm_g[i], w_in[i], conv_w[i], conv_b[i], conv_ln_g[i],
                   conv_ln_b[i], w_a_out[i], w_b_out[i], w_o[i], ple_norm_g[i],
                   w_ple_gate[i], w_ple_proj[i], final_g, i == depth - 1)
    return x
```

```python
import functools
import math

import jax
import jax.numpy as jnp
import numpy as np
from jax import lax
from jax.experimental import pallas as pl
from jax.experimental.pallas import tpu as pltpu

D_MODEL = 1024
D_PLE = 256
D_CONV = 512
CONV_WIDTH = 31
N_HEADS = 8
HEAD_DIM = 64
D_ATTN = N_HEADS * HEAD_DIM
N_IDX_HEADS = 8
IDX_DIM = 64
TOPK_MAX = 256
EPS = 1e-6

LANES = 128
SUBLANES = 8
TQ = 256
KC = 256
TM_PROJ = 512
TM_POST = 256
HALO = 32
NEG = -1e30
INT_MIN = -(2 ** 31)
LOG2E = math.log2(math.e)
N_ACC_ROWS = HEAD_DIM + 16
VMEM_LIMIT = 56 * 1024 * 1024

F32 = jnp.float32
BF16 = jnp.bfloat16

_C_VAL, _C_GATE, _C_Z, _C_AZ, _C_GA, _C_GB, _C_KV, _C_IKW, _C_END = (
    0, 512, 1024, 1536, 2048, 3072, 4096, 4224, 4352)
_R_Q, _R_IQ, _R_V, _R_END = 0, 512, 1024, 1088

_NT = (((1,), (1,)), ((), ()))


def _bf16_split3(a):
    a = np.float32(a)
    p1 = np.float32(a.astype(BF16))
    p2 = np.float32(np.float32(a - p1).astype(BF16))
    p3 = np.float32(a - p1 - p2)
    assert np.float32(np.float32(p3).astype(BF16)) == p3
    return float(p1), float(p2), float(p3)


def _sigmoid(x):
    return 1.0 / (1.0 + jnp.exp(-x))


def _silu(x):
    return x * _sigmoid(x)


def _rms(x, g):
    return x * lax.rsqrt(jnp.mean(x * x, axis=-1, keepdims=True) + EPS) * g


def _proj_kernel(x_ref, g_ref, w_ref, wt_ref,
                 u_ref, sz_ref, saz_ref, ga_ref, gb_ref,
                 qpt_ref, kp_ref, vt_ref, iqt_ref, ik_ref, iwt_ref, *, tm):
    i = pl.program_id(1)
    h = _rms(x_ref[0], g_ref[...]).astype(BF16)

    def mm(a, b):
        return jnp.dot(h, w_ref[:, a:b], preferred_element_type=F32)

    def mm_t(a, b):
        return lax.dot_general(wt_ref[a:b, :], h, _NT, preferred_element_type=F32)

    u_ref[0] = mm(_C_VAL, _C_GATE) * _sigmoid(mm(_C_GATE, _C_Z))
    sz_ref[0] = _silu(mm(_C_Z, _C_AZ))
    saz_ref[0] = _silu(mm(_C_AZ, _C_GA))
    ga_ref[0] = _sigmoid(mm(_C_GA, _C_GB))
    gb_ref[0] = _sigmoid(mm(_C_GB, _C_KV))

    lane = lax.broadcasted_iota(jnp.int32, (tm, LANES), 1)
    pos = i * tm + lax.broadcasted_iota(jnp.int32, (tm, LANES), 0)
    kext = jnp.where(lane < 67, 1.0,
           jnp.where(lane < 70, (pos >> 6).astype(F32),
           jnp.where(lane < 73, (pos & 63).astype(F32), 0.0)))
    kp_ref[0] = jnp.where(lane < 64, mm(_C_KV, _C_IKW), kext).astype(BF16)

    ikw = mm(_C_IKW, _C_END)
    ik_ref[0] = ikw[:, :IDX_DIM].astype(BF16)
    iwt_ref[0] = ikw.T[IDX_DIM:IDX_DIM + N_IDX_HEADS, :] * (N_IDX_HEADS ** -0.5)

    sub = lax.broadcasted_iota(jnp.int32, (16, tm), 0)
    tpos = (i * tm + lax.broadcasted_iota(jnp.int32, (16, tm), 1)).astype(F32)
    qt = mm_t(_R_Q, _R_IQ) * (LOG2E * HEAD_DIM ** -0.5)
    for hd in range(N_HEADS):
        a = np.float32(2.0 ** (-8.0 * (hd + 1) / N_HEADS) * LOG2E)
        a1, a2, a3 = _bf16_split3(a)
        v = tpos * float(-a)
        v1 = v.astype(BF16).astype(F32)
        v2 = (v - v1).astype(BF16).astype(F32)
        v3 = v - v1 - v2
        ext = jnp.where(sub == 0, v1, jnp.where(sub == 1, v2, jnp.where(sub == 2, v3,
              jnp.where(sub == 3, 64.0 * a1, jnp.where(sub == 4, 64.0 * a2,
              jnp.where(sub == 5, 64.0 * a3, jnp.where(sub == 6, a1,
              jnp.where(sub == 7, a2, jnp.where(sub == 8, a3, 0.0)))))))))
        qpt_ref[0, hd, 0:HEAD_DIM, :] = qt[hd * HEAD_DIM:(hd + 1) * HEAD_DIM, :].astype(BF16)
        qpt_ref[0, hd, HEAD_DIM:HEAD_DIM + 16, :] = ext.astype(BF16)
        qpt_ref[0, hd, HEAD_DIM + 16:, :] = jnp.zeros((LANES - HEAD_DIM - 16, tm), BF16)

    iqt = mm_t(_R_IQ, _R_V) * (IDX_DIM ** -0.5)
    for hd in range(N_IDX_HEADS):
        iqt_ref[0, hd] = iqt[hd * IDX_DIM:(hd + 1) * IDX_DIM, :].astype(BF16)

    vt = mm_t(_R_V, _R_END).astype(BF16)
    for c in range(tm // KC):
        vt_ref[0, c] = vt[:, c * KC:(c + 1) * KC]


def _key_to_float(k):
    bits = jnp.where(k >= 0, k, k ^ jnp.int32(0x7FFFFFFF))
    f = lax.bitcast_convert_type(bits, F32)
    return jnp.where(k < 0, jnp.where(f != f, -jnp.inf, f), f)


def _attn_kernel(iqt_ref, ik_ref, iwt_ref, qpt_ref, kp_ref, vt_ref, o_ref,
                 st_ref, bias_ref, ot_ref, lga_ref, lgb_ref, m_ref, mx_ref, *acc_refs, n_keep):
    j = pl.program_id(1)
    t0 = j * TQ
    nkc = j + 1

    tcol = t0 + lax.broadcasted_iota(jnp.int32, (KC, TQ), 1)
    srow0 = lax.broadcasted_iota(jnp.int32, (KC, TQ), 0)

    def score_body(c, carry):
        ks = pl.multiple_of(c * KC, KC)
        kic = ik_ref[0, pl.ds(ks, KC), :]
        acc = jnp.zeros((KC, TQ), F32)
        for hd in range(N_IDX_HEADS):
            d = jnp.dot(kic, iqt_ref[0, hd], preferred_element_type=F32)
            acc = acc + iwt_ref[0, hd:hd + 1, :] * jnp.maximum(d, 0.0)
        st_ref[pl.ds(ks, KC), :] = jnp.where((ks + srow0) <= tcol, acc, -jnp.inf)
        return carry

    lax.fori_loop(0, nkc, score_body, 0)

    n_part = 4

    def count_ge(thr8):
        def body(c, cnt):
            ks = pl.multiple_of(c * KC, KC)
            s = st_ref[pl.ds(ks, KC), :].reshape(KC // (SUBLANES * n_part), n_part, SUBLANES, TQ)
            return cnt + jnp.sum((s >= thr8[None, None]).astype(jnp.int32), axis=0)
        cnt = lax.fori_loop(0, nkc, body, jnp.zeros((n_part, SUBLANES, TQ), jnp.int32))
        cnt8 = jnp.sum(cnt, axis=0)
        return jnp.broadcast_to(jnp.sum(cnt8, axis=0, keepdims=True), (SUBLANES, TQ))

    cnt0 = count_ge(jnp.zeros((SUBLANES, TQ), F32))
    ok0 = cnt0 >= n_keep
    key0 = jnp.where(ok0, jnp.int32(0), jnp.int32(INT_MIN))
    cnt_lo0 = jnp.where(ok0, cnt0, nkc * KC)

    def bit_body(it, carry):
        key, cnt_lo = carry
        cand = key | jnp.left_shift(jnp.int32(1), 30 - it)
        cnt = count_ge(_key_to_float(cand))
        ok = cnt >= n_keep
        return jnp.where(ok, cand, key), jnp.where(ok, cnt, cnt_lo)

    key, cnt_lo = lax.fori_loop(0, 31, bit_body, (key0, cnt_lo0))
    lo8 = _key_to_float(key)
    hi8 = _key_to_float(key + 1)
    lo8 = jnp.where(lo8 == -jnp.inf, hi8, lo8)
    lo = lo8[0:1]

    has_tie = jnp.max(jnp.where(lo8 == hi8, n_keep, cnt_lo)) > n_keep

    @pl.when(jnp.logical_not(has_tie))
    def _():
        def body(c, carry):
            ks = pl.multiple_of(c * KC, KC)
            bias_ref[pl.ds(ks, KC), :] = jnp.where(st_ref[pl.ds(ks, KC), :] >= lo, 0.0, NEG)
            return carry
        lax.fori_loop(0, nkc, body, 0)

    @pl.when(has_tie)
    def _():
        hi = hi8[0:1]
        need = (n_keep - count_ge(hi8)).astype(F32)[0:1]
        tri = (lax.broadcasted_iota(jnp.int32, (KC, KC), 0)
               >= lax.broadcasted_iota(jnp.int32, (KC, KC), 1)).astype(BF16)

        def body(c, taken):
            ks = pl.multiple_of(c * KC, KC)
            s = st_ref[pl.ds(ks, KC), :]
            above = s >= hi
            tied = jnp.where(above, 0.0, jnp.where(s >= lo, 1.0, 0.0))
            rank = taken + jnp.dot(tri, tied.astype(BF16), preferred_element_type=F32)
            pick = jnp.where(above, 1.0, jnp.where(rank <= need, tied, 0.0))
            bias_ref[pl.ds(ks, KC), :] = jnp.where(pick > 0.5, 0.0, NEG)
            return rank[KC - 1:KC, :]

        lax.fori_loop(0, nkc, body, jnp.zeros((1, TQ), F32))

    for acc_ref in acc_refs:
        acc_ref[...] = jnp.zeros(acc_ref.shape, F32)
    ones_rows = (lax.broadcasted_iota(jnp.int32, (N_ACC_ROWS - HEAD_DIM, KC), 0) == 0).astype(BF16)

    m_ref[...] = jnp.full(m_ref.shape, NEG, F32)

    def heads_step(c_qk, qk_ref, c_pv, pv_ref):
        if c_qk is not None:
            ks = pl.multiple_of(c_qk * KC, KC)
            kpc = kp_ref[0, pl.ds(ks, KC), :]
            bias = bias_ref[pl.ds(ks, KC), :]
        if c_pv is not None:
            vtc = jnp.concatenate([vt_ref[0, c_pv], ones_rows], axis=0)
            m_all, mx_cur = m_ref[...], mx_ref[...]
        m_rows, mx_rows = [], []
        for hd in range(N_HEADS):
            if c_qk is not None:
                lg = jnp.dot(kpc, qpt_ref[0, hd], preferred_element_type=F32) + bias
                qk_ref[hd] = lg
                mx_rows.append(jnp.max(lg, axis=0, keepdims=True))
            if c_pv is not None:
                m_old = m_all[hd:hd + 1, :]
                m_new = jnp.maximum(m_old, mx_cur[hd:hd + 1, :])
                alpha = jnp.exp2(m_old - m_new)
                p = jnp.exp2(pv_ref[hd] - m_new).astype(BF16)
                acc_refs[hd][...] = alpha * acc_refs[hd][...] + jnp.dot(
                    vtc, p, preferred_element_type=F32)
                m_rows.append(m_new)
        if c_pv is not None:
            m_ref[...] = jnp.concatenate(m_rows, axis=0)
        if c_qk is not None:
            mx_ref[...] = jnp.concatenate(mx_rows, axis=0)

    n_steps = nkc - 1
    heads_step(0, lga_ref, None, None)

    def pair_body(i, carry):
        heads_step(2 * i + 1, lgb_ref, 2 * i, lga_ref)
        heads_step(2 * i + 2, lga_ref, 2 * i + 1, lgb_ref)
        return carry

    lax.fori_loop(0, n_steps // 2, pair_body, 0)

    @pl.when(n_steps % 2 == 1)
    def _():
        heads_step(n_steps, lgb_ref, n_steps - 1, lga_ref)
        heads_step(None, None, n_steps, lgb_ref)

    @pl.when(n_steps % 2 == 0)
    def _():
        heads_step(None, None, n_steps, lga_ref)

    for hd in range(N_HEADS):
        acc = acc_refs[hd][...]
        ot_ref[hd * HEAD_DIM:(hd + 1) * HEAD_DIM, :] = acc[0:HEAD_DIM] / acc[HEAD_DIM:HEAD_DIM + 1]
    o_ref[0] = ot_ref[...].T


def _post_kernel(x_ref, u_ref, uh_ref, sz_ref, o_ref, saz_ref, ga_ref, gb_ref, p_ref,
                 cw_ref, cb_ref, lg_ref, lb_ref, wa_ref, wb_ref, wo_ref,
                 pg_ref, wpg_ref, wpp_ref, fg_ref, out_ref, ext_ref, es_ref, *, tm, final_norm):
    i = pl.program_id(1)
    halo = uh_ref[0]
    ext_ref[0:HALO, :] = jnp.where(i > 0, halo, jnp.zeros_like(halo))
    ext_ref[HALO:, :] = u_ref[0]

    n_sh = HALO + tm - SUBLANES
    for r in range(1, SUBLANES):
        es_ref[r - 1] = ext_ref[r:r + n_sh, :]
    base = HALO - (CONV_WIDTH - 1)
    c = jnp.zeros((tm, D_CONV), F32)
    for jj in range(CONV_WIDTH):
        row8, r = divmod(base + jj, SUBLANES)
        row = row8 * SUBLANES
        tap = ext_ref[row:row + tm, :] if r == 0 else es_ref[r - 1, row:row + tm, :]
        c = c + cw_ref[jj:jj + 1, :] * tap
    c = c + cb_ref[...]

    mu = jnp.mean(c, axis=-1, keepdims=True)
    cc = c - mu
    var = jnp.mean(cc * cc, axis=-1, keepdims=True)
    ln = cc * lax.rsqrt(var + EPS) * lg_ref[...] + lb_ref[...]
    y_a = (_silu(ln) * sz_ref[0]).astype(BF16)
    y_b = (o_ref[0] * saz_ref[0]).astype(BF16)

    merged = (ga_ref[0] * jnp.dot(y_a, wa_ref[...], preferred_element_type=F32)
              + gb_ref[0] * jnp.dot(y_b, wb_ref[...], preferred_element_type=F32))
    x1 = x_ref[0] + jnp.dot(merged.astype(BF16), wo_ref[...], preferred_element_type=F32)

    gate = _sigmoid(jnp.dot(_rms(x1, pg_ref[...]).astype(BF16), wpg_ref[...],
                            preferred_element_type=F32))
    x2 = x1 + gate * jnp.dot(p_ref[0].astype(BF16), wpp_ref[...], preferred_element_type=F32)
    out_ref[0] = _rms(x2, fg_ref[...]) if final_norm else x2


def _pack_w_in(w):
    o_in, o_z, o_q, o_k, o_v, o_az, o_iq, o_ik, o_iw, o_ga, o_gb = (
        0, 1024, 1536, 2048, 2112, 2176, 2688, 3200, 3264, 3272, 4296)
    pad = jnp.zeros((D_MODEL, LANES - IDX_DIM - N_IDX_HEADS), w.dtype)
    cols = [w[:, o_in:o_z], w[:, o_z:o_q], w[:, o_az:o_iq],
            w[:, o_ga:o_gb], w[:, o_gb:o_gb + D_MODEL], w[:, o_k:o_az],
            w[:, o_ik:o_ga], pad]
    w_tok = jnp.concatenate(cols, axis=1).astype(BF16)
    rows = [w[:, o_q:o_k], w[:, o_iq:o_ik], w[:, o_v:o_az]]
    w_feat = jnp.concatenate(rows, axis=1).T.astype(BF16)
    return w_tok, w_feat


def _layer(x, p, norm_g, w_in, conv_w, conv_b, ln_g, ln_b, w_a, w_b, w_o,
           ple_g, w_pg, w_pp, out_g, final_norm):
    B, L, D = x.shape
    n_keep = min(TOPK_MAX, L // 4)
    cp = functools.partial(pltpu.CompilerParams, vmem_limit_bytes=VMEM_LIMIT)
    row = lambda a: a.reshape(1, -1)

    tm1 = TM_PROJ
    w_tok, w_feat = _pack_w_in(w_in)
    tok = lambda w: pl.BlockSpec((1, tm1, w), lambda b, i: (b, i, 0))
    featm = lambda r: pl.BlockSpec((1, N_HEADS, r, tm1), lambda b, i: (b, 0, 0, i))
    f32s = lambda w: jax.ShapeDtypeStruct((B, L, w), F32)
    u, sz, saz, ga, gb, qpt, kp, vt, iqt, ik, iwt = pl.pallas_call(
        functools.partial(_proj_kernel, tm=tm1),
        grid=(B, L // tm1),
        in_specs=[tok(D),
                  pl.BlockSpec((1, D), lambda b, i: (0, 0)),
                  pl.BlockSpec((D, _C_END), lambda b, i: (0, 0)),
                  pl.BlockSpec((_R_END, D), lambda b, i: (0, 0))],
        out_specs=[tok(D_CONV), tok(D_CONV), tok(D_ATTN), tok(D), tok(D),
                   featm(LANES), tok(LANES),
                   pl.BlockSpec((1, tm1 // KC, HEAD_DIM, KC), lambda b, i: (b, i, 0, 0)),
                   featm(IDX_DIM), tok(IDX_DIM),
                   pl.BlockSpec((1, N_IDX_HEADS, tm1), lambda b, i: (b, 0, i))],
        out_shape=[f32s(D_CONV), f32s(D_CONV), f32s(D_ATTN), f32s(D), f32s(D),
                   jax.ShapeDtypeStruct((B, N_HEADS, LANES, L), BF16),
                   jax.ShapeDtypeStruct((B, L, LANES), BF16),
                   jax.ShapeDtypeStruct((B, L // KC, HEAD_DIM, KC), BF16),
                   jax.ShapeDtypeStruct((B, N_IDX_HEADS, IDX_DIM, L), BF16),
                   jax.ShapeDtypeStruct((B, L, IDX_DIM), BF16),
                   jax.ShapeDtypeStruct((B, N_IDX_HEADS, L), F32)],
        compiler_params=cp(dimension_semantics=("arbitrary", "arbitrary")),
        name="proj",
    )(x, row(norm_g), w_tok, w_feat)

    o = pl.pallas_call(
        functools.partial(_attn_kernel, n_keep=n_keep),
        grid=(B, L // TQ),
        in_specs=[pl.BlockSpec((1, N_IDX_HEADS, IDX_DIM, TQ), lambda b, j: (b, 0, 0, j)),
                  pl.BlockSpec((1, L, IDX_DIM), lambda b, j: (b, 0, 0)),
                  pl.BlockSpec((1, N_IDX_HEADS, TQ), lambda b, j: (b, 0, j)),
                  pl.BlockSpec((1, N_HEADS, LANES, TQ), lambda b, j: (b, 0, 0, j)),
                  pl.BlockSpec((1, L, LANES), lambda b, j: (b, 0, 0)),
                  pl.BlockSpec((1, L // KC, HEAD_DIM, KC), lambda b, j: (b, 0, 0, 0))],
        out_specs=pl.BlockSpec((1, TQ, D_ATTN), lambda b, j: (b, j, 0)),
        out_shape=jax.ShapeDtypeStruct((B, L, D_ATTN), F32),
        scratch_shapes=[pltpu.VMEM((L, TQ), F32),
                        pltpu.VMEM((L, TQ), F32),
                        pltpu.VMEM((D_ATTN, TQ), F32),
                        pltpu.VMEM((N_HEADS, KC, TQ), F32),
                        pltpu.VMEM((N_HEADS, KC, TQ), F32),
                        pltpu.VMEM((N_HEADS, TQ), F32),
                        pltpu.VMEM((N_HEADS, TQ), F32)]
                       + [pltpu.VMEM((N_ACC_ROWS, TQ), F32)] * N_HEADS,
        compiler_params=cp(dimension_semantics=("arbitrary", "arbitrary")),
        name="attn",
    )(iqt, ik, iwt, qpt, kp, vt)

    tm3 = TM_POST
    hb = tm3 // HALO
    tok = lambda w: pl.BlockSpec((1, tm3, w), lambda b, i: (b, i, 0))
    full = lambda a: pl.BlockSpec(a.shape, lambda b, i: (0,) * a.ndim)
    consts = [conv_w, row(conv_b), row(ln_g), row(ln_b), w_a.astype(BF16), w_b.astype(BF16),
              w_o.astype(BF16), row(ple_g), w_pg.astype(BF16), w_pp.astype(BF16), row(out_g)]
    return pl.pallas_call(
        functools.partial(_post_kernel, tm=tm3, final_norm=final_norm),
        grid=(B, L // tm3),
        in_specs=[tok(D), tok(D_CONV),
                  pl.BlockSpec((1, HALO, D_CONV), lambda b, i: (b, jnp.maximum(i * hb - 1, 0), 0)),
                  tok(D_CONV), tok(D_ATTN), tok(D_ATTN), tok(D), tok(D), tok(D_PLE)]
                 + [full(a) for a in consts],
        out_specs=tok(D),
        out_shape=jax.ShapeDtypeStruct((B, L, D), F32),
        scratch_shapes=[pltpu.VMEM((HALO + tm3, D_CONV), F32),
                        pltpu.VMEM((SUBLANES - 1, HALO + tm3 - SUBLANES, D_CONV), F32)],
        compiler_params=cp(dimension_semantics=("arbitrary", "arbitrary")),
        name="post",
    )(x, u, u, sz, o, saz, ga, gb, p, *consts)


def kernel(x, p, norm_g, w_in, conv_w, conv_b, conv_ln_g, conv_ln_b, w_a_out, w_b_out,
           w_o, ple_norm_g, w_ple_gate, w_ple_proj, final_g):
    depth = w_in.shape[0]
    for i in range(depth):
        x = _layer(x, p[i], norm_g[i], w_in[i], conv_w[i], conv_b[i], conv_ln_g[i],
                   conv_ln_b[i], w_a_out[i], w_b_out[i], w_o[i], ple_norm_g[i],
                   w_ple_gate[i], w_ple_proj[i], final_g, i == depth - 1)
    return x
```

```python
import functools
import math

import jax
import jax.numpy as jnp
import numpy as np
from jax import lax
from jax.experimental import pallas as pl
from jax.experimental.pallas import tpu as pltpu

D_MODEL = 1024
D_PLE = 256
D_CONV = 512
CONV_WIDTH = 31
N_HEADS = 8
HEAD_DIM = 64
D_ATTN = N_HEADS * HEAD_DIM
N_IDX_HEADS = 8
IDX_DIM = 64
TOPK_MAX = 256
EPS = 1e-6

LANES = 128
SUBLANES = 8
TQ = 256
KC = 256
TM_PROJ = 512
TM_POST = 256
HALO = 32
NEG = -1e30
FLT_MAX = float(np.finfo(np.float32).max)
BF16_ROWS = 16
WINDOW_BITS = 16
WINDOW = 2 ** WINDOW_BITS
LOG2E = math.log2(math.e)
N_ACC_ROWS = HEAD_DIM + 16
VMEM_LIMIT = 56 * 1024 * 1024

F32 = jnp.float32
BF16 = jnp.bfloat16

_C_VAL, _C_GATE, _C_Z, _C_AZ, _C_GA, _C_GB, _C_KV, _C_IKW, _C_END = (
    0, 512, 1024, 1536, 2048, 3072, 4096, 4224, 4352)
_R_Q, _R_IQ, _R_V, _R_END = 0, 512, 1024, 1088

_NT = (((1,), (1,)), ((), ()))


def _bf16_split3(a):
    a = np.float32(a)
    p1 = np.float32(a.astype(BF16))
    p2 = np.float32(np.float32(a - p1).astype(BF16))
    p3 = np.float32(a - p1 - p2)
    assert np.float32(np.float32(p3).astype(BF16)) == p3
    return float(p1), float(p2), float(p3)


def _sigmoid(x):
    return 1.0 / (1.0 + jnp.exp(-x))


def _silu(x):
    return x * _sigmoid(x)


def _rms(x, g):
    return x * lax.rsqrt(jnp.mean(x * x, axis=-1, keepdims=True) + EPS) * g


def _proj_kernel(x_ref, g_ref, w_ref, wt_ref,
                 u_ref, sz_ref, saz_ref, ga_ref, gb_ref,
                 qpt_ref, kp_ref, vt_ref, iqt_ref, ik_ref, iwt_ref, *, tm):
    i = pl.program_id(1)
    h = _rms(x_ref[0], g_ref[...]).astype(BF16)

    def mm(a, b):
        return jnp.dot(h, w_ref[:, a:b], preferred_element_type=F32)

    def mm_t(a, b):
        return lax.dot_general(wt_ref[a:b, :], h, _NT, preferred_element_type=F32)

    u_ref[0] = mm(_C_VAL, _C_GATE) * _sigmoid(mm(_C_GATE, _C_Z))
    sz_ref[0] = _silu(mm(_C_Z, _C_AZ))
    saz_ref[0] = _silu(mm(_C_AZ, _C_GA))
    ga_ref[0] = _sigmoid(mm(_C_GA, _C_GB))
    gb_ref[0] = _sigmoid(mm(_C_GB, _C_KV))

    lane = lax.broadcasted_iota(jnp.int32, (tm, LANES), 1)
    pos = i * tm + lax.broadcasted_iota(jnp.int32, (tm, LANES), 0)
    kext = jnp.where(lane < 67, 1.0,
           jnp.where(lane < 70, (pos >> 6).astype(F32),
           jnp.where(lane < 73, (pos & 63).astype(F32), 0.0)))
    kp_ref[0] = jnp.where(lane < 64, mm(_C_KV, _C_IKW), kext).astype(BF16)

    ikw = mm(_C_IKW, _C_END)
    ik_ref[0] = ikw[:, :IDX_DIM].astype(BF16)
    iwt_ref[0] = ikw.T[IDX_DIM:IDX_DIM + N_IDX_HEADS, :] * (N_IDX_HEADS ** -0.5)

    sub = lax.broadcasted_iota(jnp.int32, (16, tm), 0)
    tpos = (i * tm + lax.broadcasted_iota(jnp.int32, (16, tm), 1)).astype(F32)
    qt = mm_t(_R_Q, _R_IQ) * (LOG2E * HEAD_DIM ** -0.5)
    for hd in range(N_HEADS):
        a = np.float32(2.0 ** (-8.0 * (hd + 1) / N_HEADS) * LOG2E)
        a1, a2, a3 = _bf16_split3(a)
        v = tpos * float(-a)
        v1 = v.astype(BF16).astype(F32)
        v2 = (v - v1).astype(BF16).astype(F32)
        v3 = v - v1 - v2
        ext = jnp.where(sub == 0, v1, jnp.where(sub == 1, v2, jnp.where(sub == 2, v3,
              jnp.where(sub == 3, 64.0 * a1, jnp.where(sub == 4, 64.0 * a2,
              jnp.where(sub == 5, 64.0 * a3, jnp.where(sub == 6, a1,
              jnp.where(sub == 7, a2, jnp.where(sub == 8, a3, 0.0)))))))))
        qpt_ref[0, hd, 0:HEAD_DIM, :] = qt[hd * HEAD_DIM:(hd + 1) * HEAD_DIM, :].astype(BF16)
        qpt_ref[0, hd, HEAD_DIM:HEAD_DIM + 16, :] = ext.astype(BF16)
        qpt_ref[0, hd, HEAD_DIM + 16:, :] = jnp.zeros((LANES - HEAD_DIM - 16, tm), BF16)

    iqt = mm_t(_R_IQ, _R_V) * (IDX_DIM ** -0.5)
    for hd in range(N_IDX_HEADS):
        iqt_ref[0, hd] = iqt[hd * IDX_DIM:(hd + 1) * IDX_DIM, :].astype(BF16)

    vt = mm_t(_R_V, _R_END).astype(BF16)
    for c in range(tm // KC):
        vt_ref[0, c] = vt[:, c * KC:(c + 1) * KC]


def _key_to_float(k):
    bits = jnp.where(k >= 0, k, k ^ jnp.int32(0x7FFFFFFF))
    f = lax.bitcast_convert_type(bits, F32)
    return jnp.where(k < 0, jnp.where(f != f, -jnp.inf, f), f)


def _key16_to_float(k):
    bits = jnp.left_shift(jnp.where(k >= 0, k, k ^ jnp.int32(0x7FFF)), 16)
    f = lax.bitcast_convert_type(bits, F32)
    return jnp.where(k < 0, jnp.where(f != f, -jnp.inf, f), f)


def _attn_kernel(iqt_ref, ik_ref, iwt_ref, qpt_ref, kp_ref, vt_ref, o_ref,
                 st_ref, rt_ref, key_ref, bias_ref, ot_ref, lga_ref, lgb_ref, m_ref, mx_ref,
                 *acc_refs, n_keep):
    j = pl.program_id(1)
    t0 = j * TQ
    nkc = j + 1

    tcol = t0 + lax.broadcasted_iota(jnp.int32, (KC, TQ), 1)
    srow0 = lax.broadcasted_iota(jnp.int32, (KC, TQ), 0)

    def score_body(c, carry):
        ks = pl.multiple_of(c * KC, KC)
        kic = ik_ref[0, pl.ds(ks, KC), :]
        acc = jnp.zeros((KC, TQ), F32)
        for hd in range(N_IDX_HEADS):
            d = jnp.dot(kic, iqt_ref[0, hd], preferred_element_type=F32)
            acc = acc + iwt_ref[0, hd:hd + 1, :] * jnp.maximum(d, 0.0)
        masked = jnp.where((ks + srow0) <= tcol, acc, -jnp.inf)
        st_ref[pl.ds(ks, KC), :] = masked
        rt_ref[pl.ds(ks, KC), :] = masked.astype(BF16)
        return carry

    lax.fori_loop(0, nkc, score_body, 0)

    n_part = 4

    def count_ge(thr8):
        def body(c, cnt):
            ks = pl.multiple_of(c * KC, KC)
            s = st_ref[pl.ds(ks, KC), :].reshape(KC // (SUBLANES * n_part), n_part, SUBLANES, TQ)
            return cnt + jnp.sum((s >= thr8[None, None]).astype(jnp.int32), axis=0)
        cnt = lax.fori_loop(0, nkc, body, jnp.zeros((n_part, SUBLANES, TQ), jnp.int32))
        cnt8 = jnp.sum(cnt, axis=0)
        return jnp.broadcast_to(jnp.sum(cnt8, axis=0, keepdims=True), (SUBLANES, TQ))

    def count_ge16(thr_bf):
        def body(c, cnt):
            ks = pl.multiple_of(c * KC, KC)
            r = rt_ref[pl.ds(ks, KC), :].reshape(KC // (BF16_ROWS * 2), 2, BF16_ROWS, TQ)
            w = jnp.where(r >= thr_bf[None, None], jnp.ones((), BF16), jnp.zeros((), BF16))
            part = w[0]
            for g in range(1, w.shape[0]):
                part = part + w[g]
            return cnt + part
        cnt = lax.fori_loop(0, nkc, body, jnp.zeros((2, BF16_ROWS, TQ), BF16)).astype(F32)
        tot = jnp.sum(cnt[0] + cnt[1], axis=0, keepdims=True)
        return jnp.broadcast_to(tot, (SUBLANES, TQ))

    def thr16(k16):
        f = _key16_to_float(k16)
        return jnp.concatenate([f] * (BF16_ROWS // SUBLANES), axis=0).astype(BF16)

    k16 = jnp.where(count_ge16(thr16(jnp.zeros((SUBLANES, TQ), jnp.int32))) >= n_keep,
                    jnp.int32(0), jnp.int32(-(2 ** 15)))

    def bit16_body(it, k):
        cand = k | jnp.left_shift(jnp.int32(1), 14 - it)
        return jnp.where(count_ge16(thr16(cand)) >= n_keep, cand, k)

    k16 = lax.fori_loop(0, 15, bit16_body, k16)
    hb_bits = lax.bitcast_convert_type(_key16_to_float(k16), jnp.int32)
    k_hb = hb_bits ^ ((hb_bits >> 31) & jnp.int32(0x7FFFFFFF))
    k_base = k_hb - WINDOW
    w_lo = jnp.maximum(_key_to_float(k_base), -FLT_MAX)
    w_hi = _key_to_float(k_base + 2 * WINDOW)

    def gather_body(c, carry):
        m1, m2, m3, ovf, above = carry
        ks = pl.multiple_of(c * KC, KC)
        s = st_ref[pl.ds(ks, KC), :].reshape(KC // (SUBLANES * n_part), n_part, SUBLANES, TQ)
        for g in range(s.shape[0]):
            sg = s[g]
            ge_hi = sg >= w_hi[None]
            above = above + ge_hi.astype(jnp.int32)
            v = jnp.where(ge_hi, -jnp.inf, jnp.where(sg >= w_lo[None], sg, -jnp.inf))
            t1 = jnp.minimum(m1, v)
            m1 = jnp.maximum(m1, v)
            t2 = jnp.minimum(m2, t1)
            m2 = jnp.maximum(m2, t1)
            ovf = jnp.maximum(ovf, jnp.minimum(m3, t2))
            m3 = jnp.maximum(m3, t2)
        return m1, m2, m3, ovf, above

    ninf = jnp.full((n_part, SUBLANES, TQ), -jnp.inf, F32)
    m1, m2, m3, ovf, above = lax.fori_loop(
        0, nkc, gather_body, (ninf, ninf, ninf, ninf, jnp.zeros((n_part, SUBLANES, TQ), jnp.int32)))
    gathered = jnp.concatenate([m1, m2, m3], axis=0)
    above8 = jnp.sum(above, axis=0)
    above8 = jnp.broadcast_to(jnp.sum(above8, axis=0, keepdims=True), (SUBLANES, TQ))

    def count_ge_gathered(thr8):
        c8 = jnp.sum((gathered >= thr8[None]).astype(jnp.int32), axis=0)
        return above8 + jnp.broadcast_to(jnp.sum(c8, axis=0, keepdims=True), (SUBLANES, TQ))

    def refine(count_fn):
        cnt0 = count_fn(_key_to_float(k_base))
        def body(it, carry):
            off, cnt_lo = carry
            cand = off | jnp.left_shift(jnp.int32(1), WINDOW_BITS - it)
            cnt = count_fn(_key_to_float(k_base + cand))
            ok = cnt >= n_keep
            return jnp.where(ok, cand, off), jnp.where(ok, cnt, cnt_lo)
        off, cnt_lo = lax.fori_loop(0, WINDOW_BITS + 1, body,
                                    (jnp.zeros((SUBLANES, TQ), jnp.int32), cnt0))
        key_ref[0] = k_base + off
        key_ref[1] = cnt_lo

    refine(count_ge_gathered)

    @pl.when(jnp.max(ovf) > -jnp.inf)
    def _():
        refine(count_ge)

    key = key_ref[0]
    cnt_lo = key_ref[1]
    lo8 = _key_to_float(key)
    hi8 = _key_to_float(key + 1)
    short = lo8 == -jnp.inf
    lo8 = jnp.where(short, -FLT_MAX, lo8)
    lo = lo8[0:1]

    has_tie = jnp.max(jnp.where(short, n_keep, cnt_lo)) > n_keep

    @pl.when(jnp.logical_not(has_tie))
    def _():
        def body(c, carry):
            ks = pl.multiple_of(c * KC, KC)
            bias_ref[pl.ds(ks, KC), :] = jnp.where(st_ref[pl.ds(ks, KC), :] >= lo, 0.0, NEG)
            return carry
        lax.fori_loop(0, nkc, body, 0)

    @pl.when(has_tie)
    def _():
        hi = hi8[0:1]
        need = (n_keep - count_ge(hi8)).astype(F32)[0:1]
        tri = (lax.broadcasted_iota(jnp.int32, (KC, KC), 0)
               >= lax.broadcasted_iota(jnp.int32, (KC, KC), 1)).astype(BF16)

        def body(c, taken):
            ks = pl.multiple_of(c * KC, KC)
            s = st_ref[pl.ds(ks, KC), :]
            above = s >= hi
            tied = jnp.where(above, 0.0, jnp.where(s >= lo, 1.0, 0.0))
            rank = taken + jnp.dot(tri, tied.astype(BF16), preferred_element_type=F32)
            pick = jnp.where(above, 1.0, jnp.where(rank <= need, tied, 0.0))
            bias_ref[pl.ds(ks, KC), :] = jnp.where(pick > 0.5, 0.0, NEG)
            return rank[KC - 1:KC, :]

        lax.fori_loop(0, nkc, body, jnp.zeros((1, TQ), F32))

    for acc_ref in acc_refs:
        acc_ref[...] = jnp.zeros(acc_ref.shape, F32)
    ones_rows = (lax.broadcasted_iota(jnp.int32, (N_ACC_ROWS - HEAD_DIM, KC), 0) == 0).astype(BF16)

    m_ref[...] = jnp.full(m_ref.shape, NEG, F32)

    def heads_step(c_qk, qk_ref, c_pv, pv_ref):
        if c_qk is not None:
            ks = pl.multiple_of(c_qk * KC, KC)
            kpc = kp_ref[0, pl.ds(ks, KC), :]
            bias = bias_ref[pl.ds(ks, KC), :]
        if c_pv is not None:
            vtc = jnp.concatenate([vt_ref[0, c_pv], ones_rows], axis=0)
            m_all, mx_cur = m_ref[...], mx_ref[...]
        m_rows, mx_rows = [], []
        for hd in range(N_HEADS):
            if c_qk is not None:
                lg = jnp.dot(kpc, qpt_ref[0, hd], preferred_element_type=F32) + bias
                qk_ref[hd] = lg
                mx_rows.append(jnp.max(lg, axis=0, keepdims=True))
            if c_pv is not None:
                m_old = m_all[hd:hd + 1, :]
                m_new = jnp.maximum(m_old, mx_cur[hd:hd + 1, :])
                alpha = jnp.exp2(m_old - m_new)
                p = jnp.exp2(pv_ref[hd] - m_new).astype(BF16)
                acc_refs[hd][...] = alpha * acc_refs[hd][...] + jnp.dot(
                    vtc, p, preferred_element_type=F32)
                m_rows.append(m_new)
        if c_pv is not None:
            m_ref[...] = jnp.concatenate(m_rows, axis=0)
        if c_qk is not None:
            mx_ref[...] = jnp.concatenate(mx_rows, axis=0)

    n_steps = nkc - 1
    heads_step(0, lga_ref, None, None)

    def pair_body(i, carry):
        heads_step(2 * i + 1, lgb_ref, 2 * i, lga_ref)
        heads_step(2 * i + 2, lga_ref, 2 * i + 1, lgb_ref)
        return carry

    lax.fori_loop(0, n_steps // 2, pair_body, 0)

    @pl.when(n_steps % 2 == 1)
    def _():
        heads_step(n_steps, lgb_ref, n_steps - 1, lga_ref)
        heads_step(None, None, n_steps, lgb_ref)

    @pl.when(n_steps % 2 == 0)
    def _():
        heads_step(None, None, n_steps, lga_ref)

    for hd in range(N_HEADS):
        acc = acc_refs[hd][...]
        ot_ref[hd * HEAD_DIM:(hd + 1) * HEAD_DIM, :] = acc[0:HEAD_DIM] / acc[HEAD_DIM:HEAD_DIM + 1]
    o_ref[0] = ot_ref[...].T


def _post_kernel(x_ref, u_ref, uh_ref, sz_ref, o_ref, saz_ref, ga_ref, gb_ref, p_ref,
                 cw_ref, cb_ref, lg_ref, lb_ref, wa_ref, wb_ref, wo_ref,
                 pg_ref, wpg_ref, wpp_ref, fg_ref, out_ref, ext_ref, es_ref, *, tm, final_norm):
    i = pl.program_id(1)
    halo = uh_ref[0]
    ext_ref[0:HALO, :] = jnp.where(i > 0, halo, jnp.zeros_like(halo))
    ext_ref[HALO:, :] = u_ref[0]

    n_sh = HALO + tm - SUBLANES
    for r in range(1, SUBLANES):
        es_ref[r - 1] = ext_ref[r:r + n_sh, :]
    base = HALO - (CONV_WIDTH - 1)
    c = jnp.zeros((tm, D_CONV), F32)
    for jj in range(CONV_WIDTH):
        row8, r = divmod(base + jj, SUBLANES)
        row = row8 * SUBLANES
        tap = ext_ref[row:row + tm, :] if r == 0 else es_ref[r - 1, row:row + tm, :]
        c = c + cw_ref[jj:jj + 1, :] * tap
    c = c + cb_ref[...]

    mu = jnp.mean(c, axis=-1, keepdims=True)
    cc = c - mu
    var = jnp.mean(cc * cc, axis=-1, keepdims=True)
    ln = cc * lax.rsqrt(var + EPS) * lg_ref[...] + lb_ref[...]
    y_a = (_silu(ln) * sz_ref[0]).astype(BF16)
    y_b = (o_ref[0] * saz_ref[0]).astype(BF16)

    merged = (ga_ref[0] * jnp.dot(y_a, wa_ref[...], preferred_element_type=F32)
              + gb_ref[0] * jnp.dot(y_b, wb_ref[...], preferred_element_type=F32))
    x1 = x_ref[0] + jnp.dot(merged.astype(BF16), wo_ref[...], preferred_element_type=F32)

    gate = _sigmoid(jnp.dot(_rms(x1, pg_ref[...]).astype(BF16), wpg_ref[...],
                            preferred_element_type=F32))
    x2 = x1 + gate * jnp.dot(p_ref[0].astype(BF16), wpp_ref[...], preferred_element_type=F32)
    out_ref[0] = _rms(x2, fg_ref[...]) if final_norm else x2


def _pack_w_in(w):
    o_in, o_z, o_q, o_k, o_v, o_az, o_iq, o_ik, o_iw, o_ga, o_gb = (
        0, 1024, 1536, 2048, 2112, 2176, 2688, 3200, 3264, 3272, 4296)
    pad = jnp.zeros((D_MODEL, LANES - IDX_DIM - N_IDX_HEADS), w.dtype)
    cols = [w[:, o_in:o_z], w[:, o_z:o_q], w[:, o_az:o_iq],
            w[:, o_ga:o_gb], w[:, o_gb:o_gb + D_MODEL], w[:, o_k:o_az],
            w[:, o_ik:o_ga], pad]
    w_tok = jnp.concatenate(cols, axis=1).astype(BF16)
    rows = [w[:, o_q:o_k], w[:, o_iq:o_ik], w[:, o_v:o_az]]
    w_feat = jnp.concatenate(rows, axis=1).T.astype(BF16)
    return w_tok, w_feat


def _layer(x, p, norm_g, w_in, conv_w, conv_b, ln_g, ln_b, w_a, w_b, w_o,
           ple_g, w_pg, w_pp, out_g, final_norm):
    B, L, D = x.shape
    n_keep = min(TOPK_MAX, L // 4)
    cp = functools.partial(pltpu.CompilerParams, vmem_limit_bytes=VMEM_LIMIT)
    row = lambda a: a.reshape(1, -1)

    tm1 = TM_PROJ
    w_tok, w_feat = _pack_w_in(w_in)
    tok = lambda w: pl.BlockSpec((1, tm1, w), lambda b, i: (b, i, 0))
    featm = lambda r: pl.BlockSpec((1, N_HEADS, r, tm1), lambda b, i: (b, 0, 0, i))
    f32s = lambda w: jax.ShapeDtypeStruct((B, L, w), F32)
    u, sz, saz, ga, gb, qpt, kp, vt, iqt, ik, iwt = pl.pallas_call(
        functools.partial(_proj_kernel, tm=tm1),
        grid=(B, L // tm1),
        in_specs=[tok(D),
                  pl.BlockSpec((1, D), lambda b, i: (0, 0)),
                  pl.BlockSpec((D, _C_END), lambda b, i: (0, 0)),
                  pl.BlockSpec((_R_END, D), lambda b, i: (0, 0))],
        out_specs=[tok(D_CONV), tok(D_CONV), tok(D_ATTN), tok(D), tok(D),
                   featm(LANES), tok(LANES),
                   pl.BlockSpec((1, tm1 // KC, HEAD_DIM, KC), lambda b, i: (b, i, 0, 0)),
                   featm(IDX_DIM), tok(IDX_DIM),
                   pl.BlockSpec((1, N_IDX_HEADS, tm1), lambda b, i: (b, 0, i))],
        out_shape=[f32s(D_CONV), f32s(D_CONV), f32s(D_ATTN), f32s(D), f32s(D),
                   jax.ShapeDtypeStruct((B, N_HEADS, LANES, L), BF16),
                   jax.ShapeDtypeStruct((B, L, LANES), BF16),
                   jax.ShapeDtypeStruct((B, L // KC, HEAD_DIM, KC), BF16),
                   jax.ShapeDtypeStruct((B, N_IDX_HEADS, IDX_DIM, L), BF16),
                   jax.ShapeDtypeStruct((B, L, IDX_DIM), BF16),
                   jax.ShapeDtypeStruct((B, N_IDX_HEADS, L), F32)],
        compiler_params=cp(dimension_semantics=("arbitrary", "arbitrary")),
        name="proj",
    )(x, row(norm_g), w_tok, w_feat)

    o = pl.pallas_call(
        functools.partial(_attn_kernel, n_keep=n_keep),
        grid=(B, L // TQ),
        in_specs=[pl.BlockSpec((1, N_IDX_HEADS, IDX_DIM, TQ), lambda b, j: (b, 0, 0, j)),
                  pl.BlockSpec((1, L, IDX_DIM), lambda b, j: (b, 0, 0)),
                  pl.BlockSpec((1, N_IDX_HEADS, TQ), lambda b, j: (b, 0, j)),
                  pl.BlockSpec((1, N_HEADS, LANES, TQ), lambda b, j: (b, 0, 0, j)),
                  pl.BlockSpec((1, L, LANES), lambda b, j: (b, 0, 0)),
                  pl.BlockSpec((1, L // KC, HEAD_DIM, KC), lambda b, j: (b, 0, 0, 0))],
        out_specs=pl.BlockSpec((1, TQ, D_ATTN), lambda b, j: (b, j, 0)),
        out_shape=jax.ShapeDtypeStruct((B, L, D_ATTN), F32),
        scratch_shapes=[pltpu.VMEM((L, TQ), F32),
                        pltpu.VMEM((L, TQ), BF16),
                        pltpu.VMEM((2, SUBLANES, TQ), jnp.int32),
                        pltpu.VMEM((L, TQ), F32),
                        pltpu.VMEM((D_ATTN, TQ), F32),
                        pltpu.VMEM((N_HEADS, KC, TQ), F32),
                        pltpu.VMEM((N_HEADS, KC, TQ), F32),
                        pltpu.VMEM((N_HEADS, TQ), F32),
                        pltpu.VMEM((N_HEADS, TQ), F32)]
                       + [pltpu.VMEM((N_ACC_ROWS, TQ), F32)] * N_HEADS,
        compiler_params=cp(dimension_semantics=("arbitrary", "arbitrary")),
        name="attn",
    )(iqt, ik, iwt, qpt, kp, vt)

    tm3 = TM_POST
    hb = tm3 // HALO
    tok = lambda w: pl.BlockSpec((1, tm3, w), lambda b, i: (b, i, 0))
    full = lambda a: pl.BlockSpec(a.shape, lambda b, i: (0,) * a.ndim)
    consts = [conv_w, row(conv_b), row(ln_g), row(ln_b), w_a.astype(BF16), w_b.astype(BF16),
              w_o.astype(BF16), row(ple_g), w_pg.astype(BF16), w_pp.astype(BF16), row(out_g)]
    return pl.pallas_call(
        functools.partial(_post_kernel, tm=tm3, final_norm=final_norm),
        grid=(B, L // tm3),
        in_specs=[tok(D), tok(D_CONV),
                  pl.BlockSpec((1, HALO, D_CONV), lambda b, i: (b, jnp.maximum(i * hb - 1, 0), 0)),
                  tok(D_CONV), tok(D_ATTN), tok(D_ATTN), tok(D), tok(D), tok(D_PLE)]
                 + [full(a) for a in consts],
        out_specs=tok(D),
        out_shape=jax.ShapeDtypeStruct((B, L, D), F32),
        scratch_shapes=[pltpu.VMEM((HALO + tm3, D_CONV), F32),
                        pltpu.VMEM((SUBLANES - 1, HALO + tm3 - SUBLANES, D_CONV), F32)],
        compiler_params=cp(dimension_semantics=("arbitrary", "arbitrary")),
        name="post",
    )(x, u, u, sz, o, saz, ga, gb, p, *consts)


def kernel(x, p, norm_g, w_in, conv_w, conv_b, conv_ln_g, conv_ln_b, w_a_out, w_b_out,
           w_o, ple_norm_g, w_ple_gate, w_ple_proj, final_g):
    depth = w_in.shape[0]
    for i in range(depth):
        x = _layer(x, p[i], norm_g[i], w_in[i], conv_w[i], conv_b[i], conv_ln_g[i],
                   conv_ln_b[i], w_a_out[i], w_b_out[i], w_o[i], ple_norm_g[i],
                   w_ple_gate[i], w_ple_proj[i], final_g, i == depth - 1)
    return x
```

```python
import functools
import math

import jax
import jax.numpy as jnp
import numpy as np
from jax import lax
from jax.experimental import pallas as pl
from jax.experimental.pallas import tpu as pltpu

D_MODEL = 1024
D_PLE = 256
D_CONV = 512
CONV_WIDTH = 31
N_HEADS = 8
HEAD_DIM = 64
D_ATTN = N_HEADS * HEAD_DIM
N_IDX_HEADS = 8
IDX_DIM = 64
TOPK_MAX = 256
EPS = 1e-6

LANES = 128
SUBLANES = 8
TQ = 256
KC = 256
TM_PROJ = 512
TM_POST = 512
HALO = 32
NEG = -1e30
FLT_MAX = float(np.finfo(np.float32).max)
BF16_ROWS = 16
WINDOW_BITS = 16
WINDOW = 2 ** WINDOW_BITS
LOG2E = math.log2(math.e)
N_ACC_ROWS = HEAD_DIM + 16
VMEM_LIMIT = 56 * 1024 * 1024

F32 = jnp.float32
BF16 = jnp.bfloat16

_C_VAL, _C_GATE, _C_Z, _C_AZ, _C_GA, _C_GB, _C_KV, _C_IKW, _C_END = (
    0, 512, 1024, 1536, 2048, 3072, 4096, 4224, 4352)
_R_Q, _R_IQ, _R_V, _R_END = 0, 512, 1024, 1088

_NT = (((1,), (1,)), ((), ()))


def _bf16_split3(a):
    a = np.float32(a)
    p1 = np.float32(a.astype(BF16))
    p2 = np.float32(np.float32(a - p1).astype(BF16))
    p3 = np.float32(a - p1 - p2)
    assert np.float32(np.float32(p3).astype(BF16)) == p3
    return float(p1), float(p2), float(p3)


def _sigmoid(x):
    return 1.0 / (1.0 + jnp.exp(-x))


def _silu(x):
    return x * _sigmoid(x)


def _rms(x, g):
    return x * lax.rsqrt(jnp.mean(x * x, axis=-1, keepdims=True) + EPS) * g


def _proj_kernel(x_ref, g_ref, w_ref, wt_ref, cw_ref, cb_ref, lng_ref, lnb_ref,
                 ya_ref, saz_ref, ga_ref, gb_ref,
                 qpt_ref, kp_ref, vt_ref, iqt_ref, ik_ref, iwt_ref,
                 ext_ref, es_ref, carry_ref, *, tm):
    i = pl.program_id(1)
    h = _rms(x_ref[0], g_ref[...]).astype(BF16)

    def mm(a, b):
        return jnp.dot(h, w_ref[:, a:b], preferred_element_type=F32)

    def mm_t(a, b):
        return lax.dot_general(wt_ref[a:b, :], h, _NT, preferred_element_type=F32)

    u = mm(_C_VAL, _C_GATE) * _sigmoid(mm(_C_GATE, _C_Z))
    prev = carry_ref[...]
    ext_ref[0:HALO, :] = jnp.where(i > 0, prev, jnp.zeros_like(prev))
    ext_ref[HALO:, :] = u
    carry_ref[...] = u[tm - HALO:, :]
    n_sh = HALO + tm - SUBLANES
    for r in range(1, SUBLANES):
        es_ref[r - 1] = ext_ref[r:r + n_sh, :]

    def w_conformer():
        base = HALO - (CONV_WIDTH - 1)
        c = jnp.zeros((tm, D_CONV), F32)
        for jj in range(CONV_WIDTH):
            row8, r = divmod(base + jj, SUBLANES)
            row = row8 * SUBLANES
            tap = ext_ref[row:row + tm, :] if r == 0 else es_ref[r - 1, row:row + tm, :]
            c = c + cw_ref[jj:jj + 1, :] * tap
        c = c + cb_ref[...]
        mu = jnp.mean(c, axis=-1, keepdims=True)
        cc = c - mu
        var = jnp.mean(cc * cc, axis=-1, keepdims=True)
        ln = cc * lax.rsqrt(var + EPS) * lng_ref[...] + lnb_ref[...]
        ya_ref[0] = (_silu(ln) * _silu(mm(_C_Z, _C_AZ))).astype(BF16)

    def w_saz():
        saz_ref[0] = _silu(mm(_C_AZ, _C_GA))

    def w_gate(ref, c0, half):
        def run():
            lo = half * (D_MODEL // 2)
            ref[0, :, lo:lo + D_MODEL // 2] = _sigmoid(mm(c0 + lo, c0 + lo + D_MODEL // 2))
        return run

    def w_keys():
        lane = lax.broadcasted_iota(jnp.int32, (tm, LANES), 1)
        pos = i * tm + lax.broadcasted_iota(jnp.int32, (tm, LANES), 0)
        kext = jnp.where(lane < 67, 1.0,
               jnp.where(lane < 70, (pos >> 6).astype(F32),
               jnp.where(lane < 73, (pos & 63).astype(F32), 0.0)))
        kp_ref[0] = jnp.where(lane < 64, mm(_C_KV, _C_IKW), kext).astype(BF16)
        ikw = mm(_C_IKW, _C_END)
        ik_ref[0] = ikw[:, :IDX_DIM].astype(BF16)
        iwt_ref[0] = ikw.T[IDX_DIM:IDX_DIM + N_IDX_HEADS, :] * (N_IDX_HEADS ** -0.5)

    def w_queries():
        sub = lax.broadcasted_iota(jnp.int32, (16, tm), 0)
        tpos = (i * tm + lax.broadcasted_iota(jnp.int32, (16, tm), 1)).astype(F32)
        qt = mm_t(_R_Q, _R_IQ) * (LOG2E * HEAD_DIM ** -0.5)
        for hd in range(N_HEADS):
            a = np.float32(2.0 ** (-8.0 * (hd + 1) / N_HEADS) * LOG2E)
            a1, a2, a3 = _bf16_split3(a)
            v = tpos * float(-a)
            v1 = v.astype(BF16).astype(F32)
            v2 = (v - v1).astype(BF16).astype(F32)
            v3 = v - v1 - v2
            ext = jnp.where(sub == 0, v1, jnp.where(sub == 1, v2, jnp.where(sub == 2, v3,
                  jnp.where(sub == 3, 64.0 * a1, jnp.where(sub == 4, 64.0 * a2,
                  jnp.where(sub == 5, 64.0 * a3, jnp.where(sub == 6, a1,
                  jnp.where(sub == 7, a2, jnp.where(sub == 8, a3, 0.0)))))))))
            qpt_ref[0, hd, 0:HEAD_DIM, :] = qt[hd * HEAD_DIM:(hd + 1) * HEAD_DIM, :].astype(BF16)
            qpt_ref[0, hd, HEAD_DIM:HEAD_DIM + 16, :] = ext.astype(BF16)
            qpt_ref[0, hd, HEAD_DIM + 16:, :] = jnp.zeros((LANES - HEAD_DIM - 16, tm), BF16)

    def w_index_values():
        iqt = mm_t(_R_IQ, _R_V) * (IDX_DIM ** -0.5)
        for hd in range(N_IDX_HEADS):
            iqt_ref[0, hd] = iqt[hd * IDX_DIM:(hd + 1) * IDX_DIM, :].astype(BF16)
        vt = mm_t(_R_V, _R_END).astype(BF16)
        for c in range(tm // KC):
            vt_ref[0, c] = vt[:, c * KC:(c + 1) * KC]

    for work in (w_conformer, w_saz, w_gate(ga_ref, _C_GA, 0), w_gate(ga_ref, _C_GA, 1),
                 w_gate(gb_ref, _C_GB, 0), w_gate(gb_ref, _C_GB, 1), w_keys, w_queries,
                 w_index_values):
        work()


def _key_to_float(k):
    bits = jnp.where(k >= 0, k, k ^ jnp.int32(0x7FFFFFFF))
    f = lax.bitcast_convert_type(bits, F32)
    return jnp.where(k < 0, jnp.where(f != f, -jnp.inf, f), f)


def _key16_to_float(k):
    bits = jnp.left_shift(jnp.where(k >= 0, k, k ^ jnp.int32(0x7FFF)), 16)
    f = lax.bitcast_convert_type(bits, F32)
    return jnp.where(k < 0, jnp.where(f != f, -jnp.inf, f), f)


def _attn_kernel(iqt_ref, ik_ref, iwt_ref, qpt_ref, kp_ref, vt_ref, saz_ref, yb_ref,
                 st_ref, rt_ref, key_ref, bias_ref, ot_ref, lga_ref, lgb_ref, m_ref, mx_ref,
                 *acc_refs, n_keep):
    j = pl.program_id(1)
    t0 = j * TQ
    nkc = j + 1

    tcol = t0 + lax.broadcasted_iota(jnp.int32, (KC, TQ), 1)
    srow0 = lax.broadcasted_iota(jnp.int32, (KC, TQ), 0)

    def score_body(c, carry):
        ks = pl.multiple_of(c * KC, KC)
        kic = ik_ref[0, pl.ds(ks, KC), :]
        acc = jnp.zeros((KC, TQ), F32)
        for hd in range(N_IDX_HEADS):
            d = jnp.dot(kic, iqt_ref[0, hd], preferred_element_type=F32)
            acc = acc + iwt_ref[0, hd:hd + 1, :] * jnp.maximum(d, 0.0)
        masked = jnp.where((ks + srow0) <= tcol, acc, -jnp.inf)
        st_ref[pl.ds(ks, KC), :] = masked
        rt_ref[pl.ds(ks, KC), :] = masked.astype(BF16)
        return carry

    lax.fori_loop(0, nkc, score_body, 0)

    n_part = 4

    def count_ge(thr8):
        def body(c, cnt):
            ks = pl.multiple_of(c * KC, KC)
            s = st_ref[pl.ds(ks, KC), :].reshape(KC // (SUBLANES * n_part), n_part, SUBLANES, TQ)
            return cnt + jnp.sum((s >= thr8[None, None]).astype(jnp.int32), axis=0)
        cnt = lax.fori_loop(0, nkc, body, jnp.zeros((n_part, SUBLANES, TQ), jnp.int32))
        cnt8 = jnp.sum(cnt, axis=0)
        return jnp.broadcast_to(jnp.sum(cnt8, axis=0, keepdims=True), (SUBLANES, TQ))

    def count_ge16(thr_bf):
        def body(c, cnt):
            ks = pl.multiple_of(c * KC, KC)
            r = rt_ref[pl.ds(ks, KC), :].reshape(KC // (BF16_ROWS * 2), 2, BF16_ROWS, TQ)
            w = jnp.where(r >= thr_bf[None, None], jnp.ones((), BF16), jnp.zeros((), BF16))
            part = w[0]
            for g in range(1, w.shape[0]):
                part = part + w[g]
            return cnt + part
        cnt = lax.fori_loop(0, nkc, body, jnp.zeros((2, BF16_ROWS, TQ), BF16)).astype(F32)
        tot = jnp.sum(cnt[0] + cnt[1], axis=0, keepdims=True)
        return jnp.broadcast_to(tot, (SUBLANES, TQ))

    def thr16(k16):
        f = _key16_to_float(k16)
        return jnp.concatenate([f] * (BF16_ROWS // SUBLANES), axis=0).astype(BF16)

    k16 = jnp.where(count_ge16(thr16(jnp.zeros((SUBLANES, TQ), jnp.int32))) >= n_keep,
                    jnp.int32(0), jnp.int32(-(2 ** 15)))

    def bit16_body(it, k):
        cand = k | jnp.left_shift(jnp.int32(1), 14 - it)
        return jnp.where(count_ge16(thr16(cand)) >= n_keep, cand, k)

    k16 = lax.fori_loop(0, 15, bit16_body, k16)
    hb_bits = lax.bitcast_convert_type(_key16_to_float(k16), jnp.int32)
    k_hb = hb_bits ^ ((hb_bits >> 31) & jnp.int32(0x7FFFFFFF))
    k_base = k_hb - WINDOW
    w_lo = jnp.maximum(_key_to_float(k_base), -FLT_MAX)
    w_hi = _key_to_float(k_base + 2 * WINDOW)

    def gather_body(c, carry):
        m1, m2, m3, ovf, above = carry
        ks = pl.multiple_of(c * KC, KC)
        s = st_ref[pl.ds(ks, KC), :].reshape(KC // (SUBLANES * n_part), n_part, SUBLANES, TQ)
        for g in range(s.shape[0]):
            sg = s[g]
            ge_hi = sg >= w_hi[None]
            above = above + ge_hi.astype(jnp.int32)
            v = jnp.where(ge_hi, -jnp.inf, jnp.where(sg >= w_lo[None], sg, -jnp.inf))
            t1 = jnp.minimum(m1, v)
            m1 = jnp.maximum(m1, v)
            t2 = jnp.minimum(m2, t1)
            m2 = jnp.maximum(m2, t1)
            ovf = jnp.maximum(ovf, jnp.minimum(m3, t2))
            m3 = jnp.maximum(m3, t2)
        return m1, m2, m3, ovf, above

    ninf = jnp.full((n_part, SUBLANES, TQ), -jnp.inf, F32)
    m1, m2, m3, ovf, above = lax.fori_loop(
        0, nkc, gather_body, (ninf, ninf, ninf, ninf, jnp.zeros((n_part, SUBLANES, TQ), jnp.int32)))
    gathered = jnp.concatenate([m1, m2, m3], axis=0)
    above8 = jnp.sum(above, axis=0)
    above8 = jnp.broadcast_to(jnp.sum(above8, axis=0, keepdims=True), (SUBLANES, TQ))

    def count_ge_gathered(thr8):
        c8 = jnp.sum((gathered >= thr8[None]).astype(jnp.int32), axis=0)
        return above8 + jnp.broadcast_to(jnp.sum(c8, axis=0, keepdims=True), (SUBLANES, TQ))

    def refine(count_fn):
        cnt0 = count_fn(_key_to_float(k_base))
        def body(it, carry):
            off, cnt_lo = carry
            cand = off | jnp.left_shift(jnp.int32(1), WINDOW_BITS - it)
            cnt = count_fn(_key_to_float(k_base + cand))
            ok = cnt >= n_keep
            return jnp.where(ok, cand, off), jnp.where(ok, cnt, cnt_lo)
        off, cnt_lo = lax.fori_loop(0, WINDOW_BITS + 1, body,
                                    (jnp.zeros((SUBLANES, TQ), jnp.int32), cnt0))
        key_ref[0] = k_base + off
        key_ref[1] = cnt_lo

    refine(count_ge_gathered)

    @pl.when(jnp.max(ovf) > -jnp.inf)
    def _():
        refine(count_ge)

    key = key_ref[0]
    cnt_lo = key_ref[1]
    lo8 = _key_to_float(key)
    hi8 = _key_to_float(key + 1)
    short = lo8 == -jnp.inf
    lo8 = jnp.where(short, -FLT_MAX, lo8)
    lo = lo8[0:1]

    has_tie = jnp.max(jnp.where(short, n_keep, cnt_lo)) > n_keep

    @pl.when(jnp.logical_not(has_tie))
    def _():
        def body(c, carry):
            ks = pl.multiple_of(c * KC, KC)
            bias_ref[pl.ds(ks, KC), :] = jnp.where(st_ref[pl.ds(ks, KC), :] >= lo, 0.0, NEG)
            return carry
        lax.fori_loop(0, nkc, body, 0)

    @pl.when(has_tie)
    def _():
        hi = hi8[0:1]
        need = (n_keep - count_ge(hi8)).astype(F32)[0:1]
        tri = (lax.broadcasted_iota(jnp.int32, (KC, KC), 0)
               >= lax.broadcasted_iota(jnp.int32, (KC, KC), 1)).astype(BF16)

        def body(c, taken):
            ks = pl.multiple_of(c * KC, KC)
            s = st_ref[pl.ds(ks, KC), :]
            above = s >= hi
            tied = jnp.where(above, 0.0, jnp.where(s >= lo, 1.0, 0.0))
            rank = taken + jnp.dot(tri, tied.astype(BF16), preferred_element_type=F32)
            pick = jnp.where(above, 1.0, jnp.where(rank <= need, tied, 0.0))
            bias_ref[pl.ds(ks, KC), :] = jnp.where(pick > 0.5, 0.0, NEG)
            return rank[KC - 1:KC, :]

        lax.fori_loop(0, nkc, body, jnp.zeros((1, TQ), F32))

    for acc_ref in acc_refs:
        acc_ref[...] = jnp.zeros(acc_ref.shape, F32)
    ones_rows = (lax.broadcasted_iota(jnp.int32, (N_ACC_ROWS - HEAD_DIM, KC), 0) == 0).astype(BF16)

    m_ref[...] = jnp.full(m_ref.shape, NEG, F32)

    def heads_step(c_qk, qk_ref, c_pv, pv_ref):
        if c_qk is not None:
            ks = pl.multiple_of(c_qk * KC, KC)
            kpc = kp_ref[0, pl.ds(ks, KC), :]
            bias = bias_ref[pl.ds(ks, KC), :]
        if c_pv is not None:
            vtc = jnp.concatenate([vt_ref[0, c_pv], ones_rows], axis=0)
            m_all, mx_cur = m_ref[...], mx_ref[...]
        m_rows, mx_rows = [], []
        for hd in range(N_HEADS):
            if c_qk is not None:
                lg = jnp.dot(kpc, qpt_ref[0, hd], preferred_element_type=F32) + bias
                qk_ref[hd] = lg
                mx_rows.append(jnp.max(lg, axis=0, keepdims=True))
            if c_pv is not None:
                m_old = m_all[hd:hd + 1, :]
                m_new = jnp.maximum(m_old, mx_cur[hd:hd + 1, :])
                alpha = jnp.exp2(m_old - m_new)
                p = jnp.exp2(pv_ref[hd] - m_new).astype(BF16)
                acc_refs[hd][...] = alpha * acc_refs[hd][...] + jnp.dot(
                    vtc, p, preferred_element_type=F32)
                m_rows.append(m_new)
        if c_pv is not None:
            m_ref[...] = jnp.concatenate(m_rows, axis=0)
        if c_qk is not None:
            mx_ref[...] = jnp.concatenate(mx_rows, axis=0)

    n_steps = nkc - 1
    heads_step(0, lga_ref, None, None)

    def pair_body(i, carry):
        heads_step(2 * i + 1, lgb_ref, 2 * i, lga_ref)
        heads_step(2 * i + 2, lga_ref, 2 * i + 1, lgb_ref)
        return carry

    lax.fori_loop(0, n_steps // 2, pair_body, 0)

    @pl.when(n_steps % 2 == 1)
    def _():
        heads_step(n_steps, lgb_ref, n_steps - 1, lga_ref)
        heads_step(None, None, n_steps, lgb_ref)

    @pl.when(n_steps % 2 == 0)
    def _():
        heads_step(None, None, n_steps, lga_ref)

    for hd in range(N_HEADS):
        acc = acc_refs[hd][...]
        ot_ref[hd * HEAD_DIM:(hd + 1) * HEAD_DIM, :] = acc[0:HEAD_DIM] / acc[HEAD_DIM:HEAD_DIM + 1]
    yb_ref[0] = (ot_ref[...].T * saz_ref[0]).astype(BF16)


def _post_kernel(x_ref, ya_ref, yb_ref, ga_ref, gb_ref, p_ref,
                 wa_ref, wb_ref, wo_ref, pg_ref, wpg_ref, wpp_ref, fg_ref, out_ref, *, tm, final_norm):
    for r0 in range(0, tm, tm // 2):
        rows = slice(r0, r0 + tm // 2)
        merged = (ga_ref[0, rows] * jnp.dot(ya_ref[0, rows], wa_ref[...], preferred_element_type=F32)
                  + gb_ref[0, rows] * jnp.dot(yb_ref[0, rows], wb_ref[...],
                                              preferred_element_type=F32))
        x1 = x_ref[0, rows] + jnp.dot(merged.astype(BF16), wo_ref[...], preferred_element_type=F32)
        gate = _sigmoid(jnp.dot(_rms(x1, pg_ref[...]).astype(BF16), wpg_ref[...],
                                preferred_element_type=F32))
        x2 = x1 + gate * jnp.dot(p_ref[0, rows].astype(BF16), wpp_ref[...],
                                 preferred_element_type=F32)
        out_ref[0, rows] = _rms(x2, fg_ref[...]) if final_norm else x2


def _pack_w_in(w):
    o_in, o_z, o_q, o_k, o_v, o_az, o_iq, o_ik, o_iw, o_ga, o_gb = (
        0, 1024, 1536, 2048, 2112, 2176, 2688, 3200, 3264, 3272, 4296)
    pad = jnp.zeros((D_MODEL, LANES - IDX_DIM - N_IDX_HEADS), w.dtype)
    cols = [w[:, o_in:o_z], w[:, o_z:o_q], w[:, o_az:o_iq],
            w[:, o_ga:o_gb], w[:, o_gb:o_gb + D_MODEL], w[:, o_k:o_az],
            w[:, o_ik:o_ga], pad]
    w_tok = jnp.concatenate(cols, axis=1).astype(BF16)
    rows = [w[:, o_q:o_k], w[:, o_iq:o_ik], w[:, o_v:o_az]]
    w_feat = jnp.concatenate(rows, axis=1).T.astype(BF16)
    return w_tok, w_feat


def _layer(x, p, norm_g, w_in, conv_w, conv_b, ln_g, ln_b, w_a, w_b, w_o,
           ple_g, w_pg, w_pp, out_g, final_norm):
    B, L, D = x.shape
    n_keep = min(TOPK_MAX, L // 4)
    cp = functools.partial(pltpu.CompilerParams, vmem_limit_bytes=VMEM_LIMIT)
    row = lambda a: a.reshape(1, -1)

    tm1 = TM_PROJ
    w_tok, w_feat = _pack_w_in(w_in)
    tok = lambda w: pl.BlockSpec((1, tm1, w), lambda b, i: (b, i, 0))
    featm = lambda r: pl.BlockSpec((1, N_HEADS, r, tm1), lambda b, i: (b, 0, 0, i))
    const = lambda a: pl.BlockSpec(a.shape, lambda b, i: (0,) * a.ndim, pipeline_mode=pl.Buffered(1))
    f32s = lambda w: jax.ShapeDtypeStruct((B, L, w), F32)
    consts1 = [row(norm_g), w_tok, w_feat, conv_w, row(conv_b), row(ln_g), row(ln_b)]
    ya, saz, ga, gb, qpt, kp, vt, iqt, ik, iwt = pl.pallas_call(
        functools.partial(_proj_kernel, tm=tm1),
        grid=(B, L // tm1),
        in_specs=[tok(D)] + [const(a) for a in consts1],
        out_specs=[tok(D_CONV), tok(D_ATTN), tok(D), tok(D),
                   featm(LANES), tok(LANES),
                   pl.BlockSpec((1, tm1 // KC, HEAD_DIM, KC), lambda b, i: (b, i, 0, 0)),
                   featm(IDX_DIM), tok(IDX_DIM),
                   pl.BlockSpec((1, N_IDX_HEADS, tm1), lambda b, i: (b, 0, i))],
        out_shape=[jax.ShapeDtypeStruct((B, L, D_CONV), BF16), f32s(D_ATTN), f32s(D), f32s(D),
                   jax.ShapeDtypeStruct((B, N_HEADS, LANES, L), BF16),
                   jax.ShapeDtypeStruct((B, L, LANES), BF16),
                   jax.ShapeDtypeStruct((B, L // KC, HEAD_DIM, KC), BF16),
                   jax.ShapeDtypeStruct((B, N_IDX_HEADS, IDX_DIM, L), BF16),
                   jax.ShapeDtypeStruct((B, L, IDX_DIM), BF16),
                   jax.ShapeDtypeStruct((B, N_IDX_HEADS, L), F32)],
        scratch_shapes=[pltpu.VMEM((HALO + tm1, D_CONV), F32),
                        pltpu.VMEM((SUBLANES - 1, HALO + tm1 - SUBLANES, D_CONV), F32),
                        pltpu.VMEM((HALO, D_CONV), F32)],
        compiler_params=cp(dimension_semantics=("arbitrary", "arbitrary")),
        name="proj",
    )(x, *consts1)

    yb = pl.pallas_call(
        functools.partial(_attn_kernel, n_keep=n_keep),
        grid=(B, L // TQ),
        in_specs=[pl.BlockSpec((1, N_IDX_HEADS, IDX_DIM, TQ), lambda b, j: (b, 0, 0, j)),
                  pl.BlockSpec((1, L, IDX_DIM), lambda b, j: (b, 0, 0)),
                  pl.BlockSpec((1, N_IDX_HEADS, TQ), lambda b, j: (b, 0, j)),
                  pl.BlockSpec((1, N_HEADS, LANES, TQ), lambda b, j: (b, 0, 0, j)),
                  pl.BlockSpec((1, L, LANES), lambda b, j: (b, 0, 0)),
                  pl.BlockSpec((1, L // KC, HEAD_DIM, KC), lambda b, j: (b, 0, 0, 0)),
                  pl.BlockSpec((1, TQ, D_ATTN), lambda b, j: (b, j, 0))],
        out_specs=pl.BlockSpec((1, TQ, D_ATTN), lambda b, j: (b, j, 0)),
        out_shape=jax.ShapeDtypeStruct((B, L, D_ATTN), BF16),
        scratch_shapes=[pltpu.VMEM((L, TQ), F32),
                        pltpu.VMEM((L, TQ), BF16),
                        pltpu.VMEM((2, SUBLANES, TQ), jnp.int32),
                        pltpu.VMEM((L, TQ), F32),
                        pltpu.VMEM((D_ATTN, TQ), F32),
                        pltpu.VMEM((N_HEADS, KC, TQ), F32),
                        pltpu.VMEM((N_HEADS, KC, TQ), F32),
                        pltpu.VMEM((N_HEADS, TQ), F32),
                        pltpu.VMEM((N_HEADS, TQ), F32)]
                       + [pltpu.VMEM((N_ACC_ROWS, TQ), F32)] * N_HEADS,
        compiler_params=cp(dimension_semantics=("arbitrary", "arbitrary")),
        name="attn",
    )(iqt, ik, iwt, qpt, kp, vt, saz)

    tm3 = TM_POST
    tok = lambda w: pl.BlockSpec((1, tm3, w), lambda b, i: (b, i, 0))
    consts3 = [w_a.astype(BF16), w_b.astype(BF16), w_o.astype(BF16), row(ple_g),
               w_pg.astype(BF16), w_pp.astype(BF16), row(out_g)]
    return pl.pallas_call(
        functools.partial(_post_kernel, tm=tm3, final_norm=final_norm),
        grid=(B, L // tm3),
        in_specs=[tok(D), tok(D_CONV), tok(D_ATTN), tok(D), tok(D), tok(D_PLE)]
                 + [const(a) for a in consts3],
        out_specs=tok(D),
        out_shape=jax.ShapeDtypeStruct((B, L, D), F32),
        compiler_params=cp(dimension_semantics=("arbitrary", "arbitrary")),
        name="post",
    )(x, ya, yb, ga, gb, p, *consts3)


def kernel(x, p, norm_g, w_in, conv_w, conv_b, conv_ln_g, conv_ln_b, w_a_out, w_b_out,
           w_o, ple_norm_g, w_ple_gate, w_ple_proj, final_g):
    depth = w_in.shape[0]
    for i in range(depth):
        x = _layer(x, p[i], norm_g[i], w_in[i], conv_w[i], conv_b[i], conv_ln_g[i],
                   conv_ln_b[i], w_a_out[i], w_b_out[i], w_o[i], ple_norm_g[i],
                   w_ple_gate[i], w_ple_proj[i], final_g, i == depth - 1)
    return x
```

```python
import functools
import math

import jax
import jax.numpy as jnp
import numpy as np
from jax import lax
from jax.experimental import pallas as pl
from jax.experimental.pallas import tpu as pltpu

D_MODEL = 1024
D_PLE = 256
D_CONV = 512
CONV_WIDTH = 31
N_HEADS = 8
HEAD_DIM = 64
D_ATTN = N_HEADS * HEAD_DIM
N_IDX_HEADS = 8
IDX_DIM = 64
TOPK_MAX = 256
EPS = 1e-6

LANES = 128
SUBLANES = 8
TQ = 256
KC = 256
TM_PROJ = 512
TM_POST = 512
HALO = 32
NEG = -1e30
FLT_MAX = float(np.finfo(np.float32).max)
BF16_ROWS = 16
WINDOW_BITS = 16
WINDOW = 2 ** WINDOW_BITS
LOG2E = math.log2(math.e)
N_ACC_ROWS = HEAD_DIM + 16
VMEM_LIMIT = 56 * 1024 * 1024

F32 = jnp.float32
BF16 = jnp.bfloat16

_C_VAL, _C_GATE, _C_Z, _C_AZ, _C_GA, _C_GB, _C_KV, _C_IKW, _C_END = (
    0, 512, 1024, 1536, 2048, 3072, 4096, 4224, 4352)
_R_Q, _R_IQ, _R_V, _R_END = 0, 512, 1024, 1088

_NT = (((1,), (1,)), ((), ()))


def _bf16_split3(a):
    a = np.float32(a)
    p1 = np.float32(a.astype(BF16))
    p2 = np.float32(np.float32(a - p1).astype(BF16))
    p3 = np.float32(a - p1 - p2)
    assert np.float32(np.float32(p3).astype(BF16)) == p3
    return float(p1), float(p2), float(p3)


def _sigmoid(x):
    return 1.0 / (1.0 + jnp.exp(-x))


def _silu(x):
    return x * _sigmoid(x)


def _rms(x, g):
    return x * lax.rsqrt(jnp.mean(x * x, axis=-1, keepdims=True) + EPS) * g


def _proj_kernel(x_ref, g_ref, w_ref, wt_ref, cw_ref, cb_ref, lng_ref, lnb_ref,
                 ya_ref, saz_ref, ga_ref, gb_ref,
                 qpt_ref, kp_ref, vt_ref, iqt_ref, ik_ref, iwt_ref,
                 ext_ref, es_ref, carry_ref, *, tm):
    i = pl.program_id(1)
    h = _rms(x_ref[0], g_ref[...]).astype(BF16)

    def mm(a, b):
        return jnp.dot(h, w_ref[:, a:b], preferred_element_type=F32)

    def mm_t(a, b):
        return lax.dot_general(wt_ref[a:b, :], h, _NT, preferred_element_type=F32)

    u = mm(_C_VAL, _C_GATE) * _sigmoid(mm(_C_GATE, _C_Z))
    prev = carry_ref[...]
    ext_ref[0:HALO, :] = jnp.where(i > 0, prev, jnp.zeros_like(prev))
    ext_ref[HALO:, :] = u
    carry_ref[...] = u[tm - HALO:, :]
    n_sh = HALO + tm - SUBLANES
    for r in range(1, SUBLANES):
        es_ref[r - 1] = ext_ref[r:r + n_sh, :]

    def w_conformer():
        base = HALO - (CONV_WIDTH - 1)
        c = jnp.zeros((tm, D_CONV), F32)
        for jj in range(CONV_WIDTH):
            row8, r = divmod(base + jj, SUBLANES)
            row = row8 * SUBLANES
            tap = ext_ref[row:row + tm, :] if r == 0 else es_ref[r - 1, row:row + tm, :]
            c = c + cw_ref[jj:jj + 1, :] * tap
        c = c + cb_ref[...]
        mu = jnp.mean(c, axis=-1, keepdims=True)
        cc = c - mu
        var = jnp.mean(cc * cc, axis=-1, keepdims=True)
        ln = cc * lax.rsqrt(var + EPS) * lng_ref[...] + lnb_ref[...]
        ya_ref[0] = (_silu(ln) * _silu(mm(_C_Z, _C_AZ))).astype(BF16)

    def w_saz():
        saz_ref[0] = _silu(mm(_C_AZ, _C_GA))

    def w_gate(ref, c0, half):
        def run():
            lo = half * (D_MODEL // 2)
            ref[0, :, lo:lo + D_MODEL // 2] = _sigmoid(mm(c0 + lo, c0 + lo + D_MODEL // 2))
        return run

    def w_keys():
        lane = lax.broadcasted_iota(jnp.int32, (tm, LANES), 1)
        pos = i * tm + lax.broadcasted_iota(jnp.int32, (tm, LANES), 0)
        kext = jnp.where(lane < 67, 1.0,
               jnp.where(lane < 70, (pos >> 6).astype(F32),
               jnp.where(lane < 73, (pos & 63).astype(F32), 0.0)))
        kp_ref[0] = jnp.where(lane < 64, mm(_C_KV, _C_IKW), kext).astype(BF16)
        ikw = mm(_C_IKW, _C_END)
        ik_ref[0] = ikw[:, :IDX_DIM].astype(BF16)
        iwt_ref[0] = ikw.T[IDX_DIM:IDX_DIM + N_IDX_HEADS, :] * (N_IDX_HEADS ** -0.5)

    def w_queries():
        sub = lax.broadcasted_iota(jnp.int32, (16, tm), 0)
        tpos = (i * tm + lax.broadcasted_iota(jnp.int32, (16, tm), 1)).astype(F32)
        qt = mm_t(_R_Q, _R_IQ) * (LOG2E * HEAD_DIM ** -0.5)
        for hd in range(N_HEADS):
            a = np.float32(2.0 ** (-8.0 * (hd + 1) / N_HEADS) * LOG2E)
            a1, a2, a3 = _bf16_split3(a)
            v = tpos * float(-a)
            v1 = v.astype(BF16).astype(F32)
            v2 = (v - v1).astype(BF16).astype(F32)
            v3 = v - v1 - v2
            ext = jnp.where(sub == 0, v1, jnp.where(sub == 1, v2, jnp.where(sub == 2, v3,
                  jnp.where(sub == 3, 64.0 * a1, jnp.where(sub == 4, 64.0 * a2,
                  jnp.where(sub == 5, 64.0 * a3, jnp.where(sub == 6, a1,
                  jnp.where(sub == 7, a2, jnp.where(sub == 8, a3, 0.0)))))))))
            qpt_ref[0, hd, 0:HEAD_DIM, :] = qt[hd * HEAD_DIM:(hd + 1) * HEAD_DIM, :].astype(BF16)
            qpt_ref[0, hd, HEAD_DIM:HEAD_DIM + 16, :] = ext.astype(BF16)
            qpt_ref[0, hd, HEAD_DIM + 16:, :] = jnp.zeros((LANES - HEAD_DIM - 16, tm), BF16)

    def w_index_values():
        iqt = mm_t(_R_IQ, _R_V) * (IDX_DIM ** -0.5)
        for hd in range(N_IDX_HEADS):
            iqt_ref[0, hd] = iqt[hd * IDX_DIM:(hd + 1) * IDX_DIM, :].astype(BF16)
        vt = mm_t(_R_V, _R_END).astype(BF16)
        for c in range(tm // KC):
            vt_ref[0, c] = vt[:, c * KC:(c + 1) * KC]

    for work in (w_conformer, w_saz, w_gate(ga_ref, _C_GA, 0), w_gate(ga_ref, _C_GA, 1),
                 w_gate(gb_ref, _C_GB, 0), w_gate(gb_ref, _C_GB, 1), w_keys, w_queries,
                 w_index_values):
        work()


def _key_to_float(k):
    bits = jnp.where(k >= 0, k, k ^ jnp.int32(0x7FFFFFFF))
    f = lax.bitcast_convert_type(bits, F32)
    return jnp.where(k < 0, jnp.where(f != f, -jnp.inf, f), f)


def _key16_to_float(k):
    bits = jnp.left_shift(jnp.where(k >= 0, k, k ^ jnp.int32(0x7FFF)), 16)
    f = lax.bitcast_convert_type(bits, F32)
    return jnp.where(k < 0, jnp.where(f != f, -jnp.inf, f), f)


def _attn_kernel(iqt_ref, ik_ref, iwt_ref, qpt_ref, kp_ref, vt_ref, saz_ref, yb_ref,
                 st_ref, rt_ref, key_ref, bias_ref, ot_ref, lga_ref, lgb_ref, m_ref, mx_ref,
                 *acc_refs, n_keep):
    j = pl.program_id(1)
    t0 = j * TQ
    nkc = j + 1

    tcol = t0 + lax.broadcasted_iota(jnp.int32, (KC, TQ), 1)
    srow0 = lax.broadcasted_iota(jnp.int32, (KC, TQ), 0)

    def score_chunk(c):
        ks = pl.multiple_of(c * KC, KC)
        kic = ik_ref[0, pl.ds(ks, KC), :]
        acc = jnp.zeros((KC, TQ), F32)
        for hd in range(N_IDX_HEADS):
            d = jnp.dot(kic, iqt_ref[0, hd], preferred_element_type=F32)
            acc = acc + iwt_ref[0, hd:hd + 1, :] * jnp.maximum(d, 0.0)
        masked = jnp.where((ks + srow0) <= tcol, acc, -jnp.inf)
        st_ref[pl.ds(ks, KC), :] = masked
        rt_ref[pl.ds(ks, KC), :] = masked.astype(BF16)

    def score_pair(i, carry):
        score_chunk(2 * i)
        score_chunk(2 * i + 1)
        return carry

    lax.fori_loop(0, nkc // 2, score_pair, 0)

    @pl.when(nkc % 2 == 1)
    def _():
        score_chunk(nkc - 1)

    n_part = 4

    def count_ge(thr8):
        def body(c, cnt):
            ks = pl.multiple_of(c * KC, KC)
            s = st_ref[pl.ds(ks, KC), :].reshape(KC // (SUBLANES * n_part), n_part, SUBLANES, TQ)
            return cnt + jnp.sum((s >= thr8[None, None]).astype(jnp.int32), axis=0)
        cnt = lax.fori_loop(0, nkc, body, jnp.zeros((n_part, SUBLANES, TQ), jnp.int32))
        cnt8 = jnp.sum(cnt, axis=0)
        return jnp.broadcast_to(jnp.sum(cnt8, axis=0, keepdims=True), (SUBLANES, TQ))

    def count_ge16(thr_bf):
        def body(c, cnt):
            ks = pl.multiple_of(c * KC, KC)
            r = rt_ref[pl.ds(ks, KC), :].reshape(KC // (BF16_ROWS * 2), 2, BF16_ROWS, TQ)
            w = jnp.where(r >= thr_bf[None, None], jnp.ones((), BF16), jnp.zeros((), BF16))
            part = w[0]
            for g in range(1, w.shape[0]):
                part = part + w[g]
            return cnt + part
        cnt = lax.fori_loop(0, nkc, body, jnp.zeros((2, BF16_ROWS, TQ), BF16)).astype(F32)
        tot = jnp.sum(cnt[0] + cnt[1], axis=0, keepdims=True)
        return jnp.broadcast_to(tot, (SUBLANES, TQ))

    def thr16(k16):
        f = _key16_to_float(k16)
        return jnp.concatenate([f] * (BF16_ROWS // SUBLANES), axis=0).astype(BF16)

    k16 = jnp.where(count_ge16(thr16(jnp.zeros((SUBLANES, TQ), jnp.int32))) >= n_keep,
                    jnp.int32(0), jnp.int32(-(2 ** 15)))

    def bit16_body(it, k):
        cand = k | jnp.left_shift(jnp.int32(1), 14 - it)
        return jnp.where(count_ge16(thr16(cand)) >= n_keep, cand, k)

    k16 = lax.fori_loop(0, 15, bit16_body, k16)
    hb_bits = lax.bitcast_convert_type(_key16_to_float(k16), jnp.int32)
    k_hb = hb_bits ^ ((hb_bits >> 31) & jnp.int32(0x7FFFFFFF))
    k_base = k_hb - WINDOW // 2
    k_base_wide = k_hb - WINDOW
    w_lo = jnp.maximum(_key_to_float(k_base), -FLT_MAX)
    w_hi = _key_to_float(k_base + WINDOW + 1)

    def gather_body(c, carry):
        m1, m2, m3, ovf, above = carry
        ks = pl.multiple_of(c * KC, KC)
        s = st_ref[pl.ds(ks, KC), :].reshape(KC // (SUBLANES * n_part), n_part, SUBLANES, TQ)
        for g in range(s.shape[0]):
            sg = s[g]
            ge_hi = sg >= w_hi[None]
            above = above + ge_hi.astype(jnp.int32)
            v = jnp.where(ge_hi, -jnp.inf, jnp.where(sg >= w_lo[None], sg, -jnp.inf))
            t1 = jnp.minimum(m1, v)
            m1 = jnp.maximum(m1, v)
            t2 = jnp.minimum(m2, t1)
            m2 = jnp.maximum(m2, t1)
            ovf = jnp.maximum(ovf, jnp.minimum(m3, t2))
            m3 = jnp.maximum(m3, t2)
        return m1, m2, m3, ovf, above

    ninf = jnp.full((n_part, SUBLANES, TQ), -jnp.inf, F32)
    m1, m2, m3, ovf, above = lax.fori_loop(
        0, nkc, gather_body, (ninf, ninf, ninf, ninf, jnp.zeros((n_part, SUBLANES, TQ), jnp.int32)))
    gathered = jnp.concatenate([m1, m2, m3], axis=0)
    above8 = jnp.sum(above, axis=0)
    above8 = jnp.broadcast_to(jnp.sum(above8, axis=0, keepdims=True), (SUBLANES, TQ))

    def count_ge_gathered(thr8):
        c8 = jnp.sum((gathered >= thr8[None]).astype(jnp.int32), axis=0)
        return above8 + jnp.broadcast_to(jnp.sum(c8, axis=0, keepdims=True), (SUBLANES, TQ))

    def refine(count_fn, base):
        cnt0 = count_fn(_key_to_float(base))
        def body(it, carry):
            off, cnt_lo = carry
            cand = off | jnp.left_shift(jnp.int32(1), WINDOW_BITS - it)
            cnt = count_fn(_key_to_float(base + cand))
            ok = cnt >= n_keep
            return jnp.where(ok, cand, off), jnp.where(ok, cnt, cnt_lo)
        off, cnt_lo = lax.fori_loop(0, WINDOW_BITS + 1, body,
                                    (jnp.zeros((SUBLANES, TQ), jnp.int32), cnt0))
        key_ref[0] = base + off
        key_ref[1] = cnt_lo

    refine(count_ge_gathered, k_base)

    in_window = count_ge_gathered(w_lo)
    outside = jnp.where(in_window < n_keep, 1, jnp.where(above8 >= n_keep, 1, 0))
    outside = jnp.where(w_lo > -FLT_MAX, outside, 0)
    gathered_ok = (jnp.max(ovf) == -jnp.inf) & (jnp.max(outside) == 0)

    @pl.when(jnp.logical_not(gathered_ok))
    def _():
        refine(count_ge, k_base_wide)

    key = key_ref[0]
    cnt_lo = key_ref[1]
    lo8 = _key_to_float(key)
    hi8 = _key_to_float(key + 1)
    short = lo8 == -jnp.inf
    lo8 = jnp.where(short, -FLT_MAX, lo8)
    lo = lo8[0:1]

    has_tie = jnp.max(jnp.where(short, n_keep, cnt_lo)) > n_keep

    @pl.when(jnp.logical_not(has_tie))
    def _():
        def body(c, carry):
            ks = pl.multiple_of(c * KC, KC)
            bias_ref[pl.ds(ks, KC), :] = jnp.where(st_ref[pl.ds(ks, KC), :] >= lo, 0.0, NEG)
            return carry
        lax.fori_loop(0, nkc, body, 0)

    @pl.when(has_tie)
    def _():
        hi = hi8[0:1]
        key_ref[1] = count_ge_gathered(hi8)

        @pl.when(jnp.logical_not(gathered_ok))
        def _():
            key_ref[1] = count_ge(hi8)

        need = (n_keep - key_ref[1]).astype(F32)[0:1]
        tri = (lax.broadcasted_iota(jnp.int32, (KC, KC), 0)
               >= lax.broadcasted_iota(jnp.int32, (KC, KC), 1)).astype(BF16)

        def body(c, taken):
            ks = pl.multiple_of(c * KC, KC)
            s = st_ref[pl.ds(ks, KC), :]
            ge_lo = s >= lo
            ge_hi = s >= hi
            tied = jnp.where(ge_lo, jnp.where(ge_hi, 0.0, 1.0), 0.0).astype(BF16)
            rank = taken + jnp.dot(tri, tied, preferred_element_type=F32)
            bias_ref[pl.ds(ks, KC), :] = jnp.where(
                ge_lo, jnp.where(ge_hi, 0.0, jnp.where(rank <= need, 0.0, NEG)), NEG)
            return rank[KC - 1:KC, :]

        lax.fori_loop(0, nkc, body, jnp.zeros((1, TQ), F32))

    for acc_ref in acc_refs:
        acc_ref[...] = jnp.zeros(acc_ref.shape, F32)
    ones_rows = (lax.broadcasted_iota(jnp.int32, (N_ACC_ROWS - HEAD_DIM, KC), 0) == 0).astype(BF16)

    m_ref[...] = jnp.full(m_ref.shape, NEG, F32)

    def heads_step(c_qk, qk_ref, c_pv, pv_ref):
        if c_qk is not None:
            ks = pl.multiple_of(c_qk * KC, KC)
            kpc = kp_ref[0, pl.ds(ks, KC), :]
            bias = bias_ref[pl.ds(ks, KC), :]
        if c_pv is not None:
            vtc = jnp.concatenate([vt_ref[0, c_pv], ones_rows], axis=0)
            m_all, mx_cur = m_ref[...], mx_ref[...]
        m_rows, mx_rows = [], []
        for hd in range(N_HEADS):
            if c_qk is not None:
                lg = jnp.dot(kpc, qpt_ref[0, hd], preferred_element_type=F32) + bias
                qk_ref[hd] = lg
                mx_rows.append(jnp.max(lg, axis=0, keepdims=True))
            if c_pv is not None:
                m_old = m_all[hd:hd + 1, :]
                m_new = jnp.maximum(m_old, mx_cur[hd:hd + 1, :])
                alpha = jnp.exp2(m_old - m_new)
                p = jnp.exp2(pv_ref[hd] - m_new).astype(BF16)
                acc_refs[hd][...] = alpha * acc_refs[hd][...] + jnp.dot(
                    vtc, p, preferred_element_type=F32)
                m_rows.append(m_new)
        if c_pv is not None:
            m_ref[...] = jnp.concatenate(m_rows, axis=0)
        if c_qk is not None:
            mx_ref[...] = jnp.concatenate(mx_rows, axis=0)

    n_steps = nkc - 1
    heads_step(0, lga_ref, None, None)

    def pair_body(i, carry):
        heads_step(2 * i + 1, lgb_ref, 2 * i, lga_ref)
        heads_step(2 * i + 2, lga_ref, 2 * i + 1, lgb_ref)
        return carry

    lax.fori_loop(0, n_steps // 2, pair_body, 0)

    @pl.when(n_steps % 2 == 1)
    def _():
        heads_step(n_steps, lgb_ref, n_steps - 1, lga_ref)
        heads_step(None, None, n_steps, lgb_ref)

    @pl.when(n_steps % 2 == 0)
    def _():
        heads_step(None, None, n_steps, lga_ref)

    for hd in range(N_HEADS):
        acc = acc_refs[hd][...]
        ot_ref[hd * HEAD_DIM:(hd + 1) * HEAD_DIM, :] = acc[0:HEAD_DIM] / acc[HEAD_DIM:HEAD_DIM + 1]
    yb_ref[0] = (ot_ref[...].T * saz_ref[0]).astype(BF16)


def _post_kernel(x_ref, ya_ref, yb_ref, ga_ref, gb_ref, p_ref,
                 wa_ref, wb_ref, wo_ref, pg_ref, wpg_ref, wpp_ref, fg_ref, out_ref, *, tm, final_norm):
    for r0 in range(0, tm, tm // 2):
        rows = slice(r0, r0 + tm // 2)
        merged = (ga_ref[0, rows] * jnp.dot(ya_ref[0, rows], wa_ref[...], preferred_element_type=F32)
                  + gb_ref[0, rows] * jnp.dot(yb_ref[0, rows], wb_ref[...],
                                              preferred_element_type=F32))
        x1 = x_ref[0, rows] + jnp.dot(merged.astype(BF16), wo_ref[...], preferred_element_type=F32)
        gate = _sigmoid(jnp.dot(_rms(x1, pg_ref[...]).astype(BF16), wpg_ref[...],
                                preferred_element_type=F32))
        x2 = x1 + gate * jnp.dot(p_ref[0, rows].astype(BF16), wpp_ref[...],
                                 preferred_element_type=F32)
        out_ref[0, rows] = _rms(x2, fg_ref[...]) if final_norm else x2


def _pack_w_in(w):
    o_in, o_z, o_q, o_k, o_v, o_az, o_iq, o_ik, o_iw, o_ga, o_gb = (
        0, 1024, 1536, 2048, 2112, 2176, 2688, 3200, 3264, 3272, 4296)
    pad = jnp.zeros((D_MODEL, LANES - IDX_DIM - N_IDX_HEADS), w.dtype)
    cols = [w[:, o_in:o_z], w[:, o_z:o_q], w[:, o_az:o_iq],
            w[:, o_ga:o_gb], w[:, o_gb:o_gb + D_MODEL], w[:, o_k:o_az],
            w[:, o_ik:o_ga], pad]
    w_tok = jnp.concatenate(cols, axis=1).astype(BF16)
    rows = [w[:, o_q:o_k], w[:, o_iq:o_ik], w[:, o_v:o_az]]
    w_feat = jnp.concatenate(rows, axis=1).T.astype(BF16)
    return w_tok, w_feat


def _layer(x, p, norm_g, w_in, conv_w, conv_b, ln_g, ln_b, w_a, w_b, w_o,
           ple_g, w_pg, w_pp, out_g, final_norm):
    B, L, D = x.shape
    n_keep = min(TOPK_MAX, L // 4)
    cp = functools.partial(pltpu.CompilerParams, vmem_limit_bytes=VMEM_LIMIT)
    row = lambda a: a.reshape(1, -1)

    tm1 = TM_PROJ
    w_tok, w_feat = _pack_w_in(w_in)
    tok = lambda w: pl.BlockSpec((1, tm1, w), lambda b, i: (b, i, 0))
    featm = lambda r: pl.BlockSpec((1, N_HEADS, r, tm1), lambda b, i: (b, 0, 0, i))
    const = lambda a: pl.BlockSpec(a.shape, lambda b, i: (0,) * a.ndim, pipeline_mode=pl.Buffered(1))
    f32s = lambda w: jax.ShapeDtypeStruct((B, L, w), F32)
    consts1 = [row(norm_g), w_tok, w_feat, conv_w, row(conv_b), row(ln_g), row(ln_b)]
    ya, saz, ga, gb, qpt, kp, vt, iqt, ik, iwt = pl.pallas_call(
        functools.partial(_proj_kernel, tm=tm1),
        grid=(B, L // tm1),
        in_specs=[tok(D)] + [const(a) for a in consts1],
        out_specs=[tok(D_CONV), tok(D_ATTN), tok(D), tok(D),
                   featm(LANES), tok(LANES),
                   pl.BlockSpec((1, tm1 // KC, HEAD_DIM, KC), lambda b, i: (b, i, 0, 0)),
                   featm(IDX_DIM), tok(IDX_DIM),
                   pl.BlockSpec((1, N_IDX_HEADS, tm1), lambda b, i: (b, 0, i))],
        out_shape=[jax.ShapeDtypeStruct((B, L, D_CONV), BF16), f32s(D_ATTN), f32s(D), f32s(D),
                   jax.ShapeDtypeStruct((B, N_HEADS, LANES, L), BF16),
                   jax.ShapeDtypeStruct((B, L, LANES), BF16),
                   jax.ShapeDtypeStruct((B, L // KC, HEAD_DIM, KC), BF16),
                   jax.ShapeDtypeStruct((B, N_IDX_HEADS, IDX_DIM, L), BF16),
                   jax.ShapeDtypeStruct((B, L, IDX_DIM), BF16),
                   jax.ShapeDtypeStruct((B, N_IDX_HEADS, L), F32)],
        scratch_shapes=[pltpu.VMEM((HALO + tm1, D_CONV), F32),
                        pltpu.VMEM((SUBLANES - 1, HALO + tm1 - SUBLANES, D_CONV), F32),
                        pltpu.VMEM((HALO, D_CONV), F32)],
        compiler_params=cp(dimension_semantics=("arbitrary", "arbitrary")),
        name="proj",
    )(x, *consts1)

    yb = pl.pallas_call(
        functools.partial(_attn_kernel, n_keep=n_keep),
        grid=(B, L // TQ),
        in_specs=[pl.BlockSpec((1, N_IDX_HEADS, IDX_DIM, TQ), lambda b, j: (b, 0, 0, j)),
                  pl.BlockSpec((1, L, IDX_DIM), lambda b, j: (b, 0, 0)),
                  pl.BlockSpec((1, N_IDX_HEADS, TQ), lambda b, j: (b, 0, j)),
                  pl.BlockSpec((1, N_HEADS, LANES, TQ), lambda b, j: (b, 0, 0, j)),
                  pl.BlockSpec((1, L, LANES), lambda b, j: (b, 0, 0)),
                  pl.BlockSpec((1, L // KC, HEAD_DIM, KC), lambda b, j: (b, 0, 0, 0)),
                  pl.BlockSpec((1, TQ, D_ATTN), lambda b, j: (b, j, 0))],
        out_specs=pl.BlockSpec((1, TQ, D_ATTN), lambda b, j: (b, j, 0)),
        out_shape=jax.ShapeDtypeStruct((B, L, D_ATTN), BF16),
        scratch_shapes=[pltpu.VMEM((L, TQ), F32),
                        pltpu.VMEM((L, TQ), BF16),
                        pltpu.VMEM((2, SUBLANES, TQ), jnp.int32),
                        pltpu.VMEM((L, TQ), F32),
                        pltpu.VMEM((D_ATTN, TQ), F32),
                        pltpu.VMEM((N_HEADS, KC, TQ), F32),
                        pltpu.VMEM((N_HEADS, KC, TQ), F32),
                        pltpu.VMEM((N_HEADS, TQ), F32),
                        pltpu.VMEM((N_HEADS, TQ), F32)]
                       + [pltpu.VMEM((N_ACC_ROWS, TQ), F32)] * N_HEADS,
        compiler_params=cp(dimension_semantics=("arbitrary", "arbitrary")),
        name="attn",
    )(iqt, ik, iwt, qpt, kp, vt, saz)

    tm3 = TM_POST
    tok = lambda w: pl.BlockSpec((1, tm3, w), lambda b, i: (b, i, 0))
    consts3 = [w_a.astype(BF16), w_b.astype(BF16), w_o.astype(BF16), row(ple_g),
               w_pg.astype(BF16), w_pp.astype(BF16), row(out_g)]
    return pl.pallas_call(
        functools.partial(_post_kernel, tm=tm3, final_norm=final_norm),
        grid=(B, L // tm3),
        in_specs=[tok(D), tok(D_CONV), tok(D_ATTN), tok(D), tok(D), tok(D_PLE)]
                 + [const(a) for a in consts3],
        out_specs=tok(D),
        out_shape=jax.ShapeDtypeStruct((B, L, D), F32),
        compiler_params=cp(dimension_semantics=("arbitrary", "arbitrary")),
        name="post",
    )(x, ya, yb, ga, gb, p, *consts3)


def kernel(x, p, norm_g, w_in, conv_w, conv_b, conv_ln_g, conv_ln_b, w_a_out, w_b_out,
           w_o, ple_norm_g, w_ple_gate, w_ple_proj, final_g):
    depth = w_in.shape[0]
    for i in range(depth):
        x = _layer(x, p[i], norm_g[i], w_in[i], conv_w[i], conv_b[i], conv_ln_g[i],
                   conv_ln_b[i], w_a_out[i], w_b_out[i], w_o[i], ple_norm_g[i],
                   w_ple_gate[i], w_ple_proj[i], final_g, i == depth - 1)
    return x
```

```python
import functools
import math

import jax
import jax.numpy as jnp
import numpy as np
from jax import lax
from jax.experimental import pallas as pl
from jax.experimental.pallas import tpu as pltpu

D_MODEL = 1024
D_PLE = 256
D_CONV = 512
CONV_WIDTH = 31
N_HEADS = 8
HEAD_DIM = 64
D_ATTN = N_HEADS * HEAD_DIM
N_IDX_HEADS = 8
IDX_DIM = 64
TOPK_MAX = 256
EPS = 1e-6

LANES = 128
SUBLANES = 8
TQ = 256
KC = 256
TM_PROJ = 512
TM_POST = 512
HALO = 32
NEG = -1e30
FLT_MAX = float(np.finfo(np.float32).max)
FLT_TINY = float(np.finfo(np.float32).tiny)
BF16_ROWS = 16
WINDOW_BITS = 16
WINDOW = 2 ** WINDOW_BITS
LOG2E = math.log2(math.e)
N_ACC_ROWS = HEAD_DIM + 16
VMEM_LIMIT = 56 * 1024 * 1024

F32 = jnp.float32
BF16 = jnp.bfloat16

_C_VAL, _C_GATE, _C_Z, _C_AZ, _C_GA, _C_GB, _C_KV, _C_IKW, _C_END = (
    0, 512, 1024, 1536, 2048, 3072, 4096, 4224, 4352)
_R_Q, _R_IQ, _R_V, _R_END = 0, 512, 1024, 1088

_NT = (((1,), (1,)), ((), ()))


def _bf16_split3(a):
    a = np.float32(a)
    p1 = np.float32(a.astype(BF16))
    p2 = np.float32(np.float32(a - p1).astype(BF16))
    p3 = np.float32(a - p1 - p2)
    assert np.float32(np.float32(p3).astype(BF16)) == p3
    return float(p1), float(p2), float(p3)


def _sigmoid(x):
    return 1.0 / (1.0 + jnp.exp(-x))


def _silu(x):
    return x * _sigmoid(x)


def _rms(x, g):
    return x * lax.rsqrt(jnp.mean(x * x, axis=-1, keepdims=True) + EPS) * g


def _proj_kernel(x_ref, g_ref, w_ref, wt_ref, cw_ref, cb_ref, lng_ref, lnb_ref,
                 ya_ref, saz_ref, ga_ref, gb_ref,
                 qpt_ref, kp_ref, vt_ref, iqt_ref, ik_ref, iwt_ref,
                 ext_ref, es_ref, carry_ref, *, tm):
    i = pl.program_id(1)
    h = _rms(x_ref[0], g_ref[...]).astype(BF16)

    def mm(a, b):
        return jnp.dot(h, w_ref[:, a:b], preferred_element_type=F32)

    def mm_t(a, b):
        return lax.dot_general(wt_ref[a:b, :], h, _NT, preferred_element_type=F32)

    u = mm(_C_VAL, _C_GATE) * _sigmoid(mm(_C_GATE, _C_Z))
    prev = carry_ref[...]
    ext_ref[0:HALO, :] = jnp.where(i > 0, prev, jnp.zeros_like(prev))
    ext_ref[HALO:, :] = u
    carry_ref[...] = u[tm - HALO:, :]
    n_sh = HALO + tm - SUBLANES
    for r in range(1, SUBLANES):
        es_ref[r - 1] = ext_ref[r:r + n_sh, :]

    def w_conformer():
        base = HALO - (CONV_WIDTH - 1)
        c = jnp.zeros((tm, D_CONV), F32)
        for jj in range(CONV_WIDTH):
            row8, r = divmod(base + jj, SUBLANES)
            row = row8 * SUBLANES
            tap = ext_ref[row:row + tm, :] if r == 0 else es_ref[r - 1, row:row + tm, :]
            c = c + cw_ref[jj:jj + 1, :] * tap
        c = c + cb_ref[...]
        mu = jnp.mean(c, axis=-1, keepdims=True)
        cc = c - mu
        var = jnp.mean(cc * cc, axis=-1, keepdims=True)
        ln = cc * lax.rsqrt(var + EPS) * lng_ref[...] + lnb_ref[...]
        ya_ref[0] = (_silu(ln) * _silu(mm(_C_Z, _C_AZ))).astype(BF16)

    def w_saz():
        saz_ref[0] = _silu(mm(_C_AZ, _C_GA))

    def w_gate(ref, c0, half):
        def run():
            lo = half * (D_MODEL // 2)
            ref[0, :, lo:lo + D_MODEL // 2] = _sigmoid(mm(c0 + lo, c0 + lo + D_MODEL // 2))
        return run

    def w_keys():
        lane = lax.broadcasted_iota(jnp.int32, (tm, LANES), 1)
        pos = i * tm + lax.broadcasted_iota(jnp.int32, (tm, LANES), 0)
        kext = jnp.where(lane < 67, 1.0,
               jnp.where(lane < 70, (pos >> 6).astype(F32),
               jnp.where(lane < 73, (pos & 63).astype(F32), 0.0)))
        kp_ref[0] = jnp.where(lane < 64, mm(_C_KV, _C_IKW), kext).astype(BF16)
        ikw = mm(_C_IKW, _C_END)
        ik_ref[0] = ikw[:, :IDX_DIM].astype(BF16)
        iwt_ref[0] = ikw.T[IDX_DIM:IDX_DIM + N_IDX_HEADS, :] * (N_IDX_HEADS ** -0.5)

    def w_queries():
        sub = lax.broadcasted_iota(jnp.int32, (16, tm), 0)
        tpos = (i * tm + lax.broadcasted_iota(jnp.int32, (16, tm), 1)).astype(F32)
        qt = mm_t(_R_Q, _R_IQ) * (LOG2E * HEAD_DIM ** -0.5)
        for hd in range(N_HEADS):
            a = np.float32(2.0 ** (-8.0 * (hd + 1) / N_HEADS) * LOG2E)
            a1, a2, a3 = _bf16_split3(a)
            v = tpos * float(-a)
            v1 = v.astype(BF16).astype(F32)
            v2 = (v - v1).astype(BF16).astype(F32)
            v3 = v - v1 - v2
            ext = jnp.where(sub == 0, v1, jnp.where(sub == 1, v2, jnp.where(sub == 2, v3,
                  jnp.where(sub == 3, 64.0 * a1, jnp.where(sub == 4, 64.0 * a2,
                  jnp.where(sub == 5, 64.0 * a3, jnp.where(sub == 6, a1,
                  jnp.where(sub == 7, a2, jnp.where(sub == 8, a3, 0.0)))))))))
            qpt_ref[0, hd, 0:HEAD_DIM, :] = qt[hd * HEAD_DIM:(hd + 1) * HEAD_DIM, :].astype(BF16)
            qpt_ref[0, hd, HEAD_DIM:HEAD_DIM + 16, :] = ext.astype(BF16)
            qpt_ref[0, hd, HEAD_DIM + 16:, :] = jnp.zeros((LANES - HEAD_DIM - 16, tm), BF16)

    def w_index_values():
        iqt = mm_t(_R_IQ, _R_V) * (IDX_DIM ** -0.5)
        for hd in range(N_IDX_HEADS):
            iqt_ref[0, hd] = iqt[hd * IDX_DIM:(hd + 1) * IDX_DIM, :].astype(BF16)
        vt = mm_t(_R_V, _R_END).astype(BF16)
        for c in range(tm // KC):
            vt_ref[0, c] = vt[:, c * KC:(c + 1) * KC]

    for work in (w_conformer, w_saz, w_gate(ga_ref, _C_GA, 0), w_gate(ga_ref, _C_GA, 1),
                 w_gate(gb_ref, _C_GB, 0), w_gate(gb_ref, _C_GB, 1), w_keys, w_queries,
                 w_index_values):
        work()


def _threshold_of_key(k, bits, n_subnormal):
    f = lax.bitcast_convert_type(bits, F32)
    pos = jnp.where(k < n_subnormal, jnp.where(k == 0, 0.0, FLT_TINY), f)
    neg = jnp.where(k >= -n_subnormal, 0.0, jnp.where(f != f, -jnp.inf, f))
    return jnp.where(k >= 0, pos, neg)


def _key_to_float(k):
    return _threshold_of_key(k, jnp.where(k >= 0, k, k ^ jnp.int32(0x7FFFFFFF)), 2 ** 23)


def _key16_to_float(k):
    return _threshold_of_key(k, jnp.left_shift(jnp.where(k >= 0, k, k ^ jnp.int32(0x7FFF)), 16), 2 ** 7)


def _attn_kernel(iqt_ref, ik_ref, iwt_ref, qpt_ref, kp_ref, vt_ref, saz_ref, yb_ref,
                 st_ref, rt_ref, key_ref, bias_ref, ot_ref, lga_ref, lgb_ref, m_ref, mx_ref,
                 *acc_refs, n_keep):
    j = pl.program_id(1)
    t0 = j * TQ
    nkc = j + 1

    tcol = t0 + lax.broadcasted_iota(jnp.int32, (KC, TQ), 1)
    srow0 = lax.broadcasted_iota(jnp.int32, (KC, TQ), 0)

    def score_chunk(c):
        ks = pl.multiple_of(c * KC, KC)
        kic = ik_ref[0, pl.ds(ks, KC), :]
        acc = jnp.zeros((KC, TQ), F32)
        for hd in range(N_IDX_HEADS):
            d = jnp.dot(kic, iqt_ref[0, hd], preferred_element_type=F32)
            acc = acc + iwt_ref[0, hd:hd + 1, :] * jnp.maximum(d, 0.0)
        masked = jnp.where((ks + srow0) <= tcol, acc, -jnp.inf)
        st_ref[pl.ds(ks, KC), :] = masked
        rt_ref[pl.ds(ks, KC), :] = masked.astype(BF16)

    def score_pair(i, carry):
        score_chunk(2 * i)
        score_chunk(2 * i + 1)
        return carry

    lax.fori_loop(0, nkc // 2, score_pair, 0)

    @pl.when(nkc % 2 == 1)
    def _():
        score_chunk(nkc - 1)

    n_part = 4

    def count_ge(thr8):
        def body(c, cnt):
            ks = pl.multiple_of(c * KC, KC)
            s = st_ref[pl.ds(ks, KC), :].reshape(KC // (SUBLANES * n_part), n_part, SUBLANES, TQ)
            return cnt + jnp.sum((s >= thr8[None, None]).astype(jnp.int32), axis=0)
        cnt = lax.fori_loop(0, nkc, body, jnp.zeros((n_part, SUBLANES, TQ), jnp.int32))
        cnt8 = jnp.sum(cnt, axis=0)
        return jnp.broadcast_to(jnp.sum(cnt8, axis=0, keepdims=True), (SUBLANES, TQ))

    def count_ge16(thr_bf):
        def body(c, cnt):
            ks = pl.multiple_of(c * KC, KC)
            r = rt_ref[pl.ds(ks, KC), :].reshape(KC // (BF16_ROWS * 2), 2, BF16_ROWS, TQ)
            w = jnp.where(r >= thr_bf[None, None], jnp.ones((), BF16), jnp.zeros((), BF16))
            part = w[0]
            for g in range(1, w.shape[0]):
                part = part + w[g]
            return cnt + part
        cnt = lax.fori_loop(0, nkc, body, jnp.zeros((2, BF16_ROWS, TQ), BF16)).astype(F32)
        tot = jnp.sum(cnt[0] + cnt[1], axis=0, keepdims=True)
        return jnp.broadcast_to(tot, (SUBLANES, TQ))

    def thr16(k16):
        f = _key16_to_float(k16)
        return jnp.concatenate([f] * (BF16_ROWS // SUBLANES), axis=0).astype(BF16)

    k16 = jnp.where(count_ge16(thr16(jnp.zeros((SUBLANES, TQ), jnp.int32))) >= n_keep,
                    jnp.int32(0), jnp.int32(-(2 ** 15)))

    def bit16_body(it, k):
        cand = k | jnp.left_shift(jnp.int32(1), 14 - it)
        return jnp.where(count_ge16(thr16(cand)) >= n_keep, cand, k)

    k16 = lax.fori_loop(0, 15, bit16_body, k16)
    hb_bits = lax.bitcast_convert_type(_key16_to_float(k16), jnp.int32)
    k_hb = hb_bits ^ ((hb_bits >> 31) & jnp.int32(0x7FFFFFFF))
    k_base = k_hb - WINDOW // 2
    k_base_wide = k_hb - WINDOW
    w_lo = jnp.maximum(_key_to_float(k_base), -FLT_MAX)
    w_hi = _key_to_float(k_base + WINDOW + 1)

    def gather_body(c, carry):
        m1, m2, m3, ovf, above = carry
        ks = pl.multiple_of(c * KC, KC)
        s = st_ref[pl.ds(ks, KC), :].reshape(KC // (SUBLANES * n_part), n_part, SUBLANES, TQ)
        for g in range(s.shape[0]):
            sg = s[g]
            ge_hi = sg >= w_hi[None]
            above = above + ge_hi.astype(jnp.int32)
            v = jnp.where(ge_hi, -jnp.inf, jnp.where(sg >= w_lo[None], sg, -jnp.inf))
            t1 = jnp.minimum(m1, v)
            m1 = jnp.maximum(m1, v)
            t2 = jnp.minimum(m2, t1)
            m2 = jnp.maximum(m2, t1)
            ovf = jnp.maximum(ovf, jnp.minimum(m3, t2))
            m3 = jnp.maximum(m3, t2)
        return m1, m2, m3, ovf, above

    ninf = jnp.full((n_part, SUBLANES, TQ), -jnp.inf, F32)
    m1, m2, m3, ovf, above = lax.fori_loop(
        0, nkc, gather_body, (ninf, ninf, ninf, ninf, jnp.zeros((n_part, SUBLANES, TQ), jnp.int32)))
    gathered = jnp.concatenate([m1, m2, m3], axis=0)
    above8 = jnp.sum(above, axis=0)
    above8 = jnp.broadcast_to(jnp.sum(above8, axis=0, keepdims=True), (SUBLANES, TQ))

    def count_ge_gathered(thr8):
        c8 = jnp.sum((gathered >= thr8[None]).astype(jnp.int32), axis=0)
        return above8 + jnp.broadcast_to(jnp.sum(c8, axis=0, keepdims=True), (SUBLANES, TQ))

    def refine(count_fn, base):
        cnt0 = count_fn(_key_to_float(base))
        def body(it, carry):
            off, cnt_lo = carry
            cand = off | jnp.left_shift(jnp.int32(1), WINDOW_BITS - it)
            cnt = count_fn(_key_to_float(base + cand))
            ok = cnt >= n_keep
            return jnp.where(ok, cand, off), jnp.where(ok, cnt, cnt_lo)
        off, cnt_lo = lax.fori_loop(0, WINDOW_BITS + 1, body,
                                    (jnp.zeros((SUBLANES, TQ), jnp.int32), cnt0))
        key_ref[0] = base + off
        key_ref[1] = cnt_lo

    refine(count_ge_gathered, k_base)

    in_window = count_ge_gathered(w_lo)
    outside = jnp.where(in_window < n_keep, 1, jnp.where(above8 >= n_keep, 1, 0))
    outside = jnp.where(w_lo > -FLT_MAX, outside, 0)
    gathered_ok = (jnp.max(ovf) == -jnp.inf) & (jnp.max(outside) == 0)

    @pl.when(jnp.logical_not(gathered_ok))
    def _():
        refine(count_ge, k_base_wide)

    key = key_ref[0]
    cnt_lo = key_ref[1]
    lo8 = _key_to_float(key)
    hi8 = _key_to_float(key + 1)
    short = lo8 == -jnp.inf
    lo8 = jnp.where(short, -FLT_MAX, lo8)
    lo = lo8[0:1]

    has_tie = jnp.max(jnp.where(short, n_keep, cnt_lo)) > n_keep

    @pl.when(jnp.logical_not(has_tie))
    def _():
        def body(c, carry):
            ks = pl.multiple_of(c * KC, KC)
            bias_ref[pl.ds(ks, KC), :] = jnp.where(st_ref[pl.ds(ks, KC), :] >= lo, 0.0, NEG)
            return carry
        lax.fori_loop(0, nkc, body, 0)

    @pl.when(has_tie)
    def _():
        hi = hi8[0:1]
        key_ref[1] = count_ge_gathered(hi8)

        @pl.when(jnp.logical_not(gathered_ok))
        def _():
            key_ref[1] = count_ge(hi8)

        need = (n_keep - key_ref[1]).astype(F32)[0:1]
        tri = (lax.broadcasted_iota(jnp.int32, (KC, KC), 0)
               >= lax.broadcasted_iota(jnp.int32, (KC, KC), 1)).astype(BF16)

        def body(c, taken):
            ks = pl.multiple_of(c * KC, KC)
            s = st_ref[pl.ds(ks, KC), :]
            ge_lo = s >= lo
            ge_hi = s >= hi
            tied = jnp.where(ge_lo, jnp.where(ge_hi, 0.0, 1.0), 0.0).astype(BF16)
            rank = taken + jnp.dot(tri, tied, preferred_element_type=F32)
            bias_ref[pl.ds(ks, KC), :] = jnp.where(
                ge_lo, jnp.where(ge_hi, 0.0, jnp.where(rank <= need, 0.0, NEG)), NEG)
            return rank[KC - 1:KC, :]

        lax.fori_loop(0, nkc, body, jnp.zeros((1, TQ), F32))

    for acc_ref in acc_refs:
        acc_ref[...] = jnp.zeros(acc_ref.shape, F32)
    ones_rows = (lax.broadcasted_iota(jnp.int32, (N_ACC_ROWS - HEAD_DIM, KC), 0) == 0).astype(BF16)

    m_ref[...] = jnp.full(m_ref.shape, NEG, F32)

    def heads_step(c_qk, qk_ref, c_pv, pv_ref):
        if c_qk is not None:
            ks = pl.multiple_of(c_qk * KC, KC)
            kpc = kp_ref[0, pl.ds(ks, KC), :]
            bias = bias_ref[pl.ds(ks, KC), :]
        if c_pv is not None:
            vtc = jnp.concatenate([vt_ref[0, c_pv], ones_rows], axis=0)
            m_all, mx_cur = m_ref[...], mx_ref[...]
        m_rows, mx_rows = [], []
        for hd in range(N_HEADS):
            if c_qk is not None:
                lg = jnp.dot(kpc, qpt_ref[0, hd], preferred_element_type=F32) + bias
                qk_ref[hd] = lg
                mx_rows.append(jnp.max(lg, axis=0, keepdims=True))
            if c_pv is not None:
                m_old = m_all[hd:hd + 1, :]
                m_new = jnp.maximum(m_old, mx_cur[hd:hd + 1, :])
                alpha = jnp.exp2(m_old - m_new)
                p = jnp.exp2(pv_ref[hd] - m_new).astype(BF16)
                acc_refs[hd][...] = alpha * acc_refs[hd][...] + jnp.dot(
                    vtc, p, preferred_element_type=F32)
                m_rows.append(m_new)
        if c_pv is not None:
            m_ref[...] = jnp.concatenate(m_rows, axis=0)
        if c_qk is not None:
            mx_ref[...] = jnp.concatenate(mx_rows, axis=0)

    n_steps = nkc - 1
    heads_step(0, lga_ref, None, None)

    def pair_body(i, carry):
        heads_step(2 * i + 1, lgb_ref, 2 * i, lga_ref)
        heads_step(2 * i + 2, lga_ref, 2 * i + 1, lgb_ref)
        return carry

    lax.fori_loop(0, n_steps // 2, pair_body, 0)

    @pl.when(n_steps % 2 == 1)
    def _():
        heads_step(n_steps, lgb_ref, n_steps - 1, lga_ref)
        heads_step(None, None, n_steps, lgb_ref)

    @pl.when(n_steps % 2 == 0)
    def _():
        heads_step(None, None, n_steps, lga_ref)

    for hd in range(N_HEADS):
        acc = acc_refs[hd][...]
        ot_ref[hd * HEAD_DIM:(hd + 1) * HEAD_DIM, :] = acc[0:HEAD_DIM] / acc[HEAD_DIM:HEAD_DIM + 1]
    yb_ref[0] = (ot_ref[...].T * saz_ref[0]).astype(BF16)


def _post_kernel(x_ref, ya_ref, yb_ref, ga_ref, gb_ref, p_ref,
                 wa_ref, wb_ref, wo_ref, pg_ref, wpg_ref, wpp_ref, fg_ref, out_ref, *, tm, final_norm):
    for r0 in range(0, tm, tm // 2):
        rows = slice(r0, r0 + tm // 2)
        merged = (ga_ref[0, rows] * jnp.dot(ya_ref[0, rows], wa_ref[...], preferred_element_type=F32)
                  + gb_ref[0, rows] * jnp.dot(yb_ref[0, rows], wb_ref[...],
                                              preferred_element_type=F32))
        x1 = x_ref[0, rows] + jnp.dot(merged.astype(BF16), wo_ref[...], preferred_element_type=F32)
        gate = _sigmoid(jnp.dot(_rms(x1, pg_ref[...]).astype(BF16), wpg_ref[...],
                                preferred_element_type=F32))
        x2 = x1 + gate * jnp.dot(p_ref[0, rows].astype(BF16), wpp_ref[...],
                                 preferred_element_type=F32)
        out_ref[0, rows] = _rms(x2, fg_ref[...]) if final_norm else x2


def _pack_w_in(w):
    o_in, o_z, o_q, o_k, o_v, o_az, o_iq, o_ik, o_iw, o_ga, o_gb = (
        0, 1024, 1536, 2048, 2112, 2176, 2688, 3200, 3264, 3272, 4296)
    pad = jnp.zeros((D_MODEL, LANES - IDX_DIM - N_IDX_HEADS), w.dtype)
    cols = [w[:, o_in:o_z], w[:, o_z:o_q], w[:, o_az:o_iq],
            w[:, o_ga:o_gb], w[:, o_gb:o_gb + D_MODEL], w[:, o_k:o_az],
            w[:, o_ik:o_ga], pad]
    w_tok = jnp.concatenate(cols, axis=1).astype(BF16)
    rows = [w[:, o_q:o_k], w[:, o_iq:o_ik], w[:, o_v:o_az]]
    w_feat = jnp.concatenate(rows, axis=1).T.astype(BF16)
    return w_tok, w_feat


def _layer(x, p, norm_g, w_in, conv_w, conv_b, ln_g, ln_b, w_a, w_b, w_o,
           ple_g, w_pg, w_pp, out_g, final_norm):
    B, L, D = x.shape
    n_keep = min(TOPK_MAX, L // 4)
    cp = functools.partial(pltpu.CompilerParams, vmem_limit_bytes=VMEM_LIMIT)
    row = lambda a: a.reshape(1, -1)

    tm1 = TM_PROJ
    w_tok, w_feat = _pack_w_in(w_in)
    tok = lambda w: pl.BlockSpec((1, tm1, w), lambda b, i: (b, i, 0))
    featm = lambda r: pl.BlockSpec((1, N_HEADS, r, tm1), lambda b, i: (b, 0, 0, i))
    const = lambda a: pl.BlockSpec(a.shape, lambda b, i: (0,) * a.ndim, pipeline_mode=pl.Buffered(1))
    f32s = lambda w: jax.ShapeDtypeStruct((B, L, w), F32)
    consts1 = [row(norm_g), w_tok, w_feat, conv_w, row(conv_b), row(ln_g), row(ln_b)]
    ya, saz, ga, gb, qpt, kp, vt, iqt, ik, iwt = pl.pallas_call(
        functools.partial(_proj_kernel, tm=tm1),
        grid=(B, L // tm1),
        in_specs=[tok(D)] + [const(a) for a in consts1],
        out_specs=[tok(D_CONV), tok(D_ATTN), tok(D), tok(D),
                   featm(LANES), tok(LANES),
                   pl.BlockSpec((1, tm1 // KC, HEAD_DIM, KC), lambda b, i: (b, i, 0, 0)),
                   featm(IDX_DIM), tok(IDX_DIM),
                   pl.BlockSpec((1, N_IDX_HEADS, tm1), lambda b, i: (b, 0, i))],
        out_shape=[jax.ShapeDtypeStruct((B, L, D_CONV), BF16), f32s(D_ATTN), f32s(D), f32s(D),
                   jax.ShapeDtypeStruct((B, N_HEADS, LANES, L), BF16),
                   jax.ShapeDtypeStruct((B, L, LANES), BF16),
                   jax.ShapeDtypeStruct((B, L // KC, HEAD_DIM, KC), BF16),
                   jax.ShapeDtypeStruct((B, N_IDX_HEADS, IDX_DIM, L), BF16),
                   jax.ShapeDtypeStruct((B, L, IDX_DIM), BF16),
                   jax.ShapeDtypeStruct((B, N_IDX_HEADS, L), F32)],
        scratch_shapes=[pltpu.VMEM((HALO + tm1, D_CONV), F32),
                        pltpu.VMEM((SUBLANES - 1, HALO + tm1 - SUBLANES, D_CONV), F32),
                        pltpu.VMEM((HALO, D_CONV), F32)],
        compiler_params=cp(dimension_semantics=("arbitrary", "arbitrary")),
        name="proj",
    )(x, *consts1)

    yb = pl.pallas_call(
        functools.partial(_attn_kernel, n_keep=n_keep),
        grid=(B, L // TQ),
        in_specs=[pl.BlockSpec((1, N_IDX_HEADS, IDX_DIM, TQ), lambda b, j: (b, 0, 0, j)),
                  pl.BlockSpec((1, L, IDX_DIM), lambda b, j: (b, 0, 0)),
                  pl.BlockSpec((1, N_IDX_HEADS, TQ), lambda b, j: (b, 0, j)),
                  pl.BlockSpec((1, N_HEADS, LANES, TQ), lambda b, j: (b, 0, 0, j)),
                  pl.BlockSpec((1, L, LANES), lambda b, j: (b, 0, 0)),
                  pl.BlockSpec((1, L // KC, HEAD_DIM, KC), lambda b, j: (b, 0, 0, 0)),
                  pl.BlockSpec((1, TQ, D_ATTN), lambda b, j: (b, j, 0))],
        out_specs=pl.BlockSpec((1, TQ, D_ATTN), lambda b, j: (b, j, 0)),
        out_shape=jax.ShapeDtypeStruct((B, L, D_ATTN), BF16),
        scratch_shapes=[pltpu.VMEM((L, TQ), F32),
                        pltpu.VMEM((L, TQ), BF16),
                        pltpu.VMEM((2, SUBLANES, TQ), jnp.int32),
                        pltpu.VMEM((L, TQ), F32),
                        pltpu.VMEM((D_ATTN, TQ), F32),
                        pltpu.VMEM((N_HEADS, KC, TQ), F32),
                        pltpu.VMEM((N_HEADS, KC, TQ), F32),
                        pltpu.VMEM((N_HEADS, TQ), F32),
                        pltpu.VMEM((N_HEADS, TQ), F32)]
                       + [pltpu.VMEM((N_ACC_ROWS, TQ), F32)] * N_HEADS,
        compiler_params=cp(dimension_semantics=("arbitrary", "arbitrary")),
        name="attn",
    )(iqt, ik, iwt, qpt, kp, vt, saz)

    tm3 = TM_POST
    tok = lambda w: pl.BlockSpec((1, tm3, w), lambda b, i: (b, i, 0))
    consts3 = [w_a.astype(BF16), w_b.astype(BF16), w_o.astype(BF16), row(ple_g),
               w_pg.astype(BF16), w_pp.astype(BF16), row(out_g)]
    return pl.pallas_call(
        functools.partial(_post_kernel, tm=tm3, final_norm=final_norm),
        grid=(B, L // tm3),
        in_specs=[tok(D), tok(D_CONV), tok(D_ATTN), tok(D), tok(D), tok(D_PLE)]
                 + [const(a) for a in consts3],
        out_specs=tok(D),
        out_shape=jax.ShapeDtypeStruct((B, L, D), F32),
        compiler_params=cp(dimension_semantics=("arbitrary", "arbitrary")),
        name="post",
    )(x, ya, yb, ga, gb, p, *consts3)


def kernel(x, p, norm_g, w_in, conv_w, conv_b, conv_ln_g, conv_ln_b, w_a_out, w_b_out,
           w_o, ple_norm_g, w_ple_gate, w_ple_proj, final_g):
    depth = w_in.shape[0]
    for i in range(depth):
        x = _layer(x, p[i], norm_g[i], w_in[i], conv_w[i], conv_b[i], conv_ln_g[i],
                   conv_ln_b[i], w_a_out[i], w_b_out[i], w_o[i], ple_norm_g[i],
                   w_ple_gate[i], w_ple_proj[i], final_g, i == depth - 1)
    return x
```

```python
import functools
import math

import jax
import jax.numpy as jnp
import numpy as np
from jax import lax
from jax.experimental import pallas as pl
from jax.experimental.pallas import tpu as pltpu

D_MODEL = 1024
D_PLE = 256
D_CONV = 512
CONV_WIDTH = 31
N_HEADS = 8
HEAD_DIM = 64
D_ATTN = N_HEADS * HEAD_DIM
N_IDX_HEADS = 8
IDX_DIM = 64
TOPK_MAX = 256
EPS = 1e-6

LANES = 128
SUBLANES = 8
TQ = 256
KC = 256
TM_PROJ = 512
TM_POST = 512
HALO = 32
NEG = -1e30
FLT_MAX = float(np.finfo(np.float32).max)
FLT_TINY = float(np.finfo(np.float32).tiny)
BF16_ROWS = 16
WINDOW_BITS = 16
WINDOW = 2 ** WINDOW_BITS
LOG2E = math.log2(math.e)
N_ACC_ROWS = HEAD_DIM + 16
VMEM_LIMIT = 56 * 1024 * 1024

F32 = jnp.float32
BF16 = jnp.bfloat16

_C_VAL, _C_GATE, _C_Z, _C_AZ, _C_GA, _C_GB, _C_KV, _C_IKW, _C_END = (
    0, 512, 1024, 1536, 2048, 3072, 4096, 4224, 4352)
_R_Q, _R_IQ, _R_V, _R_END = 0, 512, 1024, 1088

_NT = (((1,), (1,)), ((), ()))


def _bf16_split3(a):
    a = np.float32(a)
    p1 = np.float32(a.astype(BF16))
    p2 = np.float32(np.float32(a - p1).astype(BF16))
    p3 = np.float32(a - p1 - p2)
    assert np.float32(np.float32(p3).astype(BF16)) == p3
    return float(p1), float(p2), float(p3)


def _sigmoid(x):
    return 1.0 / (1.0 + jnp.exp(-x))


def _silu(x):
    return x * _sigmoid(x)


def _rms(x, g):
    return x * lax.rsqrt(jnp.mean(x * x, axis=-1, keepdims=True) + EPS) * g


def _proj_kernel(x_ref, g_ref, w_ref, wt_ref, cw_ref, cb_ref, lng_ref, lnb_ref,
                 ya_ref, saz_ref, ga_ref, gb_ref,
                 qpt_ref, kp_ref, vt_ref, iqt_ref, ik_ref, iwt_ref,
                 ext_ref, es_ref, carry_ref, *, tm):
    i = pl.program_id(1)
    h = _rms(x_ref[0], g_ref[...]).astype(BF16)

    def mm(a, b):
        return jnp.dot(h, w_ref[:, a:b], preferred_element_type=F32)

    def mm_t(a, b):
        return lax.dot_general(wt_ref[a:b, :], h, _NT, preferred_element_type=F32)

    u = mm(_C_VAL, _C_GATE) * _sigmoid(mm(_C_GATE, _C_Z))
    prev = carry_ref[...]
    ext_ref[0:HALO, :] = jnp.where(i > 0, prev, jnp.zeros_like(prev))
    ext_ref[HALO:, :] = u
    carry_ref[...] = u[tm - HALO:, :]
    n_sh = HALO + tm - SUBLANES
    for r in range(1, SUBLANES):
        es_ref[r - 1] = ext_ref[r:r + n_sh, :]

    def w_conformer():
        base = HALO - (CONV_WIDTH - 1)
        c = jnp.zeros((tm, D_CONV), F32)
        for jj in range(CONV_WIDTH):
            row8, r = divmod(base + jj, SUBLANES)
            row = row8 * SUBLANES
            tap = ext_ref[row:row + tm, :] if r == 0 else es_ref[r - 1, row:row + tm, :]
            c = c + cw_ref[jj:jj + 1, :] * tap
        c = c + cb_ref[...]
        mu = jnp.mean(c, axis=-1, keepdims=True)
        cc = c - mu
        var = jnp.mean(cc * cc, axis=-1, keepdims=True)
        ln = cc * lax.rsqrt(var + EPS) * lng_ref[...] + lnb_ref[...]
        ya_ref[0] = (_silu(ln) * _silu(mm(_C_Z, _C_AZ))).astype(BF16)

    def w_saz():
        saz_ref[0] = _silu(mm(_C_AZ, _C_GA))

    def w_gate(ref, c0, half):
        def run():
            lo = half * (D_MODEL // 2)
            ref[0, :, lo:lo + D_MODEL // 2] = _sigmoid(mm(c0 + lo, c0 + lo + D_MODEL // 2))
        return run

    def w_keys():
        lane = lax.broadcasted_iota(jnp.int32, (tm, LANES), 1)
        pos = i * tm + lax.broadcasted_iota(jnp.int32, (tm, LANES), 0)
        kext = jnp.where(lane < 67, 1.0,
               jnp.where(lane < 70, (pos >> 6).astype(F32),
               jnp.where(lane < 73, (pos & 63).astype(F32), 0.0)))
        kp_ref[0] = jnp.where(lane < 64, mm(_C_KV, _C_IKW), kext).astype(BF16)
        ikw = mm(_C_IKW, _C_END)
        ik_ref[0] = ikw[:, :IDX_DIM].astype(BF16)
        iwt_ref[0] = ikw.T[IDX_DIM:IDX_DIM + N_IDX_HEADS, :] * (N_IDX_HEADS ** -0.5)

    def w_queries():
        sub = lax.broadcasted_iota(jnp.int32, (16, tm), 0)
        tpos = (i * tm + lax.broadcasted_iota(jnp.int32, (16, tm), 1)).astype(F32)
        qt = mm_t(_R_Q, _R_IQ) * (LOG2E * HEAD_DIM ** -0.5)
        for hd in range(N_HEADS):
            a = np.float32(2.0 ** (-8.0 * (hd + 1) / N_HEADS) * LOG2E)
            a1, a2, a3 = _bf16_split3(a)
            v = tpos * float(-a)
            v1 = v.astype(BF16).astype(F32)
            v2 = (v - v1).astype(BF16).astype(F32)
            v3 = v - v1 - v2
            ext = jnp.where(sub == 0, v1, jnp.where(sub == 1, v2, jnp.where(sub == 2, v3,
                  jnp.where(sub == 3, 64.0 * a1, jnp.where(sub == 4, 64.0 * a2,
                  jnp.where(sub == 5, 64.0 * a3, jnp.where(sub == 6, a1,
                  jnp.where(sub == 7, a2, jnp.where(sub == 8, a3, 0.0)))))))))
            qpt_ref[0, hd, 0:HEAD_DIM, :] = qt[hd * HEAD_DIM:(hd + 1) * HEAD_DIM, :].astype(BF16)
            qpt_ref[0, hd, HEAD_DIM:HEAD_DIM + 16, :] = ext.astype(BF16)
            qpt_ref[0, hd, HEAD_DIM + 16:, :] = jnp.zeros((LANES - HEAD_DIM - 16, tm), BF16)

    def w_index_values():
        iqt = mm_t(_R_IQ, _R_V) * (IDX_DIM ** -0.5)
        for hd in range(N_IDX_HEADS):
            iqt_ref[0, hd] = iqt[hd * IDX_DIM:(hd + 1) * IDX_DIM, :].astype(BF16)
        vt = mm_t(_R_V, _R_END).astype(BF16)
        for c in range(tm // KC):
            vt_ref[0, c] = vt[:, c * KC:(c + 1) * KC]

    for work in (w_conformer, w_saz, w_gate(ga_ref, _C_GA, 0), w_gate(ga_ref, _C_GA, 1),
                 w_gate(gb_ref, _C_GB, 0), w_gate(gb_ref, _C_GB, 1), w_keys, w_queries,
                 w_index_values):
        work()


def _threshold_of_key(k, bits, n_subnormal):
    f = lax.bitcast_convert_type(bits, F32)
    pos = jnp.where(k < n_subnormal, jnp.where(k == 0, 0.0, FLT_TINY), f)
    neg = jnp.where(k >= -n_subnormal, 0.0, jnp.where(f != f, -jnp.inf, f))
    return jnp.where(k >= 0, pos, neg)


def _key_to_float(k):
    return _threshold_of_key(k, jnp.where(k >= 0, k, k ^ jnp.int32(0x7FFFFFFF)), 2 ** 23)


def _key16_to_float(k):
    return _threshold_of_key(k, jnp.left_shift(jnp.where(k >= 0, k, k ^ jnp.int32(0x7FFF)), 16), 2 ** 7)


def _attn_kernel(iqt_ref, ik_ref, iwt_ref, qpt_ref, kp_ref, vt_ref, saz_ref, yb_ref,
                 st_ref, rt_ref, key_ref, bias_ref, ot_ref, lga_ref, lgb_ref, *head_refs, n_keep):
    m_refs, mxa_refs, mxb_refs, acc_refs = (head_refs[g * N_HEADS:(g + 1) * N_HEADS] for g in range(4))
    j = pl.program_id(1)
    t0 = j * TQ
    nkc = j + 1

    tcol = t0 + lax.broadcasted_iota(jnp.int32, (KC, TQ), 1)
    srow0 = lax.broadcasted_iota(jnp.int32, (KC, TQ), 0)

    def score_chunk(c):
        ks = pl.multiple_of(c * KC, KC)
        kic = ik_ref[0, pl.ds(ks, KC), :]
        acc = jnp.zeros((KC, TQ), F32)
        for hd in range(N_IDX_HEADS):
            d = jnp.dot(kic, iqt_ref[0, hd], preferred_element_type=F32)
            acc = acc + iwt_ref[0, hd:hd + 1, :] * jnp.maximum(d, 0.0)
        masked = jnp.where((ks + srow0) <= tcol, acc, -jnp.inf)
        st_ref[pl.ds(ks, KC), :] = masked
        rt_ref[pl.ds(ks, KC), :] = masked.astype(BF16)

    def score_pair(i, carry):
        score_chunk(2 * i)
        score_chunk(2 * i + 1)
        return carry

    lax.fori_loop(0, nkc // 2, score_pair, 0)

    @pl.when(nkc % 2 == 1)
    def _():
        score_chunk(nkc - 1)

    n_part = 4

    def count_ge(thr8):
        def body(c, cnt):
            ks = pl.multiple_of(c * KC, KC)
            s = st_ref[pl.ds(ks, KC), :].reshape(KC // (SUBLANES * n_part), n_part, SUBLANES, TQ)
            return cnt + jnp.sum((s >= thr8[None, None]).astype(jnp.int32), axis=0)
        cnt = lax.fori_loop(0, nkc, body, jnp.zeros((n_part, SUBLANES, TQ), jnp.int32))
        cnt8 = jnp.sum(cnt, axis=0)
        return jnp.broadcast_to(jnp.sum(cnt8, axis=0, keepdims=True), (SUBLANES, TQ))

    def count_ge16(thr_bf):
        def body(c, cnt):
            ks = pl.multiple_of(c * KC, KC)
            r = rt_ref[pl.ds(ks, KC), :].reshape(KC // (BF16_ROWS * 2), 2, BF16_ROWS, TQ)
            w = jnp.where(r >= thr_bf[None, None], jnp.ones((), BF16), jnp.zeros((), BF16))
            part = w[0]
            for g in range(1, w.shape[0]):
                part = part + w[g]
            return cnt + part
        cnt = lax.fori_loop(0, nkc, body, jnp.zeros((2, BF16_ROWS, TQ), BF16)).astype(F32)
        tot = jnp.sum(cnt[0] + cnt[1], axis=0, keepdims=True)
        return jnp.broadcast_to(tot, (SUBLANES, TQ))

    def thr16(k16):
        f = _key16_to_float(k16)
        return jnp.concatenate([f] * (BF16_ROWS // SUBLANES), axis=0).astype(BF16)

    k16 = jnp.where(count_ge16(thr16(jnp.zeros((SUBLANES, TQ), jnp.int32))) >= n_keep,
                    jnp.int32(0), jnp.int32(-(2 ** 15)))

    def bit16_body(it, k):
        cand = k | jnp.left_shift(jnp.int32(1), 14 - it)
        return jnp.where(count_ge16(thr16(cand)) >= n_keep, cand, k)

    k16 = lax.fori_loop(0, 15, bit16_body, k16)
    hb_bits = lax.bitcast_convert_type(_key16_to_float(k16), jnp.int32)
    k_hb = hb_bits ^ ((hb_bits >> 31) & jnp.int32(0x7FFFFFFF))
    k_base = k_hb - WINDOW // 2
    k_base_wide = k_hb - WINDOW
    w_lo = jnp.maximum(_key_to_float(k_base), -FLT_MAX)
    w_hi = _key_to_float(k_base + WINDOW + 1)

    def gather_body(c, carry):
        m1, m2, m3, ovf, above = carry
        ks = pl.multiple_of(c * KC, KC)
        s = st_ref[pl.ds(ks, KC), :].reshape(KC // (SUBLANES * n_part), n_part, SUBLANES, TQ)
        for g in range(s.shape[0]):
            sg = s[g]
            ge_hi = sg >= w_hi[None]
            above = above + ge_hi.astype(jnp.int32)
            v = jnp.where(ge_hi, -jnp.inf, jnp.where(sg >= w_lo[None], sg, -jnp.inf))
            t1 = jnp.minimum(m1, v)
            m1 = jnp.maximum(m1, v)
            t2 = jnp.minimum(m2, t1)
            m2 = jnp.maximum(m2, t1)
            ovf = jnp.maximum(ovf, jnp.minimum(m3, t2))
            m3 = jnp.maximum(m3, t2)
        return m1, m2, m3, ovf, above

    ninf = jnp.full((n_part, SUBLANES, TQ), -jnp.inf, F32)
    m1, m2, m3, ovf, above = lax.fori_loop(
        0, nkc, gather_body, (ninf, ninf, ninf, ninf, jnp.zeros((n_part, SUBLANES, TQ), jnp.int32)))
    gathered = jnp.concatenate([m1, m2, m3], axis=0)
    above8 = jnp.sum(above, axis=0)
    above8 = jnp.broadcast_to(jnp.sum(above8, axis=0, keepdims=True), (SUBLANES, TQ))

    def count_ge_gathered(thr8):
        c8 = jnp.sum((gathered >= thr8[None]).astype(jnp.int32), axis=0)
        return above8 + jnp.broadcast_to(jnp.sum(c8, axis=0, keepdims=True), (SUBLANES, TQ))

    def refine(count_fn, base):
        cnt0 = count_fn(_key_to_float(base))
        def body(it, carry):
            off, cnt_lo = carry
            cand = off | jnp.left_shift(jnp.int32(1), WINDOW_BITS - it)
            cnt = count_fn(_key_to_float(base + cand))
            ok = cnt >= n_keep
            return jnp.where(ok, cand, off), jnp.where(ok, cnt, cnt_lo)
        off, cnt_lo = lax.fori_loop(0, WINDOW_BITS + 1, body,
                                    (jnp.zeros((SUBLANES, TQ), jnp.int32), cnt0))
        key_ref[0] = base + off
        key_ref[1] = cnt_lo

    refine(count_ge_gathered, k_base)

    in_window = count_ge_gathered(w_lo)
    outside = jnp.where(in_window < n_keep, 1, jnp.where(above8 >= n_keep, 1, 0))
    outside = jnp.where(w_lo > -FLT_MAX, outside, 0)
    gathered_ok = (jnp.max(ovf) == -jnp.inf) & (jnp.max(outside) == 0)

    @pl.when(jnp.logical_not(gathered_ok))
    def _():
        refine(count_ge, k_base_wide)

    key = key_ref[0]
    cnt_lo = key_ref[1]
    lo8 = _key_to_float(key)
    hi8 = _key_to_float(key + 1)
    short = lo8 == -jnp.inf
    lo8 = jnp.where(short, -FLT_MAX, lo8)
    lo = lo8[0:1]

    has_tie = jnp.max(jnp.where(short, n_keep, cnt_lo)) > n_keep

    @pl.when(jnp.logical_not(has_tie))
    def _():
        def body(c, carry):
            ks = pl.multiple_of(c * KC, KC)
            bias_ref[pl.ds(ks, KC), :] = jnp.where(st_ref[pl.ds(ks, KC), :] >= lo, 0.0, NEG)
            return carry
        lax.fori_loop(0, nkc, body, 0)

    @pl.when(has_tie)
    def _():
        hi = hi8[0:1]
        key_ref[1] = count_ge_gathered(hi8)

        @pl.when(jnp.logical_not(gathered_ok))
        def _():
            key_ref[1] = count_ge(hi8)

        need = (n_keep - key_ref[1]).astype(F32)[0:1]
        tri = (lax.broadcasted_iota(jnp.int32, (KC, KC), 0)
               >= lax.broadcasted_iota(jnp.int32, (KC, KC), 1)).astype(BF16)

        def body(c, taken):
            ks = pl.multiple_of(c * KC, KC)
            s = st_ref[pl.ds(ks, KC), :]
            ge_lo = s >= lo
            ge_hi = s >= hi
            tied = jnp.where(ge_lo, jnp.where(ge_hi, 0.0, 1.0), 0.0).astype(BF16)
            rank = taken + jnp.dot(tri, tied, preferred_element_type=F32)
            bias_ref[pl.ds(ks, KC), :] = jnp.where(
                ge_lo, jnp.where(ge_hi, 0.0, jnp.where(rank <= need, 0.0, NEG)), NEG)
            return rank[KC - 1:KC, :]

        lax.fori_loop(0, nkc, body, jnp.zeros((1, TQ), F32))

    for acc_ref in acc_refs:
        acc_ref[...] = jnp.zeros(acc_ref.shape, F32)
    ones_rows = (lax.broadcasted_iota(jnp.int32, (N_ACC_ROWS - HEAD_DIM, KC), 0) == 0).astype(BF16)

    for m_ref in m_refs:
        m_ref[...] = jnp.full(m_ref.shape, NEG, F32)
    buf_a = (lga_ref, mxa_refs)
    buf_b = (lgb_ref, mxb_refs)

    def heads_step(c_qk, qk_buf, c_pv, pv_buf):
        if c_qk is not None:
            ks = pl.multiple_of(c_qk * KC, KC)
            kpc = kp_ref[0, pl.ds(ks, KC), :]
            bias = bias_ref[pl.ds(ks, KC), :]
        if c_pv is not None:
            vtc = jnp.concatenate([vt_ref[0, c_pv], ones_rows], axis=0)
        for hd in range(N_HEADS):
            if c_qk is not None:
                lg = jnp.dot(kpc, qpt_ref[0, hd], preferred_element_type=F32) + bias
                qk_buf[0][hd] = lg
                qk_buf[1][hd][0:1, :] = jnp.max(lg, axis=0, keepdims=True)
            if c_pv is not None:
                m_old = m_refs[hd][0:1, :]
                m_new = jnp.maximum(m_old, pv_buf[1][hd][0:1, :])
                alpha = jnp.exp2(m_old - m_new)
                p = jnp.exp2(pv_buf[0][hd] - m_new).astype(BF16)
                acc_refs[hd][...] = alpha * acc_refs[hd][...] + jnp.dot(
                    vtc, p, preferred_element_type=F32)
                m_refs[hd][0:1, :] = m_new

    n_steps = nkc - 1
    heads_step(0, buf_a, None, None)

    def pair_body(i, carry):
        heads_step(2 * i + 1, buf_b, 2 * i, buf_a)
        heads_step(2 * i + 2, buf_a, 2 * i + 1, buf_b)
        return carry

    lax.fori_loop(0, n_steps // 2, pair_body, 0)

    @pl.when(n_steps % 2 == 1)
    def _():
        heads_step(n_steps, buf_b, n_steps - 1, buf_a)
        heads_step(None, None, n_steps, buf_b)

    @pl.when(n_steps % 2 == 0)
    def _():
        heads_step(None, None, n_steps, buf_a)

    for hd in range(N_HEADS):
        acc = acc_refs[hd][...]
        ot_ref[hd * HEAD_DIM:(hd + 1) * HEAD_DIM, :] = acc[0:HEAD_DIM] / acc[HEAD_DIM:HEAD_DIM + 1]
    yb_ref[0] = (ot_ref[...].T * saz_ref[0]).astype(BF16)


def _post_kernel(x_ref, ya_ref, yb_ref, ga_ref, gb_ref, p_ref,
                 wa_ref, wb_ref, wo_ref, pg_ref, wpg_ref, wpp_ref, fg_ref, out_ref, *, tm, final_norm):
    for r0 in range(0, tm, tm // 2):
        rows = slice(r0, r0 + tm // 2)
        merged = (ga_ref[0, rows] * jnp.dot(ya_ref[0, rows], wa_ref[...], preferred_element_type=F32)
                  + gb_ref[0, rows] * jnp.dot(yb_ref[0, rows], wb_ref[...],
                                              preferred_element_type=F32))
        x1 = x_ref[0, rows] + jnp.dot(merged.astype(BF16), wo_ref[...], preferred_element_type=F32)
        gate = _sigmoid(jnp.dot(_rms(x1, pg_ref[...]).astype(BF16), wpg_ref[...],
                                preferred_element_type=F32))
        x2 = x1 + gate * jnp.dot(p_ref[0, rows].astype(BF16), wpp_ref[...],
                                 preferred_element_type=F32)
        out_ref[0, rows] = _rms(x2, fg_ref[...]) if final_norm else x2


def _pack_w_in(w):
    o_in, o_z, o_q, o_k, o_v, o_az, o_iq, o_ik, o_iw, o_ga, o_gb = (
        0, 1024, 1536, 2048, 2112, 2176, 2688, 3200, 3264, 3272, 4296)
    pad = jnp.zeros((D_MODEL, LANES - IDX_DIM - N_IDX_HEADS), w.dtype)
    cols = [w[:, o_in:o_z], w[:, o_z:o_q], w[:, o_az:o_iq],
            w[:, o_ga:o_gb], w[:, o_gb:o_gb + D_MODEL], w[:, o_k:o_az],
            w[:, o_ik:o_ga], pad]
    w_tok = jnp.concatenate(cols, axis=1).astype(BF16)
    rows = [w[:, o_q:o_k], w[:, o_iq:o_ik], w[:, o_v:o_az]]
    w_feat = jnp.concatenate(rows, axis=1).T.astype(BF16)
    return w_tok, w_feat


def _layer(x, p, norm_g, w_in, conv_w, conv_b, ln_g, ln_b, w_a, w_b, w_o,
           ple_g, w_pg, w_pp, out_g, final_norm):
    B, L, D = x.shape
    n_keep = min(TOPK_MAX, L // 4)
    cp = functools.partial(pltpu.CompilerParams, vmem_limit_bytes=VMEM_LIMIT)
    row = lambda a: a.reshape(1, -1)

    tm1 = TM_PROJ
    w_tok, w_feat = _pack_w_in(w_in)
    tok = lambda w: pl.BlockSpec((1, tm1, w), lambda b, i: (b, i, 0))
    featm = lambda r: pl.BlockSpec((1, N_HEADS, r, tm1), lambda b, i: (b, 0, 0, i))
    const = lambda a: pl.BlockSpec(a.shape, lambda b, i: (0,) * a.ndim, pipeline_mode=pl.Buffered(1))
    f32s = lambda w: jax.ShapeDtypeStruct((B, L, w), F32)
    consts1 = [row(norm_g), w_tok, w_feat, conv_w, row(conv_b), row(ln_g), row(ln_b)]
    ya, saz, ga, gb, qpt, kp, vt, iqt, ik, iwt = pl.pallas_call(
        functools.partial(_proj_kernel, tm=tm1),
        grid=(B, L // tm1),
        in_specs=[tok(D)] + [const(a) for a in consts1],
        out_specs=[tok(D_CONV), tok(D_ATTN), tok(D), tok(D),
                   featm(LANES), tok(LANES),
                   pl.BlockSpec((1, tm1 // KC, HEAD_DIM, KC), lambda b, i: (b, i, 0, 0)),
                   featm(IDX_DIM), tok(IDX_DIM),
                   pl.BlockSpec((1, N_IDX_HEADS, tm1), lambda b, i: (b, 0, i))],
        out_shape=[jax.ShapeDtypeStruct((B, L, D_CONV), BF16), f32s(D_ATTN), f32s(D), f32s(D),
                   jax.ShapeDtypeStruct((B, N_HEADS, LANES, L), BF16),
                   jax.ShapeDtypeStruct((B, L, LANES), BF16),
                   jax.ShapeDtypeStruct((B, L // KC, HEAD_DIM, KC), BF16),
                   jax.ShapeDtypeStruct((B, N_IDX_HEADS, IDX_DIM, L), BF16),
                   jax.ShapeDtypeStruct((B, L, IDX_DIM), BF16),
                   jax.ShapeDtypeStruct((B, N_IDX_HEADS, L), F32)],
        scratch_shapes=[pltpu.VMEM((HALO + tm1, D_CONV), F32),
                        pltpu.VMEM((SUBLANES - 1, HALO + tm1 - SUBLANES, D_CONV), F32),
                        pltpu.VMEM((HALO, D_CONV), F32)],
        compiler_params=cp(dimension_semantics=("arbitrary", "arbitrary")),
        name="proj",
    )(x, *consts1)

    yb = pl.pallas_call(
        functools.partial(_attn_kernel, n_keep=n_keep),
        grid=(B, L // TQ),
        in_specs=[pl.BlockSpec((1, N_IDX_HEADS, IDX_DIM, TQ), lambda b, j: (b, 0, 0, j)),
                  pl.BlockSpec((1, L, IDX_DIM), lambda b, j: (b, 0, 0)),
                  pl.BlockSpec((1, N_IDX_HEADS, TQ), lambda b, j: (b, 0, j)),
                  pl.BlockSpec((1, N_HEADS, LANES, TQ), lambda b, j: (b, 0, 0, j)),
                  pl.BlockSpec((1, L, LANES), lambda b, j: (b, 0, 0)),
                  pl.BlockSpec((1, L // KC, HEAD_DIM, KC), lambda b, j: (b, 0, 0, 0)),
                  pl.BlockSpec((1, TQ, D_ATTN), lambda b, j: (b, j, 0))],
        out_specs=pl.BlockSpec((1, TQ, D_ATTN), lambda b, j: (b, j, 0)),
        out_shape=jax.ShapeDtypeStruct((B, L, D_ATTN), BF16),
        scratch_shapes=[pltpu.VMEM((L, TQ), F32),
                        pltpu.VMEM((L, TQ), BF16),
                        pltpu.VMEM((2, SUBLANES, TQ), jnp.int32),
                        pltpu.VMEM((L, TQ), F32),
                        pltpu.VMEM((D_ATTN, TQ), F32),
                        pltpu.VMEM((N_HEADS, KC, TQ), F32),
                        pltpu.VMEM((N_HEADS, KC, TQ), F32)]
                       + [pltpu.VMEM((SUBLANES, TQ), F32)] * (3 * N_HEADS)
                       + [pltpu.VMEM((N_ACC_ROWS, TQ), F32)] * N_HEADS,
        compiler_params=cp(dimension_semantics=("arbitrary", "arbitrary")),
        name="attn",
    )(iqt, ik, iwt, qpt, kp, vt, saz)

    tm3 = TM_POST
    tok = lambda w: pl.BlockSpec((1, tm3, w), lambda b, i: (b, i, 0))
    consts3 = [w_a.astype(BF16), w_b.astype(BF16), w_o.astype(BF16), row(ple_g),
               w_pg.astype(BF16), w_pp.astype(BF16), row(out_g)]
    return pl.pallas_call(
        functools.partial(_post_kernel, tm=tm3, final_norm=final_norm),
        grid=(B, L // tm3),
        in_specs=[tok(D), tok(D_CONV), tok(D_ATTN), tok(D), tok(D), tok(D_PLE)]
                 + [const(a) for a in consts3],
        out_specs=tok(D),
        out_shape=jax.ShapeDtypeStruct((B, L, D), F32),
        compiler_params=cp(dimension_semantics=("arbitrary", "arbitrary")),
        name="post",
    )(x, ya, yb, ga, gb, p, *consts3)


def kernel(x, p, norm_g, w_in, conv_w, conv_b, conv_ln_g, conv_ln_b, w_a_out, w_b_out,
           w_o, ple_norm_g, w_ple_gate, w_ple_proj, final_g):
    depth = w_in.shape[0]
    for i in range(depth):
        x = _layer(x, p[i], norm_g[i], w_in[i], conv_w[i], conv_b[i], conv_ln_g[i],
                   conv_ln_b[i], w_a_out[i], w_b_out[i], w_o[i], ple_norm_g[i],
                   w_ple_gate[i], w_ple_proj[i], final_g, i == depth - 1)
    return x
```

```python
import functools
import math

import jax
import jax.numpy as jnp
import numpy as np
from jax import lax
from jax.experimental import pallas as pl
from jax.experimental.pallas import tpu as pltpu

D_MODEL = 1024
D_PLE = 256
D_CONV = 512
CONV_WIDTH = 31
N_HEADS = 8
HEAD_DIM = 64
D_ATTN = N_HEADS * HEAD_DIM
N_IDX_HEADS = 8
IDX_DIM = 64
TOPK_MAX = 256
EPS = 1e-6

LANES = 128
SUBLANES = 8
TQ = 256
KC = 256
TM_PROJ = 512
TM_POST = 512
HALO = 32
NEG = -1e30
FLT_MAX = float(np.finfo(np.float32).max)
FLT_TINY = float(np.finfo(np.float32).tiny)
BF16_ROWS = 16
WINDOW_BITS = 16
WINDOW = 2 ** WINDOW_BITS
LOG2E = math.log2(math.e)
N_ACC_ROWS = HEAD_DIM + 16
VMEM_LIMIT = 56 * 1024 * 1024

F32 = jnp.float32
BF16 = jnp.bfloat16

_C_VAL, _C_GATE, _C_Z, _C_AZ, _C_GA, _C_GB, _C_KV, _C_IKW, _C_END = (
    0, 512, 1024, 1536, 2048, 3072, 4096, 4224, 4352)
_R_Q, _R_IQ, _R_V, _R_END = 0, 512, 1024, 1088

_NT = (((1,), (1,)), ((), ()))


def _bf16_split3(a):
    a = np.float32(a)
    p1 = np.float32(a.astype(BF16))
    p2 = np.float32(np.float32(a - p1).astype(BF16))
    p3 = np.float32(a - p1 - p2)
    assert np.float32(np.float32(p3).astype(BF16)) == p3
    return float(p1), float(p2), float(p3)


def _sigmoid(x):
    return 1.0 / (1.0 + jnp.exp(-x))


def _silu(x):
    return x * _sigmoid(x)


def _rms(x, g):
    return x * lax.rsqrt(jnp.mean(x * x, axis=-1, keepdims=True) + EPS) * g


def _proj_kernel(x_ref, g_ref, w_ref, wt_ref, cw_ref, cb_ref, lng_ref, lnb_ref,
                 ya_ref, saz_ref, ga_ref, gb_ref,
                 qpt_ref, kp_ref, vt_ref, iqt_ref, ik_ref, iwt_ref,
                 ext_ref, es_ref, carry_ref, *, tm):
    i = pl.program_id(1)
    h = _rms(x_ref[0], g_ref[...]).astype(BF16)

    def mm(a, b):
        return jnp.dot(h, w_ref[:, a:b], preferred_element_type=F32)

    def mm_t(a, b):
        return lax.dot_general(wt_ref[a:b, :], h, _NT, preferred_element_type=F32)

    u = mm(_C_VAL, _C_GATE) * _sigmoid(mm(_C_GATE, _C_Z))
    prev = carry_ref[...]
    ext_ref[0:HALO, :] = jnp.where(i > 0, prev, jnp.zeros_like(prev))
    ext_ref[HALO:, :] = u
    carry_ref[...] = u[tm - HALO:, :]
    n_sh = HALO + tm - SUBLANES
    for r in range(1, SUBLANES):
        es_ref[r - 1] = ext_ref[r:r + n_sh, :]

    def w_conformer():
        base = HALO - (CONV_WIDTH - 1)
        c = jnp.zeros((tm, D_CONV), F32)
        for jj in range(CONV_WIDTH):
            row8, r = divmod(base + jj, SUBLANES)
            row = row8 * SUBLANES
            tap = ext_ref[row:row + tm, :] if r == 0 else es_ref[r - 1, row:row + tm, :]
            c = c + cw_ref[jj:jj + 1, :] * tap
        c = c + cb_ref[...]
        mu = jnp.mean(c, axis=-1, keepdims=True)
        cc = c - mu
        var = jnp.mean(cc * cc, axis=-1, keepdims=True)
        ln = cc * lax.rsqrt(var + EPS) * lng_ref[...] + lnb_ref[...]
        ya_ref[0] = (_silu(ln) * _silu(mm(_C_Z, _C_AZ))).astype(BF16)

    def w_saz():
        saz_ref[0] = _silu(mm(_C_AZ, _C_GA))

    def w_gate(ref, c0, half):
        def run():
            lo = half * (D_MODEL // 2)
            ref[0, :, lo:lo + D_MODEL // 2] = _sigmoid(mm(c0 + lo, c0 + lo + D_MODEL // 2))
        return run

    def w_keys():
        lane = lax.broadcasted_iota(jnp.int32, (tm, LANES), 1)
        pos = i * tm + lax.broadcasted_iota(jnp.int32, (tm, LANES), 0)
        kext = jnp.where(lane < 67, 1.0,
               jnp.where(lane < 70, (pos >> 6).astype(F32),
               jnp.where(lane < 73, (pos & 63).astype(F32), 0.0)))
        kp_ref[0] = jnp.where(lane < 64, mm(_C_KV, _C_IKW), kext).astype(BF16)
        ikw = mm(_C_IKW, _C_END)
        ik_ref[0] = ikw[:, :IDX_DIM].astype(BF16)
        iwt_ref[0] = ikw.T[IDX_DIM:IDX_DIM + N_IDX_HEADS, :] * (N_IDX_HEADS ** -0.5)

    def w_queries():
        sub = lax.broadcasted_iota(jnp.int32, (16, tm), 0)
        tpos = (i * tm + lax.broadcasted_iota(jnp.int32, (16, tm), 1)).astype(F32)
        qt = mm_t(_R_Q, _R_IQ) * (LOG2E * HEAD_DIM ** -0.5)
        for hd in range(N_HEADS):
            a = np.float32(2.0 ** (-8.0 * (hd + 1) / N_HEADS) * LOG2E)
            a1, a2, a3 = _bf16_split3(a)
            v = tpos * float(-a)
            v1 = v.astype(BF16).astype(F32)
            v2 = (v - v1).astype(BF16).astype(F32)
            v3 = v - v1 - v2
            ext = jnp.where(sub == 0, v1, jnp.where(sub == 1, v2, jnp.where(sub == 2, v3,
                  jnp.where(sub == 3, 64.0 * a1, jnp.where(sub == 4, 64.0 * a2,
                  jnp.where(sub == 5, 64.0 * a3, jnp.where(sub == 6, a1,
                  jnp.where(sub == 7, a2, jnp.where(sub == 8, a3, 0.0)))))))))
            qpt_ref[0, hd, 0:HEAD_DIM, :] = qt[hd * HEAD_DIM:(hd + 1) * HEAD_DIM, :].astype(BF16)
            qpt_ref[0, hd, HEAD_DIM:HEAD_DIM + 16, :] = ext.astype(BF16)
            qpt_ref[0, hd, HEAD_DIM + 16:, :] = jnp.zeros((LANES - HEAD_DIM - 16, tm), BF16)

    def w_index_values():
        iqt = mm_t(_R_IQ, _R_V) * (IDX_DIM ** -0.5)
        for hd in range(N_IDX_HEADS):
            iqt_ref[0, hd] = iqt[hd * IDX_DIM:(hd + 1) * IDX_DIM, :].astype(BF16)
        vt = mm_t(_R_V, _R_END).astype(BF16)
        for c in range(tm // KC):
            vt_ref[0, c] = vt[:, c * KC:(c + 1) * KC]

    for work in (w_conformer, w_saz, w_gate(ga_ref, _C_GA, 0), w_gate(ga_ref, _C_GA, 1),
                 w_gate(gb_ref, _C_GB, 0), w_gate(gb_ref, _C_GB, 1), w_keys, w_queries,
                 w_index_values):
        work()


def _threshold_of_key(k, bits, n_subnormal):
    f = lax.bitcast_convert_type(bits, F32)
    pos = jnp.where(k < n_subnormal, jnp.where(k == 0, 0.0, FLT_TINY), f)
    neg = jnp.where(k >= -n_subnormal, 0.0, jnp.where(f != f, -jnp.inf, f))
    return jnp.where(k >= 0, pos, neg)


def _key_to_float(k):
    return _threshold_of_key(k, jnp.where(k >= 0, k, k ^ jnp.int32(0x7FFFFFFF)), 2 ** 23)


def _key16_to_float(k):
    return _threshold_of_key(k, jnp.left_shift(jnp.where(k >= 0, k, k ^ jnp.int32(0x7FFF)), 16), 2 ** 7)


def _attn_kernel(iqt_ref, ik_ref, iwt_ref, qpt_ref, kp_ref, vt_ref, saz_ref, yb_ref,
                 st_ref, rt_ref, key_ref, bias_ref, ot_ref, lga_ref, lgb_ref, *head_refs, n_keep):
    m_refs, mxa_refs, mxb_refs, acc_refs = (head_refs[g * N_HEADS:(g + 1) * N_HEADS] for g in range(4))
    j = pl.program_id(1)
    t0 = j * TQ
    nkc = j + 1

    tcol = t0 + lax.broadcasted_iota(jnp.int32, (KC, TQ), 1)
    srow0 = lax.broadcasted_iota(jnp.int32, (KC, TQ), 0)

    def score_chunk(c):
        ks = pl.multiple_of(c * KC, KC)
        kic = ik_ref[0, pl.ds(ks, KC), :]
        acc = jnp.zeros((KC, TQ), F32)
        for hd in range(N_IDX_HEADS):
            d = jnp.dot(kic, iqt_ref[0, hd], preferred_element_type=F32)
            acc = acc + iwt_ref[0, hd:hd + 1, :] * jnp.maximum(d, 0.0)
        masked = jnp.where((ks + srow0) <= tcol, acc, -jnp.inf)
        st_ref[pl.ds(ks, KC), :] = masked
        rt_ref[pl.ds(ks, KC), :] = masked.astype(BF16)

    def score_pair(i, carry):
        score_chunk(2 * i)
        score_chunk(2 * i + 1)
        return carry

    lax.fori_loop(0, nkc // 2, score_pair, 0)

    @pl.when(nkc % 2 == 1)
    def _():
        score_chunk(nkc - 1)

    n_part = 4

    def count_ge(thr8):
        def body(c, cnt):
            ks = pl.multiple_of(c * KC, KC)
            s = st_ref[pl.ds(ks, KC), :].reshape(KC // (SUBLANES * n_part), n_part, SUBLANES, TQ)
            return cnt + jnp.sum((s >= thr8[None, None]).astype(jnp.int32), axis=0)
        cnt = lax.fori_loop(0, nkc, body, jnp.zeros((n_part, SUBLANES, TQ), jnp.int32))
        cnt8 = jnp.sum(cnt, axis=0)
        return jnp.broadcast_to(jnp.sum(cnt8, axis=0, keepdims=True), (SUBLANES, TQ))

    def count_ge16(thr_bf):
        def chunk(c):
            ks = pl.multiple_of(c * KC, KC)
            r = rt_ref[pl.ds(ks, KC), :].reshape(KC // (BF16_ROWS * 2), 2, BF16_ROWS, TQ)
            w = jnp.where(r >= thr_bf[None, None], jnp.ones((), BF16), jnp.zeros((), BF16))
            part = w[0]
            for g in range(1, w.shape[0]):
                part = part + w[g]
            return part
        cnt = lax.fori_loop(0, nkc // 2, lambda i, a: a + chunk(2 * i) + chunk(2 * i + 1),
                            jnp.zeros((2, BF16_ROWS, TQ), BF16))
        cnt = lax.cond(nkc % 2 == 1, lambda a: a + chunk(nkc - 1), lambda a: a, cnt).astype(F32)
        tot = jnp.sum(cnt[0] + cnt[1], axis=0, keepdims=True)
        return jnp.broadcast_to(tot, (SUBLANES, TQ))

    def thr16(k16):
        f = _key16_to_float(k16)
        return jnp.concatenate([f] * (BF16_ROWS // SUBLANES), axis=0).astype(BF16)

    k16 = jnp.where(count_ge16(thr16(jnp.zeros((SUBLANES, TQ), jnp.int32))) >= n_keep,
                    jnp.int32(0), jnp.int32(-(2 ** 15)))

    def bit16_body(it, k):
        cand = k | jnp.left_shift(jnp.int32(1), 14 - it)
        return jnp.where(count_ge16(thr16(cand)) >= n_keep, cand, k)

    k16 = lax.fori_loop(0, 15, bit16_body, k16)
    hb_bits = lax.bitcast_convert_type(_key16_to_float(k16), jnp.int32)
    k_hb = hb_bits ^ ((hb_bits >> 31) & jnp.int32(0x7FFFFFFF))
    k_base = k_hb - WINDOW // 2
    k_base_wide = k_hb - WINDOW
    w_lo = jnp.maximum(_key_to_float(k_base), -FLT_MAX)
    w_hi = _key_to_float(k_base + WINDOW + 1)

    def gather_body(c, carry):
        m1, m2, m3, ovf, above = carry
        ks = pl.multiple_of(c * KC, KC)
        s = st_ref[pl.ds(ks, KC), :].reshape(KC // (SUBLANES * n_part), n_part, SUBLANES, TQ)
        for g in range(s.shape[0]):
            sg = s[g]
            ge_hi = sg >= w_hi[None]
            above = above + ge_hi.astype(jnp.int32)
            v = jnp.where(ge_hi, -jnp.inf, sg)
            t1 = jnp.minimum(m1, v)
            m1 = jnp.maximum(m1, v)
            t2 = jnp.minimum(m2, t1)
            m2 = jnp.maximum(m2, t1)
            ovf = jnp.maximum(ovf, jnp.minimum(m3, t2))
            m3 = jnp.maximum(m3, t2)
        return m1, m2, m3, ovf, above

    ninf = jnp.full((n_part, SUBLANES, TQ), -jnp.inf, F32)
    m1, m2, m3, ovf, above = lax.fori_loop(
        0, nkc, gather_body, (ninf, ninf, ninf, ninf, jnp.zeros((n_part, SUBLANES, TQ), jnp.int32)))
    gathered = jnp.concatenate([m1, m2, m3], axis=0)
    above8 = jnp.sum(above, axis=0)
    above8 = jnp.broadcast_to(jnp.sum(above8, axis=0, keepdims=True), (SUBLANES, TQ))

    def count_ge_gathered(thr8):
        c8 = jnp.sum((gathered >= thr8[None]).astype(jnp.int32), axis=0)
        return above8 + jnp.broadcast_to(jnp.sum(c8, axis=0, keepdims=True), (SUBLANES, TQ))

    def refine(count_fn, base):
        cnt0 = count_fn(_key_to_float(base))
        def body(it, carry):
            off, cnt_lo = carry
            cand = off | jnp.left_shift(jnp.int32(1), WINDOW_BITS - it)
            cnt = count_fn(_key_to_float(base + cand))
            ok = cnt >= n_keep
            return jnp.where(ok, cand, off), jnp.where(ok, cnt, cnt_lo)
        off, cnt_lo = lax.fori_loop(0, WINDOW_BITS + 1, body,
                                    (jnp.zeros((SUBLANES, TQ), jnp.int32), cnt0))
        key_ref[0] = base + off
        key_ref[1] = cnt_lo

    refine(count_ge_gathered, k_base)

    dropped = jnp.max(jnp.where(ovf >= w_lo[None], 1, 0))
    in_window = count_ge_gathered(w_lo)
    outside = jnp.where(in_window < n_keep, 1, jnp.where(above8 >= n_keep, 1, 0))
    outside = jnp.where(w_lo > -FLT_MAX, outside, 0)
    gathered_ok = (dropped == 0) & (jnp.max(outside) == 0)

    @pl.when(jnp.logical_not(gathered_ok))
    def _():
        refine(count_ge, k_base_wide)

    key = key_ref[0]
    cnt_lo = key_ref[1]
    lo8 = _key_to_float(key)
    hi8 = _key_to_float(key + 1)
    short = lo8 == -jnp.inf
    lo8 = jnp.where(short, -FLT_MAX, lo8)
    lo = lo8[0:1]

    has_tie = jnp.max(jnp.where(short, n_keep, cnt_lo)) > n_keep

    @pl.when(jnp.logical_not(has_tie))
    def _():
        def body(c, carry):
            ks = pl.multiple_of(c * KC, KC)
            bias_ref[pl.ds(ks, KC), :] = jnp.where(st_ref[pl.ds(ks, KC), :] >= lo, 0.0, NEG)
            return carry
        lax.fori_loop(0, nkc, body, 0)

    @pl.when(has_tie)
    def _():
        hi = hi8[0:1]
        key_ref[1] = count_ge_gathered(hi8)

        @pl.when(jnp.logical_not(gathered_ok))
        def _():
            key_ref[1] = count_ge(hi8)

        need = (n_keep - key_ref[1]).astype(F32)[0:1]
        tri = (lax.broadcasted_iota(jnp.int32, (KC, KC), 0)
               >= lax.broadcasted_iota(jnp.int32, (KC, KC), 1)).astype(BF16)

        def body(c, taken):
            ks = pl.multiple_of(c * KC, KC)
            s = st_ref[pl.ds(ks, KC), :]
            ge_lo = s >= lo
            ge_hi = s >= hi
            tied = jnp.where(ge_lo, jnp.where(ge_hi, 0.0, 1.0), 0.0).astype(BF16)
            rank = taken + jnp.dot(tri, tied, preferred_element_type=F32)
            bias_ref[pl.ds(ks, KC), :] = jnp.where(
                ge_lo, jnp.where(ge_hi, 0.0, jnp.where(rank <= need, 0.0, NEG)), NEG)
            return rank[KC - 1:KC, :]

        lax.fori_loop(0, nkc, body, jnp.zeros((1, TQ), F32))

    for acc_ref in acc_refs:
        acc_ref[...] = jnp.zeros(acc_ref.shape, F32)
    ones_rows = (lax.broadcasted_iota(jnp.int32, (N_ACC_ROWS - HEAD_DIM, KC), 0) == 0).astype(BF16)

    for m_ref in m_refs:
        m_ref[...] = jnp.full(m_ref.shape, NEG, F32)
    buf_a = (lga_ref, mxa_refs)
    buf_b = (lgb_ref, mxb_refs)

    def heads_step(c_qk, qk_buf, c_pv, pv_buf):
        if c_qk is not None:
            ks = pl.multiple_of(c_qk * KC, KC)
            kpc = kp_ref[0, pl.ds(ks, KC), :]
            bias = bias_ref[pl.ds(ks, KC), :]
        if c_pv is not None:
            vtc = jnp.concatenate([vt_ref[0, c_pv], ones_rows], axis=0)
        for hd in range(N_HEADS):
            if c_qk is not None:
                lg = jnp.dot(kpc, qpt_ref[0, hd], preferred_element_type=F32) + bias
                qk_buf[0][hd] = lg
                qk_buf[1][hd][0:1, :] = jnp.max(lg, axis=0, keepdims=True)
            if c_pv is not None:
                m_old = m_refs[hd][0:1, :]
                m_new = jnp.maximum(m_old, pv_buf[1][hd][0:1, :])
                alpha = jnp.exp2(m_old - m_new)
                p = jnp.exp2(pv_buf[0][hd] - m_new).astype(BF16)
                acc_refs[hd][...] = alpha * acc_refs[hd][...] + jnp.dot(
                    vtc, p, preferred_element_type=F32)
                m_refs[hd][0:1, :] = m_new

    n_steps = nkc - 1
    heads_step(0, buf_a, None, None)

    def pair_body(i, carry):
        heads_step(2 * i + 1, buf_b, 2 * i, buf_a)
        heads_step(2 * i + 2, buf_a, 2 * i + 1, buf_b)
        return carry

    lax.fori_loop(0, n_steps // 2, pair_body, 0)

    @pl.when(n_steps % 2 == 1)
    def _():
        heads_step(n_steps, buf_b, n_steps - 1, buf_a)
        heads_step(None, None, n_steps, buf_b)

    @pl.when(n_steps % 2 == 0)
    def _():
        heads_step(None, None, n_steps, buf_a)

    for hd in range(N_HEADS):
        acc = acc_refs[hd][...]
        ot_ref[hd * HEAD_DIM:(hd + 1) * HEAD_DIM, :] = acc[0:HEAD_DIM] / acc[HEAD_DIM:HEAD_DIM + 1]
    yb_ref[0] = (ot_ref[...].T * saz_ref[0]).astype(BF16)


def _post_kernel(x_ref, ya_ref, yb_ref, ga_ref, gb_ref, p_ref,
                 wa_ref, wb_ref, wo_ref, pg_ref, wpg_ref, wpp_ref, fg_ref, out_ref, *, final_norm):
    merged = (ga_ref[0] * jnp.dot(ya_ref[0], wa_ref[...], preferred_element_type=F32)
              + gb_ref[0] * jnp.dot(yb_ref[0], wb_ref[...], preferred_element_type=F32))
    x1 = x_ref[0] + jnp.dot(merged.astype(BF16), wo_ref[...], preferred_element_type=F32)
    gate = _sigmoid(jnp.dot(_rms(x1, pg_ref[...]).astype(BF16), wpg_ref[...],
                            preferred_element_type=F32))
    x2 = x1 + gate * jnp.dot(p_ref[0].astype(BF16), wpp_ref[...], preferred_element_type=F32)
    out_ref[0] = _rms(x2, fg_ref[...]) if final_norm else x2


def _pack_w_in(w):
    o_in, o_z, o_q, o_k, o_v, o_az, o_iq, o_ik, o_iw, o_ga, o_gb = (
        0, 1024, 1536, 2048, 2112, 2176, 2688, 3200, 3264, 3272, 4296)
    pad = jnp.zeros((D_MODEL, LANES - IDX_DIM - N_IDX_HEADS), w.dtype)
    cols = [w[:, o_in:o_z], w[:, o_z:o_q], w[:, o_az:o_iq],
            w[:, o_ga:o_gb], w[:, o_gb:o_gb + D_MODEL], w[:, o_k:o_az],
            w[:, o_ik:o_ga], pad]
    w_tok = jnp.concatenate(cols, axis=1).astype(BF16)
    rows = [w[:, o_q:o_k], w[:, o_iq:o_ik], w[:, o_v:o_az]]
    w_feat = jnp.concatenate(rows, axis=1).T.astype(BF16)
    return w_tok, w_feat


def _layer(x, p, norm_g, w_in, conv_w, conv_b, ln_g, ln_b, w_a, w_b, w_o,
           ple_g, w_pg, w_pp, out_g, final_norm):
    B, L, D = x.shape
    n_keep = min(TOPK_MAX, L // 4)
    cp = functools.partial(pltpu.CompilerParams, vmem_limit_bytes=VMEM_LIMIT)
    row = lambda a: a.reshape(1, -1)

    tm1 = TM_PROJ
    w_tok, w_feat = _pack_w_in(w_in)
    tok = lambda w: pl.BlockSpec((1, tm1, w), lambda b, i: (b, i, 0))
    featm = lambda r: pl.BlockSpec((1, N_HEADS, r, tm1), lambda b, i: (b, 0, 0, i))
    const = lambda a: pl.BlockSpec(a.shape, lambda b, i: (0,) * a.ndim, pipeline_mode=pl.Buffered(1))
    f32s = lambda w: jax.ShapeDtypeStruct((B, L, w), F32)
    consts1 = [row(norm_g), w_tok, w_feat, conv_w, row(conv_b), row(ln_g), row(ln_b)]
    ya, saz, ga, gb, qpt, kp, vt, iqt, ik, iwt = pl.pallas_call(
        functools.partial(_proj_kernel, tm=tm1),
        grid=(B, L // tm1),
        in_specs=[tok(D)] + [const(a) for a in consts1],
        out_specs=[tok(D_CONV), tok(D_ATTN), tok(D), tok(D),
                   featm(LANES), tok(LANES),
                   pl.BlockSpec((1, tm1 // KC, HEAD_DIM, KC), lambda b, i: (b, i, 0, 0)),
                   featm(IDX_DIM), tok(IDX_DIM),
                   pl.BlockSpec((1, N_IDX_HEADS, tm1), lambda b, i: (b, 0, i))],
        out_shape=[jax.ShapeDtypeStruct((B, L, D_CONV), BF16), f32s(D_ATTN), f32s(D), f32s(D),
                   jax.ShapeDtypeStruct((B, N_HEADS, LANES, L), BF16),
                   jax.ShapeDtypeStruct((B, L, LANES), BF16),
                   jax.ShapeDtypeStruct((B, L // KC, HEAD_DIM, KC), BF16),
                   jax.ShapeDtypeStruct((B, N_IDX_HEADS, IDX_DIM, L), BF16),
                   jax.ShapeDtypeStruct((B, L, IDX_DIM), BF16),
                   jax.ShapeDtypeStruct((B, N_IDX_HEADS, L), F32)],
        scratch_shapes=[pltpu.VMEM((HALO + tm1, D_CONV), F32),
                        pltpu.VMEM((SUBLANES - 1, HALO + tm1 - SUBLANES, D_CONV), F32),
                        pltpu.VMEM((HALO, D_CONV), F32)],
        compiler_params=cp(dimension_semantics=("arbitrary", "arbitrary")),
        name="proj",
    )(x, *consts1)

    yb = pl.pallas_call(
        functools.partial(_attn_kernel, n_keep=n_keep),
        grid=(B, L // TQ),
        in_specs=[pl.BlockSpec((1, N_IDX_HEADS, IDX_DIM, TQ), lambda b, j: (b, 0, 0, j)),
                  pl.BlockSpec((1, L, IDX_DIM), lambda b, j: (b, 0, 0)),
                  pl.BlockSpec((1, N_IDX_HEADS, TQ), lambda b, j: (b, 0, j)),
                  pl.BlockSpec((1, N_HEADS, LANES, TQ), lambda b, j: (b, 0, 0, j)),
                  pl.BlockSpec((1, L, LANES), lambda b, j: (b, 0, 0)),
                  pl.BlockSpec((1, L // KC, HEAD_DIM, KC), lambda b, j: (b, 0, 0, 0)),
                  pl.BlockSpec((1, TQ, D_ATTN), lambda b, j: (b, j, 0))],
        out_specs=pl.BlockSpec((1, TQ, D_ATTN), lambda b, j: (b, j, 0)),
        out_shape=jax.ShapeDtypeStruct((B, L, D_ATTN), BF16),
        scratch_shapes=[pltpu.VMEM((L, TQ), F32),
                        pltpu.VMEM((L, TQ), BF16),
                        pltpu.VMEM((2, SUBLANES, TQ), jnp.int32),
                        pltpu.VMEM((L, TQ), F32),
                        pltpu.VMEM((D_ATTN, TQ), F32),
                        pltpu.VMEM((N_HEADS, KC, TQ), F32),
                        pltpu.VMEM((N_HEADS, KC, TQ), F32)]
                       + [pltpu.VMEM((SUBLANES, TQ), F32)] * (3 * N_HEADS)
                       + [pltpu.VMEM((N_ACC_ROWS, TQ), F32)] * N_HEADS,
        compiler_params=cp(dimension_semantics=("arbitrary", "arbitrary")),
        name="attn",
    )(iqt, ik, iwt, qpt, kp, vt, saz)

    tm3 = TM_POST
    tok = lambda w: pl.BlockSpec((1, tm3, w), lambda b, i: (b, i, 0))
    consts3 = [w_a.astype(BF16), w_b.astype(BF16), w_o.astype(BF16), row(ple_g),
               w_pg.astype(BF16), w_pp.astype(BF16), row(out_g)]
    return pl.pallas_call(
        functools.partial(_post_kernel, final_norm=final_norm),
        grid=(B, L // tm3),
        in_specs=[tok(D), tok(D_CONV), tok(D_ATTN), tok(D), tok(D), tok(D_PLE)]
                 + [const(a) for a in consts3],
        out_specs=tok(D),
        out_shape=jax.ShapeDtypeStruct((B, L, D), F32),
        compiler_params=cp(dimension_semantics=("arbitrary", "arbitrary")),
        name="post",
    )(x, ya, yb, ga, gb, p, *consts3)


def kernel(x, p, norm_g, w_in, conv_w, conv_b, conv_ln_g, conv_ln_b, w_a_out, w_b_out,
           w_o, ple_norm_g, w_ple_gate, w_ple_proj, final_g):
    depth = w_in.shape[0]
    for i in range(depth):
        x = _layer(x, p[i], norm_g[i], w_in[i], conv_w[i], conv_b[i], conv_ln_g[i],
                   conv_ln_b[i], w_a_out[i], w_b_out[i], w_o[i], ple_norm_g[i],
                   w_ple_gate[i], w_ple_proj[i], final_g, i == depth - 1)
    return x
```

```python
import functools
import math

import jax
import jax.numpy as jnp
import numpy as np
from jax import lax
from jax.experimental import pallas as pl
from jax.experimental.pallas import tpu as pltpu

D_MODEL = 1024
D_PLE = 256
D_CONV = 512
CONV_WIDTH = 31
N_HEADS = 8
HEAD_DIM = 64
D_ATTN = N_HEADS * HEAD_DIM
N_IDX_HEADS = 8
IDX_DIM = 64
TOPK_MAX = 256
EPS = 1e-6

LANES = 128
SUBLANES = 8
TQ = 256
KC = 256
TM_PROJ = 512
TM_POST = 512
HALO = 32
NEG = -1e30
FLT_MAX = float(np.finfo(np.float32).max)
FLT_TINY = float(np.finfo(np.float32).tiny)
BF16_ROWS = 16
WINDOW_BITS = 16
WINDOW = 2 ** WINDOW_BITS
LOG2E = math.log2(math.e)
N_ACC_ROWS = HEAD_DIM + 16
VMEM_LIMIT = 56 * 1024 * 1024

F32 = jnp.float32
BF16 = jnp.bfloat16

_C_VAL, _C_GATE, _C_Z, _C_AZ, _C_GA, _C_GB, _C_KV, _C_IKW, _C_END = (
    0, 512, 1024, 1536, 2048, 3072, 4096, 4224, 4352)
_R_Q, _R_IQ, _R_V, _R_END = 0, 512, 1024, 1088

_NT = (((1,), (1,)), ((), ()))


def _bf16_split3(a):
    a = np.float32(a)
    p1 = np.float32(a.astype(BF16))
    p2 = np.float32(np.float32(a - p1).astype(BF16))
    p3 = np.float32(a - p1 - p2)
    assert np.float32(np.float32(p3).astype(BF16)) == p3
    return float(p1), float(p2), float(p3)


def _sigmoid(x):
    return 1.0 / (1.0 + jnp.exp(-x))


def _silu(x):
    return x * _sigmoid(x)


def _rms(x, g):
    return x * lax.rsqrt(jnp.mean(x * x, axis=-1, keepdims=True) + EPS) * g


def _proj_kernel(x_ref, g_ref, w_ref, wt_ref, cw_ref, cb_ref, lng_ref, lnb_ref,
                 ya_ref, saz_ref, ga_ref, gb_ref,
                 qpt_ref, kp_ref, vt_ref, iqt_ref, ik_ref, iwt_ref,
                 ext_ref, es_ref, carry_ref, *, tm):
    i = pl.program_id(1)
    h = _rms(x_ref[0], g_ref[...]).astype(BF16)

    def mm(a, b):
        return jnp.dot(h, w_ref[:, a:b], preferred_element_type=F32)

    def mm_t(a, b):
        return lax.dot_general(wt_ref[a:b, :], h, _NT, preferred_element_type=F32)

    u = mm(_C_VAL, _C_GATE) * _sigmoid(mm(_C_GATE, _C_Z))
    prev = carry_ref[...]
    ext_ref[0:HALO, :] = jnp.where(i > 0, prev, jnp.zeros_like(prev))
    ext_ref[HALO:, :] = u
    carry_ref[...] = u[tm - HALO:, :]
    n_sh = HALO + tm - SUBLANES
    for r in range(1, SUBLANES):
        es_ref[r - 1] = ext_ref[r:r + n_sh, :]

    def w_conformer():
        base = HALO - (CONV_WIDTH - 1)
        c = jnp.zeros((tm, D_CONV), F32)
        for jj in range(CONV_WIDTH):
            row8, r = divmod(base + jj, SUBLANES)
            row = row8 * SUBLANES
            tap = ext_ref[row:row + tm, :] if r == 0 else es_ref[r - 1, row:row + tm, :]
            c = c + cw_ref[jj:jj + 1, :] * tap
        c = c + cb_ref[...]
        mu = jnp.mean(c, axis=-1, keepdims=True)
        cc = c - mu
        var = jnp.mean(cc * cc, axis=-1, keepdims=True)
        ln = cc * lax.rsqrt(var + EPS) * lng_ref[...] + lnb_ref[...]
        ya_ref[0] = (_silu(ln) * _silu(mm(_C_Z, _C_AZ))).astype(BF16)

    def w_saz():
        saz_ref[0] = _silu(mm(_C_AZ, _C_GA))

    def w_gate(ref, c0, half):
        def run():
            lo = half * (D_MODEL // 2)
            ref[0, :, lo:lo + D_MODEL // 2] = _sigmoid(mm(c0 + lo, c0 + lo + D_MODEL // 2))
        return run

    def w_keys():
        lane = lax.broadcasted_iota(jnp.int32, (tm, LANES), 1)
        pos = i * tm + lax.broadcasted_iota(jnp.int32, (tm, LANES), 0)
        kext = jnp.where(lane < 67, 1.0,
               jnp.where(lane < 70, (pos >> 6).astype(F32),
               jnp.where(lane < 73, (pos & 63).astype(F32), 0.0)))
        kp_ref[0] = jnp.where(lane < 64, mm(_C_KV, _C_IKW), kext).astype(BF16)
        ikw = mm(_C_IKW, _C_END)
        ik_ref[0] = ikw[:, :IDX_DIM].astype(BF16)
        iwt_ref[0] = ikw.T[IDX_DIM:IDX_DIM + N_IDX_HEADS, :] * (N_IDX_HEADS ** -0.5)

    def w_queries():
        sub = lax.broadcasted_iota(jnp.int32, (16, tm), 0)
        tpos = (i * tm + lax.broadcasted_iota(jnp.int32, (16, tm), 1)).astype(F32)
        qt = mm_t(_R_Q, _R_IQ) * (LOG2E * HEAD_DIM ** -0.5)
        for hd in range(N_HEADS):
            a = np.float32(2.0 ** (-8.0 * (hd + 1) / N_HEADS) * LOG2E)
            a1, a2, a3 = _bf16_split3(a)
            v = tpos * float(-a)
            v1 = v.astype(BF16).astype(F32)
            v2 = (v - v1).astype(BF16).astype(F32)
            v3 = v - v1 - v2
            ext = jnp.where(sub == 0, v1, jnp.where(sub == 1, v2, jnp.where(sub == 2, v3,
                  jnp.where(sub == 3, 64.0 * a1, jnp.where(sub == 4, 64.0 * a2,
                  jnp.where(sub == 5, 64.0 * a3, jnp.where(sub == 6, a1,
                  jnp.where(sub == 7, a2, jnp.where(sub == 8, a3, 0.0)))))))))
            qpt_ref[0, hd, 0:HEAD_DIM, :] = qt[hd * HEAD_DIM:(hd + 1) * HEAD_DIM, :].astype(BF16)
            qpt_ref[0, hd, HEAD_DIM:HEAD_DIM + 16, :] = ext.astype(BF16)
            qpt_ref[0, hd, HEAD_DIM + 16:, :] = jnp.zeros((LANES - HEAD_DIM - 16, tm), BF16)

    def w_index_values():
        iqt = mm_t(_R_IQ, _R_V) * (IDX_DIM ** -0.5)
        for hd in range(N_IDX_HEADS):
            iqt_ref[0, hd] = iqt[hd * IDX_DIM:(hd + 1) * IDX_DIM, :].astype(BF16)
        vt = mm_t(_R_V, _R_END).astype(BF16)
        for c in range(tm // KC):
            vt_ref[0, c] = vt[:, c * KC:(c + 1) * KC]

    for work in (w_conformer, w_saz, w_gate(ga_ref, _C_GA, 0), w_gate(ga_ref, _C_GA, 1),
                 w_gate(gb_ref, _C_GB, 0), w_gate(gb_ref, _C_GB, 1), w_keys, w_queries,
                 w_index_values):
        work()


def _threshold_of_key(k, bits, n_subnormal):
    f = lax.bitcast_convert_type(bits, F32)
    pos = jnp.where(k < n_subnormal, jnp.where(k == 0, 0.0, FLT_TINY), f)
    neg = jnp.where(k >= -n_subnormal, 0.0, jnp.where(f != f, -jnp.inf, f))
    return jnp.where(k >= 0, pos, neg)


def _key_to_float(k):
    return _threshold_of_key(k, jnp.where(k >= 0, k, k ^ jnp.int32(0x7FFFFFFF)), 2 ** 23)


def _key16_to_float(k):
    return _threshold_of_key(k, jnp.left_shift(jnp.where(k >= 0, k, k ^ jnp.int32(0x7FFF)), 16), 2 ** 7)


def _attn_kernel(iqt_ref, ik_ref, iwt_ref, qpt_ref, kp_ref, vt_ref, saz_ref, yb_ref,
                 st_ref, rt_ref, key_ref, bias_ref, ot_ref, lga_ref, lgb_ref, *head_refs, n_keep):
    m_refs, mxa_refs, mxb_refs, acc_refs = (head_refs[g * N_HEADS:(g + 1) * N_HEADS] for g in range(4))
    j = pl.program_id(1)
    t0 = j * TQ
    nkc = j + 1

    tcol = t0 + lax.broadcasted_iota(jnp.int32, (KC, TQ), 1)
    srow0 = lax.broadcasted_iota(jnp.int32, (KC, TQ), 0)

    def score_chunk(c):
        ks = pl.multiple_of(c * KC, KC)
        kic = ik_ref[0, pl.ds(ks, KC), :]
        acc = jnp.zeros((KC, TQ), F32)
        for hd in range(N_IDX_HEADS):
            d = jnp.dot(kic, iqt_ref[0, hd], preferred_element_type=F32)
            acc = acc + iwt_ref[0, hd:hd + 1, :] * jnp.maximum(d, 0.0)
        masked = jnp.where((ks + srow0) <= tcol, acc, -jnp.inf)
        st_ref[pl.ds(ks, KC), :] = masked
        rt_ref[pl.ds(ks, KC), :] = masked.astype(BF16)

    def score_pair(i, carry):
        score_chunk(2 * i)
        score_chunk(2 * i + 1)
        return carry

    lax.fori_loop(0, nkc // 2, score_pair, 0)

    @pl.when(nkc % 2 == 1)
    def _():
        score_chunk(nkc - 1)

    n_part = 4

    def count_ge(thr8):
        def body(c, cnt):
            ks = pl.multiple_of(c * KC, KC)
            s = st_ref[pl.ds(ks, KC), :].reshape(KC // (SUBLANES * n_part), n_part, SUBLANES, TQ)
            return cnt + jnp.sum((s >= thr8[None, None]).astype(jnp.int32), axis=0)
        cnt = lax.fori_loop(0, nkc, body, jnp.zeros((n_part, SUBLANES, TQ), jnp.int32))
        cnt8 = jnp.sum(cnt, axis=0)
        return jnp.broadcast_to(jnp.sum(cnt8, axis=0, keepdims=True), (SUBLANES, TQ))

    def count_ge16(thr_bf):
        def chunk(c):
            ks = pl.multiple_of(c * KC, KC)
            r = rt_ref[pl.ds(ks, KC), :].reshape(KC // (BF16_ROWS * 2), 2, BF16_ROWS, TQ)
            w = jnp.where(r >= thr_bf[None, None], jnp.ones((), BF16), jnp.zeros((), BF16))
            part = w[0]
            for g in range(1, w.shape[0]):
                part = part + w[g]
            return part
        cnt = lax.fori_loop(0, nkc // 2, lambda i, a: a + chunk(2 * i) + chunk(2 * i + 1),
                            jnp.zeros((2, BF16_ROWS, TQ), BF16))
        cnt = lax.cond(nkc % 2 == 1, lambda a: a + chunk(nkc - 1), lambda a: a, cnt).astype(F32)
        tot = jnp.sum(cnt[0] + cnt[1], axis=0, keepdims=True)
        return jnp.broadcast_to(tot, (SUBLANES, TQ))

    def thr16(k16):
        f = _key16_to_float(k16)
        return jnp.concatenate([f] * (BF16_ROWS // SUBLANES), axis=0).astype(BF16)

    k16 = jnp.where(count_ge16(thr16(jnp.zeros((SUBLANES, TQ), jnp.int32))) >= n_keep,
                    jnp.int32(0), jnp.int32(-(2 ** 15)))

    def bit16_body(it, k):
        cand = k | jnp.left_shift(jnp.int32(1), 14 - it)
        return jnp.where(count_ge16(thr16(cand)) >= n_keep, cand, k)

    k16 = lax.fori_loop(0, 15, bit16_body, k16)
    hb_bits = lax.bitcast_convert_type(_key16_to_float(k16), jnp.int32)
    k_hb = hb_bits ^ ((hb_bits >> 31) & jnp.int32(0x7FFFFFFF))
    k_base = k_hb - WINDOW // 2
    k_base_wide = k_hb - WINDOW
    w_lo = jnp.maximum(_key_to_float(k_base), -FLT_MAX)
    w_hi = _key_to_float(k_base + WINDOW + 1)

    def gather_body(c, carry):
        m1, m2, m3, ovf, above = carry
        ks = pl.multiple_of(c * KC, KC)
        s = st_ref[pl.ds(ks, KC), :].reshape(KC // (SUBLANES * n_part), n_part, SUBLANES, TQ)
        for g in range(s.shape[0]):
            sg = s[g]
            ge_hi = sg >= w_hi[None]
            above = above + ge_hi.astype(jnp.int32)
            v = jnp.where(ge_hi, -jnp.inf, sg)
            t1 = jnp.minimum(m1, v)
            m1 = jnp.maximum(m1, v)
            t2 = jnp.minimum(m2, t1)
            m2 = jnp.maximum(m2, t1)
            ovf = jnp.maximum(ovf, jnp.minimum(m3, t2))
            m3 = jnp.maximum(m3, t2)
        return m1, m2, m3, ovf, above

    ninf = jnp.full((n_part, SUBLANES, TQ), -jnp.inf, F32)
    m1, m2, m3, ovf, above = lax.fori_loop(
        0, nkc, gather_body, (ninf, ninf, ninf, ninf, jnp.zeros((n_part, SUBLANES, TQ), jnp.int32)))
    gathered = jnp.concatenate([m1, m2, m3], axis=0)
    above8 = jnp.sum(above, axis=0)
    above8 = jnp.broadcast_to(jnp.sum(above8, axis=0, keepdims=True), (SUBLANES, TQ))

    def count_ge_gathered(thr8):
        c8 = jnp.sum((gathered >= thr8[None]).astype(jnp.int32), axis=0)
        return above8 + jnp.broadcast_to(jnp.sum(c8, axis=0, keepdims=True), (SUBLANES, TQ))

    def refine(count_fn, base, unroll=False):
        cnt0 = count_fn(_key_to_float(base))
        def body(it, carry):
            off, cnt_lo = carry
            cand = off | jnp.left_shift(jnp.int32(1), WINDOW_BITS - it)
            cnt = count_fn(_key_to_float(base + cand))
            ok = cnt >= n_keep
            return jnp.where(ok, cand, off), jnp.where(ok, cnt, cnt_lo)
        carry = (jnp.zeros((SUBLANES, TQ), jnp.int32), cnt0)
        if unroll:
            for it in range(WINDOW_BITS + 1):
                carry = body(it, carry)
        else:
            carry = lax.fori_loop(0, WINDOW_BITS + 1, body, carry)
        key_ref[0] = base + carry[0]
        key_ref[1] = carry[1]

    kp0 = kp_ref[0, 0:KC, :]
    for hd in range(N_HEADS):
        lga_ref[hd] = jnp.dot(kp0, qpt_ref[0, hd], preferred_element_type=F32)
    refine(count_ge_gathered, k_base, unroll=True)

    dropped = jnp.max(jnp.where(ovf >= w_lo[None], 1, 0))
    in_window = count_ge_gathered(w_lo)
    outside = jnp.where(in_window < n_keep, 1, jnp.where(above8 >= n_keep, 1, 0))
    outside = jnp.where(w_lo > -FLT_MAX, outside, 0)
    gathered_ok = (dropped == 0) & (jnp.max(outside) == 0)

    @pl.when(jnp.logical_not(gathered_ok))
    def _():
        refine(count_ge, k_base_wide)

    key = key_ref[0]
    cnt_lo = key_ref[1]
    lo8 = _key_to_float(key)
    hi8 = _key_to_float(key + 1)
    short = lo8 == -jnp.inf
    lo8 = jnp.where(short, -FLT_MAX, lo8)
    lo = lo8[0:1]

    has_tie = jnp.max(jnp.where(short, n_keep, cnt_lo)) > n_keep

    @pl.when(jnp.logical_not(has_tie))
    def _():
        def body(c, carry):
            ks = pl.multiple_of(c * KC, KC)
            bias_ref[pl.ds(ks, KC), :] = jnp.where(st_ref[pl.ds(ks, KC), :] >= lo, 0.0, NEG)
            return carry
        lax.fori_loop(0, nkc, body, 0)

    @pl.when(has_tie)
    def _():
        hi = hi8[0:1]
        key_ref[1] = count_ge_gathered(hi8)

        @pl.when(jnp.logical_not(gathered_ok))
        def _():
            key_ref[1] = count_ge(hi8)

        need = (n_keep - key_ref[1]).astype(F32)[0:1]
        tri = (lax.broadcasted_iota(jnp.int32, (KC, KC), 0)
               >= lax.broadcasted_iota(jnp.int32, (KC, KC), 1)).astype(BF16)

        def tie_chunk(c, taken):
            ks = pl.multiple_of(c * KC, KC)
            s = st_ref[pl.ds(ks, KC), :]
            ge_lo = s >= lo
            ge_hi = s >= hi
            tied = jnp.where(ge_lo, jnp.where(ge_hi, 0.0, 1.0), 0.0).astype(BF16)
            rank = taken + jnp.dot(tri, tied, preferred_element_type=F32)
            bias_ref[pl.ds(ks, KC), :] = jnp.where(
                ge_lo, jnp.where(ge_hi, 0.0, jnp.where(rank <= need, 0.0, NEG)), NEG)
            return rank[KC - 1:KC, :]

        taken = lax.fori_loop(0, nkc // 2,
                              lambda i, t: tie_chunk(2 * i + 1, tie_chunk(2 * i, t)),
                              jnp.zeros((1, TQ), F32))

        @pl.when(nkc % 2 == 1)
        def _():
            tie_chunk(nkc - 1, taken)

    for acc_ref in acc_refs:
        acc_ref[...] = jnp.zeros(acc_ref.shape, F32)
    ones_rows = (lax.broadcasted_iota(jnp.int32, (N_ACC_ROWS - HEAD_DIM, KC), 0) == 0).astype(BF16)

    for m_ref in m_refs:
        m_ref[...] = jnp.full(m_ref.shape, NEG, F32)
    buf_a = (lga_ref, mxa_refs)
    buf_b = (lgb_ref, mxb_refs)

    def heads_step(c_qk, qk_buf, c_pv, pv_buf, raw_ready=False):
        if c_qk is not None:
            ks = pl.multiple_of(c_qk * KC, KC)
            kpc = kp_ref[0, pl.ds(ks, KC), :]
            bias = bias_ref[pl.ds(ks, KC), :]
        if c_pv is not None:
            vtc = jnp.concatenate([vt_ref[0, c_pv], ones_rows], axis=0)
        for hd in range(N_HEADS):
            if c_qk is not None:
                raw = qk_buf[0][hd] if raw_ready else jnp.dot(kpc, qpt_ref[0, hd],
                                                                preferred_element_type=F32)
                lg = raw + bias
                qk_buf[0][hd] = lg
                qk_buf[1][hd][0:1, :] = jnp.max(lg, axis=0, keepdims=True)
            if c_pv is not None:
                m_old = m_refs[hd][0:1, :]
                m_new = jnp.maximum(m_old, pv_buf[1][hd][0:1, :])
                alpha = jnp.exp2(m_old - m_new)
                p = jnp.exp2(pv_buf[0][hd] - m_new).astype(BF16)
                acc_refs[hd][...] = alpha * acc_refs[hd][...] + jnp.dot(
                    vtc, p, preferred_element_type=F32)
                m_refs[hd][0:1, :] = m_new

    n_steps = nkc - 1
    heads_step(0, buf_a, None, None, raw_ready=True)

    def pair_body(i, carry):
        heads_step(2 * i + 1, buf_b, 2 * i, buf_a)
        heads_step(2 * i + 2, buf_a, 2 * i + 1, buf_b)
        return carry

    lax.fori_loop(0, n_steps // 2, pair_body, 0)

    @pl.when(n_steps % 2 == 1)
    def _():
        heads_step(n_steps, buf_b, n_steps - 1, buf_a)
        heads_step(None, None, n_steps, buf_b)

    @pl.when(n_steps % 2 == 0)
    def _():
        heads_step(None, None, n_steps, buf_a)

    for hd in range(N_HEADS):
        acc = acc_refs[hd][...]
        ot_ref[hd * HEAD_DIM:(hd + 1) * HEAD_DIM, :] = acc[0:HEAD_DIM] / acc[HEAD_DIM:HEAD_DIM + 1]
    yb_ref[0] = (ot_ref[...].T * saz_ref[0]).astype(BF16)


def _post_kernel(x_ref, ya_ref, yb_ref, ga_ref, gb_ref, p_ref,
                 wa_ref, wb_ref, wo_ref, pg_ref, wpg_ref, wpp_ref, fg_ref, out_ref, *, final_norm):
    merged = (ga_ref[0] * jnp.dot(ya_ref[0], wa_ref[...], preferred_element_type=F32)
              + gb_ref[0] * jnp.dot(yb_ref[0], wb_ref[...], preferred_element_type=F32))
    x1 = x_ref[0] + jnp.dot(merged.astype(BF16), wo_ref[...], preferred_element_type=F32)
    gate = _sigmoid(jnp.dot(_rms(x1, pg_ref[...]).astype(BF16), wpg_ref[...],
                            preferred_element_type=F32))
    x2 = x1 + gate * jnp.dot(p_ref[0].astype(BF16), wpp_ref[...], preferred_element_type=F32)
    out_ref[0] = _rms(x2, fg_ref[...]) if final_norm else x2


def _pack_w_in(w):
    o_in, o_z, o_q, o_k, o_v, o_az, o_iq, o_ik, o_iw, o_ga, o_gb = (
        0, 1024, 1536, 2048, 2112, 2176, 2688, 3200, 3264, 3272, 4296)
    pad = jnp.zeros((D_MODEL, LANES - IDX_DIM - N_IDX_HEADS), w.dtype)
    cols = [w[:, o_in:o_z], w[:, o_z:o_q], w[:, o_az:o_iq],
            w[:, o_ga:o_gb], w[:, o_gb:o_gb + D_MODEL], w[:, o_k:o_az],
            w[:, o_ik:o_ga], pad]
    w_tok = jnp.concatenate(cols, axis=1).astype(BF16)
    rows = [w[:, o_q:o_k], w[:, o_iq:o_ik], w[:, o_v:o_az]]
    w_feat = jnp.concatenate(rows, axis=1).T.astype(BF16)
    return w_tok, w_feat


def _layer(x, p, norm_g, w_in, conv_w, conv_b, ln_g, ln_b, w_a, w_b, w_o,
           ple_g, w_pg, w_pp, out_g, final_norm):
    B, L, D = x.shape
    n_keep = min(TOPK_MAX, L // 4)
    cp = functools.partial(pltpu.CompilerParams, vmem_limit_bytes=VMEM_LIMIT)
    row = lambda a: a.reshape(1, -1)

    tm1 = TM_PROJ
    w_tok, w_feat = _pack_w_in(w_in)
    tok = lambda w: pl.BlockSpec((1, tm1, w), lambda b, i: (b, i, 0))
    featm = lambda r: pl.BlockSpec((1, N_HEADS, r, tm1), lambda b, i: (b, 0, 0, i))
    const = lambda a: pl.BlockSpec(a.shape, lambda b, i: (0,) * a.ndim, pipeline_mode=pl.Buffered(1))
    f32s = lambda w: jax.ShapeDtypeStruct((B, L, w), F32)
    consts1 = [row(norm_g), w_tok, w_feat, conv_w, row(conv_b), row(ln_g), row(ln_b)]
    ya, saz, ga, gb, qpt, kp, vt, iqt, ik, iwt = pl.pallas_call(
        functools.partial(_proj_kernel, tm=tm1),
        grid=(B, L // tm1),
        in_specs=[tok(D)] + [const(a) for a in consts1],
        out_specs=[tok(D_CONV), tok(D_ATTN), tok(D), tok(D),
                   featm(LANES), tok(LANES),
                   pl.BlockSpec((1, tm1 // KC, HEAD_DIM, KC), lambda b, i: (b, i, 0, 0)),
                   featm(IDX_DIM), tok(IDX_DIM),
                   pl.BlockSpec((1, N_IDX_HEADS, tm1), lambda b, i: (b, 0, i))],
        out_shape=[jax.ShapeDtypeStruct((B, L, D_CONV), BF16), f32s(D_ATTN), f32s(D), f32s(D),
                   jax.ShapeDtypeStruct((B, N_HEADS, LANES, L), BF16),
                   jax.ShapeDtypeStruct((B, L, LANES), BF16),
                   jax.ShapeDtypeStruct((B, L // KC, HEAD_DIM, KC), BF16),
                   jax.ShapeDtypeStruct((B, N_IDX_HEADS, IDX_DIM, L), BF16),
                   jax.ShapeDtypeStruct((B, L, IDX_DIM), BF16),
                   jax.ShapeDtypeStruct((B, N_IDX_HEADS, L), F32)],
        scratch_shapes=[pltpu.VMEM((HALO + tm1, D_CONV), F32),
                        pltpu.VMEM((SUBLANES - 1, HALO + tm1 - SUBLANES, D_CONV), F32),
                        pltpu.VMEM((HALO, D_CONV), F32)],
        compiler_params=cp(dimension_semantics=("arbitrary", "arbitrary")),
        name="proj",
    )(x, *consts1)

    yb = pl.pallas_call(
        functools.partial(_attn_kernel, n_keep=n_keep),
        grid=(B, L // TQ),
        in_specs=[pl.BlockSpec((1, N_IDX_HEADS, IDX_DIM, TQ), lambda b, j: (b, 0, 0, j)),
                  pl.BlockSpec((1, L, IDX_DIM), lambda b, j: (b, 0, 0)),
                  pl.BlockSpec((1, N_IDX_HEADS, TQ), lambda b, j: (b, 0, j)),
                  pl.BlockSpec((1, N_HEADS, LANES, TQ), lambda b, j: (b, 0, 0, j)),
                  pl.BlockSpec((1, L, LANES), lambda b, j: (b, 0, 0)),
                  pl.BlockSpec((1, L // KC, HEAD_DIM, KC), lambda b, j: (b, 0, 0, 0)),
                  pl.BlockSpec((1, TQ, D_ATTN), lambda b, j: (b, j, 0))],
        out_specs=pl.BlockSpec((1, TQ, D_ATTN), lambda b, j: (b, j, 0)),
        out_shape=jax.ShapeDtypeStruct((B, L, D_ATTN), BF16),
        scratch_shapes=[pltpu.VMEM((L, TQ), F32),
                        pltpu.VMEM((L, TQ), BF16),
                        pltpu.VMEM((2, SUBLANES, TQ), jnp.int32),
                        pltpu.VMEM((L, TQ), F32),
                        pltpu.VMEM((D_ATTN, TQ), F32),
                        pltpu.VMEM((N_HEADS, KC, TQ), F32),
                        pltpu.VMEM((N_HEADS, KC, TQ), F32)]
                       + [pltpu.VMEM((SUBLANES, TQ), F32)] * (3 * N_HEADS)
                       + [pltpu.VMEM((N_ACC_ROWS, TQ), F32)] * N_HEADS,
        compiler_params=cp(dimension_semantics=("arbitrary", "arbitrary")),
        name="attn",
    )(iqt, ik, iwt, qpt, kp, vt, saz)

    tm3 = TM_POST
    tok = lambda w: pl.BlockSpec((1, tm3, w), lambda b, i: (b, i, 0))
    consts3 = [w_a.astype(BF16), w_b.astype(BF16), w_o.astype(BF16), row(ple_g),
               w_pg.astype(BF16), w_pp.astype(BF16), row(out_g)]
    return pl.pallas_call(
        functools.partial(_post_kernel, final_norm=final_norm),
        grid=(B, L // tm3),
        in_specs=[tok(D), tok(D_CONV), tok(D_ATTN), tok(D), tok(D), tok(D_PLE)]
                 + [const(a) for a in consts3],
        out_specs=tok(D),
        out_shape=jax.ShapeDtypeStruct((B, L, D), F32),
        compiler_params=cp(dimension_semantics=("arbitrary", "arbitrary")),
        name="post",
    )(x, ya, yb, ga, gb, p, *consts3)


def kernel(x, p, norm_g, w_in, conv_w, conv_b, conv_ln_g, conv_ln_b, w_a_out, w_b_out,
           w_o, ple_norm_g, w_ple_gate, w_ple_proj, final_g):
    depth = w_in.shape[0]
    for i in range(depth):
        x = _layer(x, p[i], norm_g[i], w_in[i], conv_w[i], conv_b[i], conv_ln_g[i],
                   conv_ln_b[i], w_a_out[i], w_b_out[i], w_o[i], ple_norm_g[i],
                   w_ple_gate[i], w_ple_proj[i], final_g, i == depth - 1)
    return x
```

```python
import functools
import math

import jax
import jax.numpy as jnp
import numpy as np
from jax import lax
from jax.experimental import pallas as pl
from jax.experimental.pallas import tpu as pltpu

D_MODEL = 1024
D_PLE = 256
D_CONV = 512
CONV_WIDTH = 31
N_HEADS = 8
HEAD_DIM = 64
D_ATTN = N_HEADS * HEAD_DIM
N_IDX_HEADS = 8
IDX_DIM = 64
TOPK_MAX = 256
EPS = 1e-6

LANES = 128
SUBLANES = 8
TQ = 256
KC = 256
TM_PROJ = 512
TM_POST = 512
HALO = 32
NEG = -1e30
FLT_MAX = float(np.finfo(np.float32).max)
FLT_TINY = float(np.finfo(np.float32).tiny)
BF16_ROWS = 16
WINDOW_BITS = 16
WINDOW = 2 ** WINDOW_BITS
LOG2E = math.log2(math.e)
POS_SHIFT = 6
POS_RADIX = 2 ** POS_SHIFT
N_ACC_ROWS = HEAD_DIM + BF16_ROWS
VMEM_LIMIT = 56 * 1024 * 1024

F32 = jnp.float32
BF16 = jnp.bfloat16

_C_VAL, _C_GATE, _C_Z, _C_AZ, _C_GA, _C_GB, _C_KV, _C_IKW, _C_END = (
    0, 512, 1024, 1536, 2048, 3072, 4096, 4224, 4352)
_R_Q, _R_IQ, _R_V, _R_END = 0, 512, 1024, 1088

_NT = (((1,), (1,)), ((), ()))


def _bf16_split3(a):
    a = np.float32(a)
    p1 = np.float32(a.astype(BF16))
    p2 = np.float32(np.float32(a - p1).astype(BF16))
    p3 = np.float32(a - p1 - p2)
    assert np.float32(np.float32(p3).astype(BF16)) == p3
    return float(p1), float(p2), float(p3)


def _sigmoid(x):
    return 1.0 / (1.0 + jnp.exp(-x))


def _silu(x):
    return x * _sigmoid(x)


def _rms(x, g):
    return x * lax.rsqrt(jnp.mean(x * x, axis=-1, keepdims=True) + EPS) * g


def _proj_kernel(x_ref, g_ref, w_ref, wt_ref, cw_ref, cb_ref, lng_ref, lnb_ref,
                 ya_ref, saz_ref, ga_ref, gb_ref,
                 qpt_ref, kp_ref, vt_ref, iqt_ref, ik_ref, iwt_ref,
                 ext_ref, es_ref, carry_ref, *, tm):
    i = pl.program_id(1)
    h = _rms(x_ref[0], g_ref[...]).astype(BF16)

    def mm(a, b):
        return jnp.dot(h, w_ref[:, a:b], preferred_element_type=F32)

    def mm_t(a, b):
        return lax.dot_general(wt_ref[a:b, :], h, _NT, preferred_element_type=F32)

    u = mm(_C_VAL, _C_GATE) * _sigmoid(mm(_C_GATE, _C_Z))
    prev = carry_ref[...]
    ext_ref[0:HALO, :] = jnp.where(i > 0, prev, jnp.zeros_like(prev))
    ext_ref[HALO:, :] = u
    carry_ref[...] = u[tm - HALO:, :]
    n_sh = HALO + tm - SUBLANES
    for r in range(1, SUBLANES):
        es_ref[r - 1] = ext_ref[r:r + n_sh, :]

    def w_conformer():
        base = HALO - (CONV_WIDTH - 1)
        c = jnp.zeros((tm, D_CONV), F32)
        for jj in range(CONV_WIDTH):
            row8, r = divmod(base + jj, SUBLANES)
            row = row8 * SUBLANES
            tap = ext_ref[row:row + tm, :] if r == 0 else es_ref[r - 1, row:row + tm, :]
            c = c + cw_ref[jj:jj + 1, :] * tap
        c = c + cb_ref[...]
        mu = jnp.mean(c, axis=-1, keepdims=True)
        cc = c - mu
        var = jnp.mean(cc * cc, axis=-1, keepdims=True)
        ln = cc * lax.rsqrt(var + EPS) * lng_ref[...] + lnb_ref[...]
        ya_ref[0] = (_silu(ln) * _silu(mm(_C_Z, _C_AZ))).astype(BF16)

    def w_saz():
        saz_ref[0] = _silu(mm(_C_AZ, _C_GA))

    def w_gate(ref, c0, half):
        def run():
            lo = half * (D_MODEL // 2)
            ref[0, :, lo:lo + D_MODEL // 2] = _sigmoid(mm(c0 + lo, c0 + lo + D_MODEL // 2))
        return run

    def w_keys():
        lane = lax.broadcasted_iota(jnp.int32, (tm, LANES), 1)
        pos = i * tm + lax.broadcasted_iota(jnp.int32, (tm, LANES), 0)
        col = lane - HEAD_DIM
        kext = jnp.where(col < 3, 1.0,
               jnp.where(col < 6, (pos >> POS_SHIFT).astype(F32),
               jnp.where(col < 9, (pos & (POS_RADIX - 1)).astype(F32), 0.0)))
        kp_ref[0] = jnp.where(col < 0, mm(_C_KV, _C_IKW), kext).astype(BF16)
        ikw = mm(_C_IKW, _C_END)
        ik_ref[0] = ikw[:, :IDX_DIM].astype(BF16)
        iwt_ref[0] = ikw.T[IDX_DIM:IDX_DIM + N_IDX_HEADS, :] * (N_IDX_HEADS ** -0.5)

    def w_queries():
        sub = lax.broadcasted_iota(jnp.int32, (BF16_ROWS, tm), 0)
        tpos = (i * tm + lax.broadcasted_iota(jnp.int32, (BF16_ROWS, tm), 1)).astype(F32)
        qt = mm_t(_R_Q, _R_IQ) * (LOG2E * HEAD_DIM ** -0.5)
        for hd in range(N_HEADS):
            a = np.float32(2.0 ** (-8.0 * (hd + 1) / N_HEADS) * LOG2E)
            a1, a2, a3 = _bf16_split3(a)
            v = tpos * float(-a)
            v1 = v.astype(BF16).astype(F32)
            v2 = (v - v1).astype(BF16).astype(F32)
            v3 = v - v1 - v2
            ext = jnp.where(sub == 0, v1, jnp.where(sub == 1, v2, jnp.where(sub == 2, v3,
                  jnp.where(sub == 3, POS_RADIX * a1, jnp.where(sub == 4, POS_RADIX * a2,
                  jnp.where(sub == 5, POS_RADIX * a3, jnp.where(sub == 6, a1,
                  jnp.where(sub == 7, a2, jnp.where(sub == 8, a3, 0.0)))))))))
            qpt_ref[0, hd, 0:HEAD_DIM, :] = qt[hd * HEAD_DIM:(hd + 1) * HEAD_DIM, :].astype(BF16)
            qpt_ref[0, hd, HEAD_DIM:HEAD_DIM + BF16_ROWS, :] = ext.astype(BF16)
            qpt_ref[0, hd, HEAD_DIM + BF16_ROWS:, :] = jnp.zeros((LANES - HEAD_DIM - BF16_ROWS, tm), BF16)

    def w_index_values():
        iqt = mm_t(_R_IQ, _R_V) * (IDX_DIM ** -0.5)
        for hd in range(N_IDX_HEADS):
            iqt_ref[0, hd] = iqt[hd * IDX_DIM:(hd + 1) * IDX_DIM, :].astype(BF16)
        vt = mm_t(_R_V, _R_END).astype(BF16)
        for c in range(tm // KC):
            vt_ref[0, c] = vt[:, c * KC:(c + 1) * KC]

    for work in (w_conformer, w_saz, w_gate(ga_ref, _C_GA, 0), w_gate(ga_ref, _C_GA, 1),
                 w_gate(gb_ref, _C_GB, 0), w_gate(gb_ref, _C_GB, 1), w_keys, w_queries,
                 w_index_values):
        work()


def _threshold_of_key(k, bits, n_subnormal):
    f = lax.bitcast_convert_type(bits, F32)
    pos = jnp.where(k < n_subnormal, jnp.where(k == 0, 0.0, FLT_TINY), f)
    neg = jnp.where(k >= -n_subnormal, 0.0, jnp.where(f != f, -jnp.inf, f))
    return jnp.where(k >= 0, pos, neg)


def _key_to_float(k):
    return _threshold_of_key(k, jnp.where(k >= 0, k, k ^ jnp.int32(0x7FFFFFFF)), 2 ** 23)


def _key16_to_float(k):
    return _threshold_of_key(k, jnp.left_shift(jnp.where(k >= 0, k, k ^ jnp.int32(0x7FFF)), 16), 2 ** 7)


def _attn_kernel(iqt_ref, ik_ref, iwt_ref, qpt_ref, kp_ref, vt_ref, saz_ref, yb_ref,
                 st_ref, rt_ref, key_ref, bias_ref, ot_ref, lga_ref, lgb_ref, *head_refs, n_keep):
    m_refs, mxa_refs, mxb_refs, acc_refs = (head_refs[g * N_HEADS:(g + 1) * N_HEADS] for g in range(4))
    j = pl.program_id(1)
    t0 = j * TQ
    nkc = j + 1

    tcol = t0 + lax.broadcasted_iota(jnp.int32, (KC, TQ), 1)
    srow0 = lax.broadcasted_iota(jnp.int32, (KC, TQ), 0)

    def score_chunk(c):
        ks = pl.multiple_of(c * KC, KC)
        kic = ik_ref[0, pl.ds(ks, KC), :]
        acc = jnp.zeros((KC, TQ), F32)
        for hd in range(N_IDX_HEADS):
            d = jnp.dot(kic, iqt_ref[0, hd], preferred_element_type=F32)
            acc = acc + iwt_ref[0, hd:hd + 1, :] * jnp.maximum(d, 0.0)
        masked = jnp.where((ks + srow0) <= tcol, acc, -jnp.inf)
        st_ref[pl.ds(ks, KC), :] = masked
        rt_ref[pl.ds(ks, KC), :] = masked.astype(BF16)

    def score_pair(i, carry):
        score_chunk(2 * i)
        score_chunk(2 * i + 1)
        return carry

    lax.fori_loop(0, nkc // 2, score_pair, 0)

    @pl.when(nkc % 2 == 1)
    def _():
        score_chunk(nkc - 1)

    n_part = 4

    def count_ge(thr8):
        def body(c, cnt):
            ks = pl.multiple_of(c * KC, KC)
            s = st_ref[pl.ds(ks, KC), :].reshape(KC // (SUBLANES * n_part), n_part, SUBLANES, TQ)
            return cnt + jnp.sum((s >= thr8[None, None]).astype(jnp.int32), axis=0)
        cnt = lax.fori_loop(0, nkc, body, jnp.zeros((n_part, SUBLANES, TQ), jnp.int32))
        cnt8 = jnp.sum(cnt, axis=0)
        return jnp.broadcast_to(jnp.sum(cnt8, axis=0, keepdims=True), (SUBLANES, TQ))

    def count_ge16(thr_bf):
        def chunk(c):
            ks = pl.multiple_of(c * KC, KC)
            r = rt_ref[pl.ds(ks, KC), :].reshape(KC // (BF16_ROWS * 2), 2, BF16_ROWS, TQ)
            w = jnp.where(r >= thr_bf[None, None], jnp.ones((), BF16), jnp.zeros((), BF16))
            part = w[0]
            for g in range(1, w.shape[0]):
                part = part + w[g]
            return part
        cnt = lax.fori_loop(0, nkc // 2, lambda i, a: a + chunk(2 * i) + chunk(2 * i + 1),
                            jnp.zeros((2, BF16_ROWS, TQ), BF16))
        cnt = lax.cond(nkc % 2 == 1, lambda a: a + chunk(nkc - 1), lambda a: a, cnt).astype(F32)
        tot = jnp.sum(cnt[0] + cnt[1], axis=0, keepdims=True)
        return jnp.broadcast_to(tot, (SUBLANES, TQ))

    def thr16(k16):
        f = _key16_to_float(k16)
        return jnp.concatenate([f] * (BF16_ROWS // SUBLANES), axis=0).astype(BF16)

    k16 = jnp.where(count_ge16(thr16(jnp.zeros((SUBLANES, TQ), jnp.int32))) >= n_keep,
                    jnp.int32(0), jnp.int32(-(2 ** 15)))

    def bit16_body(it, k):
        cand = k | jnp.left_shift(jnp.int32(1), 14 - it)
        return jnp.where(count_ge16(thr16(cand)) >= n_keep, cand, k)

    k16 = lax.fori_loop(0, 15, bit16_body, k16)
    hb_bits = lax.bitcast_convert_type(_key16_to_float(k16), jnp.int32)
    k_hb = hb_bits ^ ((hb_bits >> 31) & jnp.int32(0x7FFFFFFF))
    k_base = k_hb - WINDOW // 2
    k_base_wide = k_hb - WINDOW
    w_lo = jnp.maximum(_key_to_float(k_base), -FLT_MAX)
    w_hi = _key_to_float(k_base + WINDOW + 1)

    def gather_body(c, carry):
        m1, m2, m3, ovf, above = carry
        ks = pl.multiple_of(c * KC, KC)
        s = st_ref[pl.ds(ks, KC), :].reshape(KC // (SUBLANES * n_part), n_part, SUBLANES, TQ)
        for g in range(s.shape[0]):
            sg = s[g]
            ge_hi = sg >= w_hi[None]
            above = above + ge_hi.astype(jnp.int32)
            v = jnp.where(ge_hi, -jnp.inf, sg)
            t1 = jnp.minimum(m1, v)
            m1 = jnp.maximum(m1, v)
            t2 = jnp.minimum(m2, t1)
            m2 = jnp.maximum(m2, t1)
            ovf = jnp.maximum(ovf, jnp.minimum(m3, t2))
            m3 = jnp.maximum(m3, t2)
        return m1, m2, m3, ovf, above

    ninf = jnp.full((n_part, SUBLANES, TQ), -jnp.inf, F32)
    m1, m2, m3, ovf, above = lax.fori_loop(
        0, nkc, gather_body, (ninf, ninf, ninf, ninf, jnp.zeros((n_part, SUBLANES, TQ), jnp.int32)))
    gathered = jnp.concatenate([m1, m2, m3], axis=0)
    above8 = jnp.sum(above, axis=0)
    above8 = jnp.broadcast_to(jnp.sum(above8, axis=0, keepdims=True), (SUBLANES, TQ))

    def count_ge_gathered(thr8):
        c8 = jnp.sum((gathered >= thr8[None]).astype(jnp.int32), axis=0)
        return above8 + jnp.broadcast_to(jnp.sum(c8, axis=0, keepdims=True), (SUBLANES, TQ))

    def refine(count_fn, base, unroll=False):
        cnt0 = count_fn(_key_to_float(base))
        def body(it, carry):
            off, cnt_lo = carry
            cand = off | jnp.left_shift(jnp.int32(1), WINDOW_BITS - it)
            cnt = count_fn(_key_to_float(base + cand))
            ok = cnt >= n_keep
            return jnp.where(ok, cand, off), jnp.where(ok, cnt, cnt_lo)
        carry = (jnp.zeros((SUBLANES, TQ), jnp.int32), cnt0)
        if unroll:
            for it in range(WINDOW_BITS + 1):
                carry = body(it, carry)
        else:
            carry = lax.fori_loop(0, WINDOW_BITS + 1, body, carry)
        key_ref[0] = base + carry[0]
        key_ref[1] = carry[1]

    kp0 = kp_ref[0, 0:KC, :]
    for hd in range(N_HEADS):
        lga_ref[hd] = jnp.dot(kp0, qpt_ref[0, hd], preferred_element_type=F32)
    refine(count_ge_gathered, k_base, unroll=True)

    dropped = jnp.max(jnp.where(ovf >= w_lo[None], 1, 0))
    in_window = count_ge_gathered(w_lo)
    outside = jnp.where(in_window < n_keep, 1, jnp.where(above8 >= n_keep, 1, 0))
    outside = jnp.where(w_lo > -FLT_MAX, outside, 0)
    gathered_ok = (dropped == 0) & (jnp.max(outside) == 0)

    @pl.when(jnp.logical_not(gathered_ok))
    def _():
        refine(count_ge, k_base_wide)

    key = key_ref[0]
    cnt_lo = key_ref[1]
    lo8 = _key_to_float(key)
    hi8 = _key_to_float(key + 1)
    short = lo8 == -jnp.inf
    lo8 = jnp.where(short, -FLT_MAX, lo8)
    lo = lo8[0:1]

    has_tie = jnp.max(jnp.where(short, n_keep, cnt_lo)) > n_keep

    @pl.when(jnp.logical_not(has_tie))
    def _():
        def body(c, carry):
            ks = pl.multiple_of(c * KC, KC)
            bias_ref[pl.ds(ks, KC), :] = jnp.where(st_ref[pl.ds(ks, KC), :] >= lo, 0.0, NEG)
            return carry
        lax.fori_loop(0, nkc, body, 0)

    @pl.when(has_tie)
    def _():
        hi = hi8[0:1]
        key_ref[1] = count_ge_gathered(hi8)

        @pl.when(jnp.logical_not(gathered_ok))
        def _():
            key_ref[1] = count_ge(hi8)

        need = (n_keep - key_ref[1]).astype(F32)[0:1]
        tri = (lax.broadcasted_iota(jnp.int32, (KC, KC), 0)
               >= lax.broadcasted_iota(jnp.int32, (KC, KC), 1)).astype(BF16)

        def tie_chunk(c, taken):
            ks = pl.multiple_of(c * KC, KC)
            s = st_ref[pl.ds(ks, KC), :]
            ge_lo = s >= lo
            ge_hi = s >= hi
            tied = jnp.where(ge_lo, jnp.where(ge_hi, 0.0, 1.0), 0.0).astype(BF16)
            rank = taken + jnp.dot(tri, tied, preferred_element_type=F32)
            bias_ref[pl.ds(ks, KC), :] = jnp.where(
                ge_lo, jnp.where(ge_hi, 0.0, jnp.where(rank <= need, 0.0, NEG)), NEG)
            return rank[KC - 1:KC, :]

        taken = lax.fori_loop(0, nkc // 2,
                              lambda i, t: tie_chunk(2 * i + 1, tie_chunk(2 * i, t)),
                              jnp.zeros((1, TQ), F32))

        @pl.when(nkc % 2 == 1)
        def _():
            tie_chunk(nkc - 1, taken)

    for acc_ref in acc_refs:
        acc_ref[...] = jnp.zeros(acc_ref.shape, F32)
    ones_rows = (lax.broadcasted_iota(jnp.int32, (N_ACC_ROWS - HEAD_DIM, KC), 0) == 0).astype(BF16)

    for m_ref in m_refs:
        m_ref[...] = jnp.full(m_ref.shape, NEG, F32)
    buf_a = (lga_ref, mxa_refs)
    buf_b = (lgb_ref, mxb_refs)

    def heads_step(c_qk, qk_buf, c_pv, pv_buf, raw_ready=False):
        if c_qk is not None:
            ks = pl.multiple_of(c_qk * KC, KC)
            kpc = kp_ref[0, pl.ds(ks, KC), :]
            bias = bias_ref[pl.ds(ks, KC), :]
        if c_pv is not None:
            vtc = jnp.concatenate([vt_ref[0, c_pv], ones_rows], axis=0)
        for hd in range(N_HEADS):
            if c_qk is not None:
                raw = qk_buf[0][hd] if raw_ready else jnp.dot(kpc, qpt_ref[0, hd],
                                                                preferred_element_type=F32)
                lg = raw + bias
                qk_buf[0][hd] = lg
                qk_buf[1][hd][0:1, :] = jnp.max(lg, axis=0, keepdims=True)
            if c_pv is not None:
                m_old = m_refs[hd][0:1, :]
                m_new = jnp.maximum(m_old, pv_buf[1][hd][0:1, :])
                alpha = jnp.exp2(m_old - m_new)
                p = jnp.exp2(pv_buf[0][hd] - m_new).astype(BF16)
                acc_refs[hd][...] = alpha * acc_refs[hd][...] + jnp.dot(
                    vtc, p, preferred_element_type=F32)
                m_refs[hd][0:1, :] = m_new

    n_steps = nkc - 1
    heads_step(0, buf_a, None, None, raw_ready=True)

    def quad_body(i, carry):
        heads_step(4 * i + 1, buf_b, 4 * i, buf_a)
        heads_step(4 * i + 2, buf_a, 4 * i + 1, buf_b)
        heads_step(4 * i + 3, buf_b, 4 * i + 2, buf_a)
        heads_step(4 * i + 4, buf_a, 4 * i + 3, buf_b)
        return carry

    lax.fori_loop(0, n_steps // 4, quad_body, 0)
    done = (n_steps // 4) * 4
    rem = n_steps - done

    @pl.when(rem == 0)
    def _():
        heads_step(None, None, done, buf_a)

    @pl.when(rem == 1)
    def _():
        heads_step(done + 1, buf_b, done, buf_a)
        heads_step(None, None, done + 1, buf_b)

    @pl.when(rem == 2)
    def _():
        heads_step(done + 1, buf_b, done, buf_a)
        heads_step(done + 2, buf_a, done + 1, buf_b)
        heads_step(None, None, done + 2, buf_a)

    @pl.when(rem == 3)
    def _():
        heads_step(done + 1, buf_b, done, buf_a)
        heads_step(done + 2, buf_a, done + 1, buf_b)
        heads_step(done + 3, buf_b, done + 2, buf_a)
        heads_step(None, None, done + 3, buf_b)

    for hd in range(N_HEADS):
        acc = acc_refs[hd][...]
        ot_ref[hd * HEAD_DIM:(hd + 1) * HEAD_DIM, :] = acc[0:HEAD_DIM] / acc[HEAD_DIM:HEAD_DIM + 1]
    yb_ref[0] = (ot_ref[...].T * saz_ref[0]).astype(BF16)


def _post_kernel(x_ref, ya_ref, yb_ref, ga_ref, gb_ref, p_ref,
                 wa_ref, wb_ref, wo_ref, pg_ref, wpg_ref, wpp_ref, fg_ref, out_ref, *, final_norm):
    merged = (ga_ref[0] * jnp.dot(ya_ref[0], wa_ref[...], preferred_element_type=F32)
              + gb_ref[0] * jnp.dot(yb_ref[0], wb_ref[...], preferred_element_type=F32))
    x1 = x_ref[0] + jnp.dot(merged.astype(BF16), wo_ref[...], preferred_element_type=F32)
    gate = _sigmoid(jnp.dot(_rms(x1, pg_ref[...]).astype(BF16), wpg_ref[...],
                            preferred_element_type=F32))
    x2 = x1 + gate * jnp.dot(p_ref[0].astype(BF16), wpp_ref[...], preferred_element_type=F32)
    out_ref[0] = _rms(x2, fg_ref[...]) if final_norm else x2


def _pack_w_in(w):
    o_in, o_z, o_q, o_k, o_v, o_az, o_iq, o_ik, o_iw, o_ga, o_gb = (
        0, 1024, 1536, 2048, 2112, 2176, 2688, 3200, 3264, 3272, 4296)
    pad = jnp.zeros((D_MODEL, LANES - IDX_DIM - N_IDX_HEADS), w.dtype)
    cols = [w[:, o_in:o_z], w[:, o_z:o_q], w[:, o_az:o_iq],
            w[:, o_ga:o_gb], w[:, o_gb:o_gb + D_MODEL], w[:, o_k:o_az],
            w[:, o_ik:o_ga], pad]
    w_tok = jnp.concatenate(cols, axis=1).astype(BF16)
    rows = [w[:, o_q:o_k], w[:, o_iq:o_ik], w[:, o_v:o_az]]
    w_feat = jnp.concatenate(rows, axis=1).T.astype(BF16)
    return w_tok, w_feat


def _layer(x, p, norm_g, w_in, conv_w, conv_b, ln_g, ln_b, w_a, w_b, w_o,
           ple_g, w_pg, w_pp, out_g, final_norm):
    B, L, D = x.shape
    n_keep = min(TOPK_MAX, L // 4)
    cp = functools.partial(pltpu.CompilerParams, vmem_limit_bytes=VMEM_LIMIT)
    row = lambda a: a.reshape(1, -1)

    tm1 = TM_PROJ
    w_tok, w_feat = _pack_w_in(w_in)
    tok = lambda w: pl.BlockSpec((1, tm1, w), lambda b, i: (b, i, 0))
    featm = lambda r: pl.BlockSpec((1, N_HEADS, r, tm1), lambda b, i: (b, 0, 0, i))
    const = lambda a: pl.BlockSpec(a.shape, lambda b, i: (0,) * a.ndim, pipeline_mode=pl.Buffered(1))
    f32s = lambda w: jax.ShapeDtypeStruct((B, L, w), F32)
    consts1 = [row(norm_g), w_tok, w_feat, conv_w, row(conv_b), row(ln_g), row(ln_b)]
    ya, saz, ga, gb, qpt, kp, vt, iqt, ik, iwt = pl.pallas_call(
        functools.partial(_proj_kernel, tm=tm1),
        grid=(B, L // tm1),
        in_specs=[tok(D)] + [const(a) for a in consts1],
        out_specs=[tok(D_CONV), tok(D_ATTN), tok(D), tok(D),
                   featm(LANES), tok(LANES),
                   pl.BlockSpec((1, tm1 // KC, HEAD_DIM, KC), lambda b, i: (b, i, 0, 0)),
                   featm(IDX_DIM), tok(IDX_DIM),
                   pl.BlockSpec((1, N_IDX_HEADS, tm1), lambda b, i: (b, 0, i))],
        out_shape=[jax.ShapeDtypeStruct((B, L, D_CONV), BF16), f32s(D_ATTN), f32s(D), f32s(D),
                   jax.ShapeDtypeStruct((B, N_HEADS, LANES, L), BF16),
                   jax.ShapeDtypeStruct((B, L, LANES), BF16),
                   jax.ShapeDtypeStruct((B, L // KC, HEAD_DIM, KC), BF16),
                   jax.ShapeDtypeStruct((B, N_IDX_HEADS, IDX_DIM, L), BF16),
                   jax.ShapeDtypeStruct((B, L, IDX_DIM), BF16),
                   jax.ShapeDtypeStruct((B, N_IDX_HEADS, L), F32)],
        scratch_shapes=[pltpu.VMEM((HALO + tm1, D_CONV), F32),
                        pltpu.VMEM((SUBLANES - 1, HALO + tm1 - SUBLANES, D_CONV), F32),
                        pltpu.VMEM((HALO, D_CONV), F32)],
        compiler_params=cp(dimension_semantics=("arbitrary", "arbitrary")),
        name="proj",
    )(x, *consts1)

    yb = pl.pallas_call(
        functools.partial(_attn_kernel, n_keep=n_keep),
        grid=(B, L // TQ),
        in_specs=[pl.BlockSpec((1, N_IDX_HEADS, IDX_DIM, TQ), lambda b, j: (b, 0, 0, j)),
                  pl.BlockSpec((1, L, IDX_DIM), lambda b, j: (b, 0, 0)),
                  pl.BlockSpec((1, N_IDX_HEADS, TQ), lambda b, j: (b, 0, j)),
                  pl.BlockSpec((1, N_HEADS, LANES, TQ), lambda b, j: (b, 0, 0, j)),
                  pl.BlockSpec((1, L, LANES), lambda b, j: (b, 0, 0)),
                  pl.BlockSpec((1, L // KC, HEAD_DIM, KC), lambda b, j: (b, 0, 0, 0)),
                  pl.BlockSpec((1, TQ, D_ATTN), lambda b, j: (b, j, 0))],
        out_specs=pl.BlockSpec((1, TQ, D_ATTN), lambda b, j: (b, j, 0)),
        out_shape=jax.ShapeDtypeStruct((B, L, D_ATTN), BF16),
        scratch_shapes=[pltpu.VMEM((L, TQ), F32),
                        pltpu.VMEM((L, TQ), BF16),
                        pltpu.VMEM((2, SUBLANES, TQ), jnp.int32),
                        pltpu.VMEM((L, TQ), F32),
                        pltpu.VMEM((D_ATTN, TQ), F32),
                        pltpu.VMEM((N_HEADS, KC, TQ), F32),
                        pltpu.VMEM((N_HEADS, KC, TQ), F32)]
                       + [pltpu.VMEM((SUBLANES, TQ), F32)] * (3 * N_HEADS)
                       + [pltpu.VMEM((N_ACC_ROWS, TQ), F32)] * N_HEADS,
        compiler_params=cp(dimension_semantics=("arbitrary", "arbitrary")),
        name="attn",
    )(iqt, ik, iwt, qpt, kp, vt, saz)

    tm3 = TM_POST
    tok = lambda w: pl.BlockSpec((1, tm3, w), lambda b, i: (b, i, 0))
    consts3 = [w_a.astype(BF16), w_b.astype(BF16), w_o.astype(BF16), row(ple_g),
               w_pg.astype(BF16), w_pp.astype(BF16), row(out_g)]
    return pl.pallas_call(
        functools.partial(_post_kernel, final_norm=final_norm),
        grid=(B, L // tm3),
        in_specs=[tok(D), tok(D_CONV), tok(D_ATTN), tok(D), tok(D), tok(D_PLE)]
                 + [const(a) for a in consts3],
        out_specs=tok(D),
        out_shape=jax.ShapeDtypeStruct((B, L, D), F32),
        compiler_params=cp(dimension_semantics=("arbitrary", "arbitrary")),
        name="post",
    )(x, ya, yb, ga, gb, p, *consts3)


def kernel(x, p, norm_g, w_in, conv_w, conv_b, conv_ln_g, conv_ln_b, w_a_out, w_b_out,
           w_o, ple_norm_g, w_ple_gate, w_ple_proj, final_g):
    depth = w_in.shape[0]
    for i in range(depth):
        x = _layer(x, p[i], norm_g[i], w_in[i], conv_w[i], conv_b[i], conv_ln_g[i],
                   conv_ln_b[i], w_a_out[i], w_b_out[i], w_o[i], ple_norm_g[i],
                   w_ple_gate[i], w_ple_proj[i], final_g, i == depth - 1)
    return x
```

```python
import functools
import math

import jax
import jax.numpy as jnp
import numpy as np
from jax import lax
from jax.experimental import pallas as pl
from jax.experimental.pallas import tpu as pltpu

D_MODEL = 1024
D_PLE = 256
D_CONV = 512
CONV_WIDTH = 31
N_HEADS = 8
HEAD_DIM = 64
D_ATTN = N_HEADS * HEAD_DIM
N_IDX_HEADS = 8
IDX_DIM = 64
TOPK_MAX = 256
EPS = 1e-6

LANES = 128
SUBLANES = 8
TQ = 256
KC = 256
assert TQ == KC
TM_PROJ = 512
TM_POST = 512
HALO = 32
NEG = -1e30
FLT_MAX = float(np.finfo(np.float32).max)
FLT_TINY = float(np.finfo(np.float32).tiny)
BF16_ROWS = 16
WINDOW_BITS = 16
WINDOW = 2 ** WINDOW_BITS
LOG2E = math.log2(math.e)
POS_SHIFT = 6
POS_RADIX = 2 ** POS_SHIFT
N_ACC_ROWS = HEAD_DIM + BF16_ROWS
VMEM_LIMIT = 56 * 1024 * 1024

F32 = jnp.float32
BF16 = jnp.bfloat16

_C_VAL, _C_GATE, _C_Z, _C_AZ, _C_GA, _C_GB, _C_KV, _C_IKW, _C_END = (
    0, 512, 1024, 1536, 2048, 3072, 4096, 4224, 4352)
_R_Q, _R_IQ, _R_V, _R_END = 0, 512, 1024, 1088

_NT = (((1,), (1,)), ((), ()))


def _bf16_split3(a):
    a = np.float32(a)
    p1 = np.float32(a.astype(BF16))
    p2 = np.float32(np.float32(a - p1).astype(BF16))
    p3 = np.float32(a - p1 - p2)
    assert np.float32(np.float32(p3).astype(BF16)) == p3
    return float(p1), float(p2), float(p3)


def _sigmoid(x):
    return 1.0 / (1.0 + jnp.exp(-x))


def _silu(x):
    return x * _sigmoid(x)


def _rms(x, g):
    return x * lax.rsqrt(jnp.mean(x * x, axis=-1, keepdims=True) + EPS) * g


def _proj_kernel(x_ref, g_ref, w_ref, wt_ref, cw_ref, cb_ref, lng_ref, lnb_ref,
                 ya_ref, saz_ref, ga_ref, gb_ref,
                 qpt_ref, kp_ref, vt_ref, iqt_ref, ik_ref, iwt_ref,
                 ext_ref, es_ref, carry_ref, *, tm):
    i = pl.program_id(1)
    h = _rms(x_ref[0], g_ref[...]).astype(BF16)

    def mm(a, b):
        return jnp.dot(h, w_ref[:, a:b], preferred_element_type=F32)

    def mm_t(a, b):
        return lax.dot_general(wt_ref[a:b, :], h, _NT, preferred_element_type=F32)

    u = mm(_C_VAL, _C_GATE) * _sigmoid(mm(_C_GATE, _C_Z))
    prev = carry_ref[...]
    ext_ref[0:HALO, :] = jnp.where(i > 0, prev, jnp.zeros_like(prev))
    ext_ref[HALO:, :] = u
    carry_ref[...] = u[tm - HALO:, :]
    n_sh = HALO + tm - SUBLANES
    for r in range(1, SUBLANES):
        es_ref[r - 1] = ext_ref[r:r + n_sh, :]

    def w_conformer():
        base = HALO - (CONV_WIDTH - 1)
        c = jnp.zeros((tm, D_CONV), F32)
        for jj in range(CONV_WIDTH):
            row8, r = divmod(base + jj, SUBLANES)
            row = row8 * SUBLANES
            tap = ext_ref[row:row + tm, :] if r == 0 else es_ref[r - 1, row:row + tm, :]
            c = c + cw_ref[jj:jj + 1, :] * tap
        c = c + cb_ref[...]
        mu = jnp.mean(c, axis=-1, keepdims=True)
        cc = c - mu
        var = jnp.mean(cc * cc, axis=-1, keepdims=True)
        ln = cc * lax.rsqrt(var + EPS) * lng_ref[...] + lnb_ref[...]
        ya_ref[0] = (_silu(ln) * _silu(mm(_C_Z, _C_AZ))).astype(BF16)

    def w_saz():
        saz_ref[0] = _silu(mm(_C_AZ, _C_GA))

    def w_gate(ref, c0, half):
        def run():
            lo = half * (D_MODEL // 2)
            ref[0, :, lo:lo + D_MODEL // 2] = _sigmoid(mm(c0 + lo, c0 + lo + D_MODEL // 2))
        return run

    def w_keys():
        lane = lax.broadcasted_iota(jnp.int32, (tm, LANES), 1)
        pos = i * tm + lax.broadcasted_iota(jnp.int32, (tm, LANES), 0)
        col = lane - HEAD_DIM
        kext = jnp.where(col < 3, 1.0,
               jnp.where(col < 6, (pos >> POS_SHIFT).astype(F32),
               jnp.where(col < 9, (pos & (POS_RADIX - 1)).astype(F32), 0.0)))
        kp_ref[0] = jnp.where(col < 0, mm(_C_KV, _C_IKW), kext).astype(BF16)
        ikw = mm(_C_IKW, _C_END)
        ik_ref[0] = ikw[:, :IDX_DIM].astype(BF16)
        iwt_ref[0] = ikw.T[IDX_DIM:IDX_DIM + N_IDX_HEADS, :] * (N_IDX_HEADS ** -0.5)

    def w_queries():
        sub = lax.broadcasted_iota(jnp.int32, (BF16_ROWS, tm), 0)
        tpos = (i * tm + lax.broadcasted_iota(jnp.int32, (BF16_ROWS, tm), 1)).astype(F32)
        qt = mm_t(_R_Q, _R_IQ) * (LOG2E * HEAD_DIM ** -0.5)
        for hd in range(N_HEADS):
            a = np.float32(2.0 ** (-8.0 * (hd + 1) / N_HEADS) * LOG2E)
            a1, a2, a3 = _bf16_split3(a)
            v = tpos * float(-a)
            v1 = v.astype(BF16).astype(F32)
            v2 = (v - v1).astype(BF16).astype(F32)
            v3 = v - v1 - v2
            ext = jnp.where(sub == 0, v1, jnp.where(sub == 1, v2, jnp.where(sub == 2, v3,
                  jnp.where(sub == 3, POS_RADIX * a1, jnp.where(sub == 4, POS_RADIX * a2,
                  jnp.where(sub == 5, POS_RADIX * a3, jnp.where(sub == 6, a1,
                  jnp.where(sub == 7, a2, jnp.where(sub == 8, a3, 0.0)))))))))
            qpt_ref[0, hd, 0:HEAD_DIM, :] = qt[hd * HEAD_DIM:(hd + 1) * HEAD_DIM, :].astype(BF16)
            qpt_ref[0, hd, HEAD_DIM:HEAD_DIM + BF16_ROWS, :] = ext.astype(BF16)
            qpt_ref[0, hd, HEAD_DIM + BF16_ROWS:, :] = jnp.zeros((LANES - HEAD_DIM - BF16_ROWS, tm), BF16)

    def w_index_values():
        iqt = mm_t(_R_IQ, _R_V) * (IDX_DIM ** -0.5)
        for hd in range(N_IDX_HEADS):
            iqt_ref[0, hd] = iqt[hd * IDX_DIM:(hd + 1) * IDX_DIM, :].astype(BF16)
        vt = mm_t(_R_V, _R_END).astype(BF16)
        for c in range(tm // KC):
            vt_ref[0, c] = vt[:, c * KC:(c + 1) * KC]

    for work in (w_conformer, w_saz, w_gate(ga_ref, _C_GA, 0), w_gate(ga_ref, _C_GA, 1),
                 w_gate(gb_ref, _C_GB, 0), w_gate(gb_ref, _C_GB, 1), w_keys, w_queries,
                 w_index_values):
        work()


def _threshold_of_key(k, bits, n_subnormal):
    f = lax.bitcast_convert_type(bits, F32)
    pos = jnp.where(k < n_subnormal, jnp.where(k == 0, 0.0, FLT_TINY), f)
    neg = jnp.where(k >= -n_subnormal, 0.0, jnp.where(f != f, -jnp.inf, f))
    return jnp.where(k >= 0, pos, neg)


def _key_to_float(k):
    return _threshold_of_key(k, jnp.where(k >= 0, k, k ^ jnp.int32(0x7FFFFFFF)), 2 ** 23)


def _key16_to_float(k):
    return _threshold_of_key(k, jnp.left_shift(jnp.where(k >= 0, k, k ^ jnp.int32(0x7FFF)), 16), 2 ** 7)


def _attn_kernel(iqt_ref, ik_ref, iwt_ref, qpt_ref, kp_ref, vt_ref, saz_ref, yb_ref,
                 st_ref, rt_ref, key_ref, bias_ref, ot_ref, lga_ref, lgb_ref, *head_refs, n_keep):
    m_refs, mxa_refs, mxb_refs, acc_refs = (head_refs[g * N_HEADS:(g + 1) * N_HEADS] for g in range(4))
    j = pl.program_id(1)
    t0 = j * TQ
    nkc = j + 1

    tcol = t0 + lax.broadcasted_iota(jnp.int32, (KC, TQ), 1)
    srow0 = lax.broadcasted_iota(jnp.int32, (KC, TQ), 0)

    def score_chunk(c):
        ks = pl.multiple_of(c * KC, KC)
        kic = ik_ref[0, pl.ds(ks, KC), :]
        acc = jnp.zeros((KC, TQ), F32)
        for hd in range(N_IDX_HEADS):
            d = jnp.dot(kic, iqt_ref[0, hd], preferred_element_type=F32)
            acc = acc + iwt_ref[0, hd:hd + 1, :] * jnp.maximum(d, 0.0)
        masked = jnp.where((ks + srow0) <= tcol, acc, -jnp.inf)
        st_ref[pl.ds(ks, KC), :] = masked
        rt_ref[pl.ds(ks, KC), :] = masked.astype(BF16)

    def score_pair(i, carry):
        score_chunk(2 * i)
        score_chunk(2 * i + 1)
        return carry

    lax.fori_loop(0, nkc // 2, score_pair, 0)

    @pl.when(nkc % 2 == 1)
    def _():
        score_chunk(nkc - 1)

    n_part = 4

    def count_ge(thr8):
        def body(c, cnt):
            ks = pl.multiple_of(c * KC, KC)
            s = st_ref[pl.ds(ks, KC), :].reshape(KC // (SUBLANES * n_part), n_part, SUBLANES, TQ)
            return cnt + jnp.sum((s >= thr8[None, None]).astype(jnp.int32), axis=0)
        cnt = lax.fori_loop(0, nkc, body, jnp.zeros((n_part, SUBLANES, TQ), jnp.int32))
        cnt8 = jnp.sum(cnt, axis=0)
        return jnp.broadcast_to(jnp.sum(cnt8, axis=0, keepdims=True), (SUBLANES, TQ))

    def count_ge16(thr_bf):
        def chunk(c):
            ks = pl.multiple_of(c * KC, KC)
            r = rt_ref[pl.ds(ks, KC), :].reshape(KC // (BF16_ROWS * 2), 2, BF16_ROWS, TQ)
            w = jnp.where(r >= thr_bf[None, None], jnp.ones((), BF16), jnp.zeros((), BF16))
            part = w[0]
            for g in range(1, w.shape[0]):
                part = part + w[g]
            return part
        cnt = lax.fori_loop(0, nkc // 4,
                            lambda i, a: a + (chunk(4 * i) + chunk(4 * i + 1))
                            + (chunk(4 * i + 2) + chunk(4 * i + 3)),
                            jnp.zeros((2, BF16_ROWS, TQ), BF16))
        cnt = lax.fori_loop((nkc // 4) * 4, nkc, lambda c, a: a + chunk(c), cnt).astype(F32)
        tot = jnp.sum(cnt[0] + cnt[1], axis=0, keepdims=True)
        return jnp.broadcast_to(tot, (SUBLANES, TQ))

    def thr16(k16):
        f = _key16_to_float(k16)
        return jnp.concatenate([f] * (BF16_ROWS // SUBLANES), axis=0).astype(BF16)

    k16 = jnp.where(count_ge16(thr16(jnp.zeros((SUBLANES, TQ), jnp.int32))) >= n_keep,
                    jnp.int32(0), jnp.int32(-(2 ** 15)))

    def bit16_body(it, k):
        cand = k | jnp.left_shift(jnp.int32(1), 14 - it)
        return jnp.where(count_ge16(thr16(cand)) >= n_keep, cand, k)

    k16 = lax.fori_loop(0, 15, bit16_body, k16)
    hb_bits = lax.bitcast_convert_type(_key16_to_float(k16), jnp.int32)
    k_hb = hb_bits ^ ((hb_bits >> 31) & jnp.int32(0x7FFFFFFF))
    k_base = k_hb - WINDOW // 2
    k_base_wide = k_hb - WINDOW
    w_lo = jnp.maximum(_key_to_float(k_base), -FLT_MAX)
    w_hi = _key_to_float(k_base + WINDOW + 1)

    def gather_body(c, carry):
        m1, m2, m3, ovf, above = carry
        ks = pl.multiple_of(c * KC, KC)
        s = st_ref[pl.ds(ks, KC), :].reshape(KC // (SUBLANES * n_part), n_part, SUBLANES, TQ)
        for g in range(s.shape[0]):
            sg = s[g]
            ge_hi = sg >= w_hi[None]
            above = above + ge_hi.astype(jnp.int32)
            v = jnp.where(ge_hi, -jnp.inf, sg)
            t1 = jnp.minimum(m1, v)
            m1 = jnp.maximum(m1, v)
            t2 = jnp.minimum(m2, t1)
            m2 = jnp.maximum(m2, t1)
            ovf = jnp.maximum(ovf, jnp.minimum(m3, t2))
            m3 = jnp.maximum(m3, t2)
        return m1, m2, m3, ovf, above

    ninf = jnp.full((n_part, SUBLANES, TQ), -jnp.inf, F32)
    m1, m2, m3, ovf, above = lax.fori_loop(
        0, nkc, gather_body, (ninf, ninf, ninf, ninf, jnp.zeros((n_part, SUBLANES, TQ), jnp.int32)))
    gathered = jnp.concatenate([m1, m2, m3], axis=0)
    above8 = jnp.sum(above, axis=0)
    above8 = jnp.broadcast_to(jnp.sum(above8, axis=0, keepdims=True), (SUBLANES, TQ))

    def count_ge_gathered(thr8):
        c8 = jnp.sum((gathered >= thr8[None]).astype(jnp.int32), axis=0)
        return above8 + jnp.broadcast_to(jnp.sum(c8, axis=0, keepdims=True), (SUBLANES, TQ))

    def refine(count_fn, base, unroll=False):
        cnt0 = count_fn(_key_to_float(base))
        def body(it, carry):
            off, cnt_lo = carry
            cand = off | jnp.left_shift(jnp.int32(1), WINDOW_BITS - it)
            cnt = count_fn(_key_to_float(base + cand))
            ok = cnt >= n_keep
            return jnp.where(ok, cand, off), jnp.where(ok, cnt, cnt_lo)
        carry = (jnp.zeros((SUBLANES, TQ), jnp.int32), cnt0)
        if unroll:
            for it in range(WINDOW_BITS + 1):
                carry = body(it, carry)
        else:
            carry = lax.fori_loop(0, WINDOW_BITS + 1, body, carry)
        key_ref[0] = base + carry[0]
        key_ref[1] = carry[1]

    kp0 = kp_ref[0, 0:KC, :]
    for hd in range(N_HEADS):
        lga_ref[hd] = jnp.dot(kp0, qpt_ref[0, hd], preferred_element_type=F32)
    refine(count_ge_gathered, k_base, unroll=True)

    dropped = jnp.max(jnp.where(ovf >= w_lo[None], 1, 0))
    in_window = count_ge_gathered(w_lo)
    outside = jnp.where(in_window < n_keep, 1, jnp.where(above8 >= n_keep, 1, 0))
    outside = jnp.where(w_lo > -FLT_MAX, outside, 0)
    gathered_ok = (dropped == 0) & (jnp.max(outside) == 0)

    @pl.when(jnp.logical_not(gathered_ok))
    def _():
        refine(count_ge, k_base_wide)

    key = key_ref[0]
    cnt_lo = key_ref[1]
    lo8 = _key_to_float(key)
    hi8 = _key_to_float(key + 1)
    short = lo8 == -jnp.inf
    lo8 = jnp.where(short, -FLT_MAX, lo8)
    lo = lo8[0:1]

    has_tie = jnp.max(jnp.where(short, n_keep, cnt_lo)) > n_keep

    @pl.when(jnp.logical_not(has_tie))
    def _():
        def body(c, carry):
            ks = pl.multiple_of(c * KC, KC)
            bias_ref[pl.ds(ks, KC), :] = jnp.where(st_ref[pl.ds(ks, KC), :] >= lo, 0.0, NEG)
            return carry
        lax.fori_loop(0, nkc, body, 0)

    @pl.when(has_tie)
    def _():
        hi = hi8[0:1]
        key_ref[1] = count_ge_gathered(hi8)

        @pl.when(jnp.logical_not(gathered_ok))
        def _():
            key_ref[1] = count_ge(hi8)

        need = (n_keep - key_ref[1]).astype(F32)[0:1]
        tri = (lax.broadcasted_iota(jnp.int32, (KC, KC), 0)
               >= lax.broadcasted_iota(jnp.int32, (KC, KC), 1)).astype(BF16)

        def tie_chunk(c, taken):
            ks = pl.multiple_of(c * KC, KC)
            s = st_ref[pl.ds(ks, KC), :]
            ge_lo = s >= lo
            ge_hi = s >= hi
            tied = jnp.where(ge_lo, jnp.where(ge_hi, 0.0, 1.0), 0.0).astype(BF16)
            rank = taken + jnp.dot(tri, tied, preferred_element_type=F32)
            bias_ref[pl.ds(ks, KC), :] = jnp.where(
                ge_lo, jnp.where(ge_hi, 0.0, jnp.where(rank <= need, 0.0, NEG)), NEG)
            return rank[KC - 1:KC, :]

        taken = lax.fori_loop(0, nkc // 2,
                              lambda i, t: tie_chunk(2 * i + 1, tie_chunk(2 * i, t)),
                              jnp.zeros((1, TQ), F32))

        @pl.when(nkc % 2 == 1)
        def _():
            tie_chunk(nkc - 1, taken)

    for acc_ref in acc_refs:
        acc_ref[...] = jnp.zeros(acc_ref.shape, F32)
    ones_rows = (lax.broadcasted_iota(jnp.int32, (N_ACC_ROWS - HEAD_DIM, KC), 0) == 0).astype(BF16)

    for m_ref in m_refs:
        m_ref[...] = jnp.full(m_ref.shape, NEG, F32)
    buf_a = (lga_ref, mxa_refs)
    buf_b = (lgb_ref, mxb_refs)

    def heads_step(c_qk, qk_buf, c_pv, pv_buf, raw_ready=False):
        if c_qk is not None:
            ks = pl.multiple_of(c_qk * KC, KC)
            kpc = kp_ref[0, pl.ds(ks, KC), :]
            bias = bias_ref[pl.ds(ks, KC), :]
        if c_pv is not None:
            vtc = jnp.concatenate([vt_ref[0, c_pv], ones_rows], axis=0)
        for hd in range(N_HEADS):
            if c_qk is not None:
                raw = qk_buf[0][hd] if raw_ready else jnp.dot(kpc, qpt_ref[0, hd],
                                                                preferred_element_type=F32)
                lg = raw + bias
                qk_buf[0][hd] = lg
                qk_buf[1][hd][0:1, :] = jnp.max(lg, axis=0, keepdims=True)
            if c_pv is not None:
                m_old = m_refs[hd][0:1, :]
                m_new = jnp.maximum(m_old, pv_buf[1][hd][0:1, :])
                alpha = jnp.exp2(m_old - m_new)
                p = jnp.exp2(pv_buf[0][hd] - m_new).astype(BF16)
                acc_refs[hd][...] = alpha * acc_refs[hd][...] + jnp.dot(
                    vtc, p, preferred_element_type=F32)
                m_refs[hd][0:1, :] = m_new

    n_steps = nkc - 1
    heads_step(0, buf_a, None, None, raw_ready=True)

    def quad_body(i, carry):
        heads_step(4 * i + 1, buf_b, 4 * i, buf_a)
        heads_step(4 * i + 2, buf_a, 4 * i + 1, buf_b)
        heads_step(4 * i + 3, buf_b, 4 * i + 2, buf_a)
        heads_step(4 * i + 4, buf_a, 4 * i + 3, buf_b)
        return carry

    lax.fori_loop(0, n_steps // 4, quad_body, 0)
    done = (n_steps // 4) * 4
    rem = n_steps - done

    @pl.when(rem == 0)
    def _():
        heads_step(None, None, done, buf_a)

    @pl.when(rem == 1)
    def _():
        heads_step(done + 1, buf_b, done, buf_a)
        heads_step(None, None, done + 1, buf_b)

    @pl.when(rem == 2)
    def _():
        heads_step(done + 1, buf_b, done, buf_a)
        heads_step(done + 2, buf_a, done + 1, buf_b)
        heads_step(None, None, done + 2, buf_a)

    @pl.when(rem == 3)
    def _():
        heads_step(done + 1, buf_b, done, buf_a)
        heads_step(done + 2, buf_a, done + 1, buf_b)
        heads_step(done + 3, buf_b, done + 2, buf_a)
        heads_step(None, None, done + 3, buf_b)

    for hd in range(N_HEADS):
        acc = acc_refs[hd][...]
        ot_ref[hd * HEAD_DIM:(hd + 1) * HEAD_DIM, :] = acc[0:HEAD_DIM] / acc[HEAD_DIM:HEAD_DIM + 1]
    yb_ref[0] = (ot_ref[...].T * saz_ref[0]).astype(BF16)


def _post_kernel(x_ref, ya_ref, yb_ref, ga_ref, gb_ref, p_ref,
                 wa_ref, wb_ref, wo_ref, pg_ref, wpg_ref, wpp_ref, fg_ref, out_ref, *, final_norm):
    merged = (ga_ref[0] * jnp.dot(ya_ref[0], wa_ref[...], preferred_element_type=F32)
              + gb_ref[0] * jnp.dot(yb_ref[0], wb_ref[...], preferred_element_type=F32))
    x1 = x_ref[0] + jnp.dot(merged.astype(BF16), wo_ref[...], preferred_element_type=F32)
    gate = _sigmoid(jnp.dot(_rms(x1, pg_ref[...]).astype(BF16), wpg_ref[...],
                            preferred_element_type=F32))
    x2 = x1 + gate * jnp.dot(p_ref[0].astype(BF16), wpp_ref[...], preferred_element_type=F32)
    out_ref[0] = _rms(x2, fg_ref[...]) if final_norm else x2


def _pack_w_in(w):
    o_in, o_z, o_q, o_k, o_v, o_az, o_iq, o_ik, o_iw, o_ga, o_gb = (
        0, 1024, 1536, 2048, 2112, 2176, 2688, 3200, 3264, 3272, 4296)
    pad = jnp.zeros((D_MODEL, LANES - IDX_DIM - N_IDX_HEADS), w.dtype)
    cols = [w[:, o_in:o_z], w[:, o_z:o_q], w[:, o_az:o_iq],
            w[:, o_ga:o_gb], w[:, o_gb:o_gb + D_MODEL], w[:, o_k:o_az],
            w[:, o_ik:o_ga], pad]
    w_tok = jnp.concatenate(cols, axis=1).astype(BF16)
    rows = [w[:, o_q:o_k], w[:, o_iq:o_ik], w[:, o_v:o_az]]
    w_feat = jnp.concatenate(rows, axis=1).T.astype(BF16)
    return w_tok, w_feat


def _layer(x, p, norm_g, w_in, conv_w, conv_b, ln_g, ln_b, w_a, w_b, w_o,
           ple_g, w_pg, w_pp, out_g, final_norm):
    B, L, D = x.shape
    n_keep = min(TOPK_MAX, L // 4)
    cp = functools.partial(pltpu.CompilerParams, vmem_limit_bytes=VMEM_LIMIT)
    row = lambda a: a.reshape(1, -1)

    tm1 = TM_PROJ
    w_tok, w_feat = _pack_w_in(w_in)
    tok = lambda w: pl.BlockSpec((1, tm1, w), lambda b, i: (b, i, 0))
    featm = lambda r: pl.BlockSpec((1, N_HEADS, r, tm1), lambda b, i: (b, 0, 0, i))
    const = lambda a: pl.BlockSpec(a.shape, lambda b, i: (0,) * a.ndim, pipeline_mode=pl.Buffered(1))
    f32s = lambda w: jax.ShapeDtypeStruct((B, L, w), F32)
    consts1 = [row(norm_g), w_tok, w_feat, conv_w, row(conv_b), row(ln_g), row(ln_b)]
    ya, saz, ga, gb, qpt, kp, vt, iqt, ik, iwt = pl.pallas_call(
        functools.partial(_proj_kernel, tm=tm1),
        grid=(B, L // tm1),
        in_specs=[tok(D)] + [const(a) for a in consts1],
        out_specs=[tok(D_CONV), tok(D_ATTN), tok(D), tok(D),
                   featm(LANES), tok(LANES),
                   pl.BlockSpec((1, tm1 // KC, HEAD_DIM, KC), lambda b, i: (b, i, 0, 0)),
                   featm(IDX_DIM), tok(IDX_DIM),
                   pl.BlockSpec((1, N_IDX_HEADS, tm1), lambda b, i: (b, 0, i))],
        out_shape=[jax.ShapeDtypeStruct((B, L, D_CONV), BF16), f32s(D_ATTN), f32s(D), f32s(D),
                   jax.ShapeDtypeStruct((B, N_HEADS, LANES, L), BF16),
                   jax.ShapeDtypeStruct((B, L, LANES), BF16),
                   jax.ShapeDtypeStruct((B, L // KC, HEAD_DIM, KC), BF16),
                   jax.ShapeDtypeStruct((B, N_IDX_HEADS, IDX_DIM, L), BF16),
                   jax.ShapeDtypeStruct((B, L, IDX_DIM), BF16),
                   jax.ShapeDtypeStruct((B, N_IDX_HEADS, L), F32)],
        scratch_shapes=[pltpu.VMEM((HALO + tm1, D_CONV), F32),
                        pltpu.VMEM((SUBLANES - 1, HALO + tm1 - SUBLANES, D_CONV), F32),
                        pltpu.VMEM((HALO, D_CONV), F32)],
        compiler_params=cp(dimension_semantics=("arbitrary", "arbitrary")),
        name="proj",
    )(x, *consts1)

    yb = pl.pallas_call(
        functools.partial(_attn_kernel, n_keep=n_keep),
        grid=(B, L // TQ),
        in_specs=[pl.BlockSpec((1, N_IDX_HEADS, IDX_DIM, TQ), lambda b, j: (b, 0, 0, j)),
                  pl.BlockSpec((1, L, IDX_DIM), lambda b, j: (b, 0, 0)),
                  pl.BlockSpec((1, N_IDX_HEADS, TQ), lambda b, j: (b, 0, j)),
                  pl.BlockSpec((1, N_HEADS, LANES, TQ), lambda b, j: (b, 0, 0, j)),
                  pl.BlockSpec((1, L, LANES), lambda b, j: (b, 0, 0)),
                  pl.BlockSpec((1, L // KC, HEAD_DIM, KC), lambda b, j: (b, 0, 0, 0)),
                  pl.BlockSpec((1, TQ, D_ATTN), lambda b, j: (b, j, 0))],
        out_specs=pl.BlockSpec((1, TQ, D_ATTN), lambda b, j: (b, j, 0)),
        out_shape=jax.ShapeDtypeStruct((B, L, D_ATTN), BF16),
        scratch_shapes=[pltpu.VMEM((L, TQ), F32),
                        pltpu.VMEM((L, TQ), BF16),
                        pltpu.VMEM((2, SUBLANES, TQ), jnp.int32),
                        pltpu.VMEM((L, TQ), F32),
                        pltpu.VMEM((D_ATTN, TQ), F32),
                        pltpu.VMEM((N_HEADS, KC, TQ), F32),
                        pltpu.VMEM((N_HEADS, KC, TQ), F32)]
                       + [pltpu.VMEM((SUBLANES, TQ), F32)] * (3 * N_HEADS)
                       + [pltpu.VMEM((N_ACC_ROWS, TQ), F32)] * N_HEADS,
        compiler_params=cp(dimension_semantics=("arbitrary", "arbitrary")),
        name="attn",
    )(iqt, ik, iwt, qpt, kp, vt, saz)

    tm3 = TM_POST
    tok = lambda w: pl.BlockSpec((1, tm3, w), lambda b, i: (b, i, 0))
    consts3 = [w_a.astype(BF16), w_b.astype(BF16), w_o.astype(BF16), row(ple_g),
               w_pg.astype(BF16), w_pp.astype(BF16), row(out_g)]
    return pl.pallas_call(
        functools.partial(_post_kernel, final_norm=final_norm),
        grid=(B, L // tm3),
        in_specs=[tok(D), tok(D_CONV), tok(D_ATTN), tok(D), tok(D), tok(D_PLE)]
                 + [const(a) for a in consts3],
        out_specs=tok(D),
        out_shape=jax.ShapeDtypeStruct((B, L, D), F32),
        compiler_params=cp(dimension_semantics=("arbitrary", "arbitrary")),
        name="post",
    )(x, ya, yb, ga, gb, p, *consts3)


def kernel(x, p, norm_g, w_in, conv_w, conv_b, conv_ln_g, conv_ln_b, w_a_out, w_b_out,
           w_o, ple_norm_g, w_ple_gate, w_ple_proj, final_g):
    depth = w_in.shape[0]
    for i in range(depth):
        x = _layer(x, p[i], norm_g[i], w_in[i], conv_w[i], conv_b[i], conv_ln_g[i],
                   conv_ln_b[i], w_a_out[i], w_b_out[i], w_o[i], ple_norm_g[i],
                   w_ple_gate[i], w_ple_proj[i], final_g, i == depth - 1)
    return x
```

```python
import functools
import math

import jax
import jax.numpy as jnp
import numpy as np
from jax import lax
from jax.experimental import pallas as pl
from jax.experimental.pallas import tpu as pltpu

D_MODEL = 1024
D_PLE = 256
D_CONV = 512
CONV_WIDTH = 31
N_HEADS = 8
HEAD_DIM = 64
D_ATTN = N_HEADS * HEAD_DIM
N_IDX_HEADS = 8
IDX_DIM = 64
TOPK_MAX = 256
EPS = 1e-6

LANES = 128
SUBLANES = 8
TQ = 256
KC = 256
assert TQ == KC
TM_PROJ = 512
TM_POST = 1024
HALO = 32
NEG = -1e30
FLT_MAX = float(np.finfo(np.float32).max)
FLT_TINY = float(np.finfo(np.float32).tiny)
BF16_ROWS = 16
WINDOW_BITS = 16
WINDOW = 2 ** WINDOW_BITS
LOG2E = math.log2(math.e)
POS_SHIFT = 6
POS_RADIX = 2 ** POS_SHIFT
N_ACC_ROWS = HEAD_DIM + BF16_ROWS
VMEM_LIMIT = 56 * 1024 * 1024

F32 = jnp.float32
BF16 = jnp.bfloat16

_C_VAL, _C_GATE, _C_Z, _C_AZ, _C_GA, _C_GB, _C_KV, _C_IKW, _C_END = (
    0, 512, 1024, 1536, 2048, 3072, 4096, 4224, 4352)
_R_Q, _R_IQ, _R_V, _R_END = 0, 512, 1024, 1088

_NT = (((1,), (1,)), ((), ()))


def _bf16_split3(a):
    a = np.float32(a)
    p1 = np.float32(a.astype(BF16))
    p2 = np.float32(np.float32(a - p1).astype(BF16))
    p3 = np.float32(a - p1 - p2)
    assert np.float32(np.float32(p3).astype(BF16)) == p3
    return float(p1), float(p2), float(p3)


def _sigmoid(x):
    return 1.0 / (1.0 + jnp.exp(-x))


def _silu(x):
    return x * _sigmoid(x)


def _rms(x, g):
    return x * lax.rsqrt(jnp.mean(x * x, axis=-1, keepdims=True) + EPS) * g


def _proj_kernel(x_ref, g_ref, w_ref, wt_ref, cw_ref, cb_ref, lng_ref, lnb_ref,
                 ya_ref, saz_ref, ga_ref, gb_ref,
                 qpt_ref, kp_ref, vt_ref, iqt_ref, ik_ref, iwt_ref,
                 ext_ref, es_ref, carry_ref, *, tm):
    i = pl.program_id(1)
    h = _rms(x_ref[0], g_ref[...]).astype(BF16)

    def mm(a, b):
        return jnp.dot(h, w_ref[:, a:b], preferred_element_type=F32)

    def mm_t(a, b):
        return lax.dot_general(wt_ref[a:b, :], h, _NT, preferred_element_type=F32)

    u = mm(_C_VAL, _C_GATE) * _sigmoid(mm(_C_GATE, _C_Z))
    prev = carry_ref[...]
    ext_ref[0:HALO, :] = jnp.where(i > 0, prev, jnp.zeros_like(prev))
    ext_ref[HALO:, :] = u
    carry_ref[...] = u[tm - HALO:, :]
    n_sh = HALO + tm - SUBLANES
    for r in range(1, SUBLANES):
        es_ref[r - 1] = ext_ref[r:r + n_sh, :]

    def w_conformer():
        base = HALO - (CONV_WIDTH - 1)
        c = jnp.zeros((tm, D_CONV), F32)
        for jj in range(CONV_WIDTH):
            row8, r = divmod(base + jj, SUBLANES)
            row = row8 * SUBLANES
            tap = ext_ref[row:row + tm, :] if r == 0 else es_ref[r - 1, row:row + tm, :]
            c = c + cw_ref[jj:jj + 1, :] * tap
        c = c + cb_ref[...]
        mu = jnp.mean(c, axis=-1, keepdims=True)
        cc = c - mu
        var = jnp.mean(cc * cc, axis=-1, keepdims=True)
        ln = cc * lax.rsqrt(var + EPS) * lng_ref[...] + lnb_ref[...]
        ya_ref[0] = (_silu(ln) * _silu(mm(_C_Z, _C_AZ))).astype(BF16)

    def w_saz():
        saz_ref[0] = _silu(mm(_C_AZ, _C_GA))

    def w_gate(ref, c0, half):
        def run():
            lo = half * (D_MODEL // 2)
            ref[0, :, lo:lo + D_MODEL // 2] = _sigmoid(mm(c0 + lo, c0 + lo + D_MODEL // 2))
        return run

    def w_keys():
        lane = lax.broadcasted_iota(jnp.int32, (tm, LANES), 1)
        pos = i * tm + lax.broadcasted_iota(jnp.int32, (tm, LANES), 0)
        col = lane - HEAD_DIM
        kext = jnp.where(col < 3, 1.0,
               jnp.where(col < 6, (pos >> POS_SHIFT).astype(F32),
               jnp.where(col < 9, (pos & (POS_RADIX - 1)).astype(F32), 0.0)))
        kp_ref[0] = jnp.where(col < 0, mm(_C_KV, _C_IKW), kext).astype(BF16)
        ikw = mm(_C_IKW, _C_END)
        ik_ref[0] = ikw[:, :IDX_DIM].astype(BF16)
        iwt_ref[0] = ikw.T[IDX_DIM:IDX_DIM + N_IDX_HEADS, :] * (N_IDX_HEADS ** -0.5)

    def w_queries():
        sub = lax.broadcasted_iota(jnp.int32, (BF16_ROWS, tm), 0)
        tpos = (i * tm + lax.broadcasted_iota(jnp.int32, (BF16_ROWS, tm), 1)).astype(F32)
        qt = mm_t(_R_Q, _R_IQ) * (LOG2E * HEAD_DIM ** -0.5)
        for hd in range(N_HEADS):
            a = np.float32(2.0 ** (-8.0 * (hd + 1) / N_HEADS) * LOG2E)
            a1, a2, a3 = _bf16_split3(a)
            v = tpos * float(-a)
            v1 = v.astype(BF16).astype(F32)
            v2 = (v - v1).astype(BF16).astype(F32)
            v3 = v - v1 - v2
            ext = jnp.where(sub == 0, v1, jnp.where(sub == 1, v2, jnp.where(sub == 2, v3,
                  jnp.where(sub == 3, POS_RADIX * a1, jnp.where(sub == 4, POS_RADIX * a2,
                  jnp.where(sub == 5, POS_RADIX * a3, jnp.where(sub == 6, a1,
                  jnp.where(sub == 7, a2, jnp.where(sub == 8, a3, 0.0)))))))))
            qpt_ref[0, hd, 0:HEAD_DIM, :] = qt[hd * HEAD_DIM:(hd + 1) * HEAD_DIM, :].astype(BF16)
            qpt_ref[0, hd, HEAD_DIM:HEAD_DIM + BF16_ROWS, :] = ext.astype(BF16)
            qpt_ref[0, hd, HEAD_DIM + BF16_ROWS:, :] = jnp.zeros((LANES - HEAD_DIM - BF16_ROWS, tm), BF16)

    def w_index_values():
        iqt = mm_t(_R_IQ, _R_V) * (IDX_DIM ** -0.5)
        for hd in range(N_IDX_HEADS):
            iqt_ref[0, hd] = iqt[hd * IDX_DIM:(hd + 1) * IDX_DIM, :].astype(BF16)
        vt = mm_t(_R_V, _R_END).astype(BF16)
        for c in range(tm // KC):
            vt_ref[0, c] = vt[:, c * KC:(c + 1) * KC]

    for work in (w_conformer, w_saz, w_gate(ga_ref, _C_GA, 0), w_gate(ga_ref, _C_GA, 1),
                 w_gate(gb_ref, _C_GB, 0), w_gate(gb_ref, _C_GB, 1), w_keys, w_queries,
                 w_index_values):
        work()


def _threshold_of_key(k, bits, n_subnormal):
    f = lax.bitcast_convert_type(bits, F32)
    pos = jnp.where(k < n_subnormal, jnp.where(k == 0, 0.0, FLT_TINY), f)
    neg = jnp.where(k >= -n_subnormal, 0.0, jnp.where(f != f, -jnp.inf, f))
    return jnp.where(k >= 0, pos, neg)


def _key_to_float(k):
    return _threshold_of_key(k, jnp.where(k >= 0, k, k ^ jnp.int32(0x7FFFFFFF)), 2 ** 23)


def _key16_to_float(k):
    return _threshold_of_key(k, jnp.left_shift(jnp.where(k >= 0, k, k ^ jnp.int32(0x7FFF)), 16), 2 ** 7)


def _attn_kernel(iqt_ref, ik_ref, iwt_ref, qpt_ref, kp_ref, vt_ref, saz_ref, yb_ref,
                 st_ref, rt_ref, key_ref, bias_ref, ot_ref, lga_ref, lgb_ref, *head_refs, n_keep):
    m_refs, mxa_refs, mxb_refs, acc_refs = (head_refs[g * N_HEADS:(g + 1) * N_HEADS] for g in range(4))
    j = pl.program_id(1)
    t0 = j * TQ
    nkc = j + 1

    tcol = t0 + lax.broadcasted_iota(jnp.int32, (KC, TQ), 1)
    srow0 = lax.broadcasted_iota(jnp.int32, (KC, TQ), 0)

    def score_chunk(c):
        ks = pl.multiple_of(c * KC, KC)
        kic = ik_ref[0, pl.ds(ks, KC), :]
        acc = jnp.zeros((KC, TQ), F32)
        for hd in range(N_IDX_HEADS):
            d = jnp.dot(kic, iqt_ref[0, hd], preferred_element_type=F32)
            acc = acc + iwt_ref[0, hd:hd + 1, :] * jnp.maximum(d, 0.0)
        masked = jnp.where((ks + srow0) <= tcol, acc, -jnp.inf)
        st_ref[pl.ds(ks, KC), :] = masked
        rt_ref[pl.ds(ks, KC), :] = masked.astype(BF16)

    def score_pair(i, carry):
        score_chunk(2 * i)
        score_chunk(2 * i + 1)
        return carry

    lax.fori_loop(0, nkc // 2, score_pair, 0)

    @pl.when(nkc % 2 == 1)
    def _():
        score_chunk(nkc - 1)

    n_part = 4

    def count_ge(thr8):
        def body(c, cnt):
            ks = pl.multiple_of(c * KC, KC)
            s = st_ref[pl.ds(ks, KC), :].reshape(KC // (SUBLANES * n_part), n_part, SUBLANES, TQ)
            return cnt + jnp.sum((s >= thr8[None, None]).astype(jnp.int32), axis=0)
        cnt = lax.fori_loop(0, nkc, body, jnp.zeros((n_part, SUBLANES, TQ), jnp.int32))
        cnt8 = jnp.sum(cnt, axis=0)
        return jnp.broadcast_to(jnp.sum(cnt8, axis=0, keepdims=True), (SUBLANES, TQ))

    def count_ge16(thr_bf):
        def chunk(c):
            ks = pl.multiple_of(c * KC, KC)
            r = rt_ref[pl.ds(ks, KC), :].reshape(KC // (BF16_ROWS * 2), 2, BF16_ROWS, TQ)
            w = jnp.where(r >= thr_bf[None, None], jnp.ones((), BF16), jnp.zeros((), BF16))
            part = w[0]
            for g in range(1, w.shape[0]):
                part = part + w[g]
            return part
        cnt = lax.fori_loop(0, nkc // 4,
                            lambda i, a: a + (chunk(4 * i) + chunk(4 * i + 1))
                            + (chunk(4 * i + 2) + chunk(4 * i + 3)),
                            jnp.zeros((2, BF16_ROWS, TQ), BF16))
        cnt = lax.fori_loop((nkc // 4) * 4, nkc, lambda c, a: a + chunk(c), cnt).astype(F32)
        tot = jnp.sum(cnt[0] + cnt[1], axis=0, keepdims=True)
        return jnp.broadcast_to(tot, (SUBLANES, TQ))

    def thr16(k16):
        f = _key16_to_float(k16)
        return jnp.concatenate([f] * (BF16_ROWS // SUBLANES), axis=0).astype(BF16)

    few_keys = nkc * KC <= n_keep
    k16 = jnp.where(count_ge16(thr16(jnp.zeros((SUBLANES, TQ), jnp.int32))) >= n_keep,
                    jnp.int32(0), jnp.int32(-(2 ** 15)))
    k16 = jnp.where(few_keys, jnp.int32(-(2 ** 15)), k16)

    def bit16_body(it, k):
        cand = k | jnp.left_shift(jnp.int32(1), 14 - it)
        return jnp.where(count_ge16(thr16(cand)) >= n_keep, cand, k)

    k16 = lax.fori_loop(0, jnp.where(few_keys, 0, 15), bit16_body, k16)
    hb_bits = lax.bitcast_convert_type(_key16_to_float(k16), jnp.int32)
    k_hb = hb_bits ^ ((hb_bits >> 31) & jnp.int32(0x7FFFFFFF))
    k_base = k_hb - WINDOW // 2
    k_base_wide = k_hb - WINDOW
    w_lo = jnp.maximum(_key_to_float(k_base), -FLT_MAX)
    w_hi = _key_to_float(k_base + WINDOW + 1)

    def gather_body(c, carry):
        m1, m2, m3, ovf, above = carry
        ks = pl.multiple_of(c * KC, KC)
        s = st_ref[pl.ds(ks, KC), :].reshape(KC // (SUBLANES * n_part), n_part, SUBLANES, TQ)
        for g in range(s.shape[0]):
            sg = s[g]
            ge_hi = sg >= w_hi[None]
            above = above + ge_hi.astype(jnp.int32)
            v = jnp.where(ge_hi, -jnp.inf, sg)
            t1 = jnp.minimum(m1, v)
            m1 = jnp.maximum(m1, v)
            t2 = jnp.minimum(m2, t1)
            m2 = jnp.maximum(m2, t1)
            ovf = jnp.maximum(ovf, jnp.minimum(m3, t2))
            m3 = jnp.maximum(m3, t2)
        return m1, m2, m3, ovf, above

    ninf = jnp.full((n_part, SUBLANES, TQ), -jnp.inf, F32)
    m1, m2, m3, ovf, above = lax.fori_loop(
        0, nkc, gather_body, (ninf, ninf, ninf, ninf, jnp.zeros((n_part, SUBLANES, TQ), jnp.int32)))
    gathered = jnp.concatenate([m1, m2, m3], axis=0)
    above8 = jnp.sum(above, axis=0)
    above8 = jnp.broadcast_to(jnp.sum(above8, axis=0, keepdims=True), (SUBLANES, TQ))

    def count_ge_gathered(thr8):
        c8 = jnp.sum((gathered >= thr8[None]).astype(jnp.int32), axis=0)
        return above8 + jnp.broadcast_to(jnp.sum(c8, axis=0, keepdims=True), (SUBLANES, TQ))

    def refine(count_fn, base, unroll=False):
        cnt0 = count_fn(_key_to_float(base))
        def body(it, carry):
            off, cnt_lo = carry
            cand = off | jnp.left_shift(jnp.int32(1), WINDOW_BITS - it)
            cnt = count_fn(_key_to_float(base + cand))
            ok = cnt >= n_keep
            return jnp.where(ok, cand, off), jnp.where(ok, cnt, cnt_lo)
        carry = (jnp.zeros((SUBLANES, TQ), jnp.int32), cnt0)
        if unroll:
            for it in range(WINDOW_BITS + 1):
                carry = body(it, carry)
        else:
            carry = lax.fori_loop(0, WINDOW_BITS + 1, body, carry)
        key_ref[0] = base + carry[0]
        key_ref[1] = carry[1]

    kp0 = kp_ref[0, 0:KC, :]
    for hd in range(N_HEADS):
        lga_ref[hd] = jnp.dot(kp0, qpt_ref[0, hd], preferred_element_type=F32)
    refine(count_ge_gathered, k_base, unroll=True)

    dropped = jnp.max(jnp.where(ovf >= w_lo[None], 1, 0))
    in_window = count_ge_gathered(w_lo)
    outside = jnp.where(in_window < n_keep, 1, jnp.where(above8 >= n_keep, 1, 0))
    outside = jnp.where(w_lo > -FLT_MAX, outside, 0)
    gathered_ok = (dropped == 0) & (jnp.max(outside) == 0)

    @pl.when(jnp.logical_not(gathered_ok))
    def _():
        refine(count_ge, k_base_wide)

    key = key_ref[0]
    cnt_lo = key_ref[1]
    lo8 = _key_to_float(key)
    hi8 = _key_to_float(key + 1)
    short = lo8 == -jnp.inf
    lo8 = jnp.where(short, -FLT_MAX, lo8)
    lo = lo8[0:1]

    has_tie = jnp.max(jnp.where(short, n_keep, cnt_lo)) > n_keep

    @pl.when(jnp.logical_not(has_tie))
    def _():
        def body(c, carry):
            ks = pl.multiple_of(c * KC, KC)
            bias_ref[pl.ds(ks, KC), :] = jnp.where(st_ref[pl.ds(ks, KC), :] >= lo, 0.0, NEG)
            return carry
        lax.fori_loop(0, nkc, body, 0)

    @pl.when(has_tie)
    def _():
        hi = hi8[0:1]
        key_ref[1] = count_ge_gathered(hi8)

        @pl.when(jnp.logical_not(gathered_ok))
        def _():
            key_ref[1] = count_ge(hi8)

        need = (n_keep - key_ref[1]).astype(F32)[0:1]
        tri = (lax.broadcasted_iota(jnp.int32, (KC, KC), 0)
               >= lax.broadcasted_iota(jnp.int32, (KC, KC), 1)).astype(BF16)

        def tie_chunk(c, taken):
            ks = pl.multiple_of(c * KC, KC)
            s = st_ref[pl.ds(ks, KC), :]
            ge_lo = s >= lo
            ge_hi = s >= hi
            tied = jnp.where(ge_lo, jnp.where(ge_hi, 0.0, 1.0), 0.0).astype(BF16)
            rank = taken + jnp.dot(tri, tied, preferred_element_type=F32)
            bias_ref[pl.ds(ks, KC), :] = jnp.where(
                ge_lo, jnp.where(ge_hi, 0.0, jnp.where(rank <= need, 0.0, NEG)), NEG)
            return rank[KC - 1:KC, :]

        taken = lax.fori_loop(0, nkc // 2,
                              lambda i, t: tie_chunk(2 * i + 1, tie_chunk(2 * i, t)),
                              jnp.zeros((1, TQ), F32))

        @pl.when(nkc % 2 == 1)
        def _():
            tie_chunk(nkc - 1, taken)

    for acc_ref in acc_refs:
        acc_ref[...] = jnp.zeros(acc_ref.shape, F32)
    ones_rows = (lax.broadcasted_iota(jnp.int32, (N_ACC_ROWS - HEAD_DIM, KC), 0) == 0).astype(BF16)

    for m_ref in m_refs:
        m_ref[...] = jnp.full(m_ref.shape, NEG, F32)
    buf_a = (lga_ref, mxa_refs)
    buf_b = (lgb_ref, mxb_refs)

    def heads_step(c_qk, qk_buf, c_pv, pv_buf, raw_ready=False):
        if c_qk is not None:
            ks = pl.multiple_of(c_qk * KC, KC)
            kpc = kp_ref[0, pl.ds(ks, KC), :]
            bias = bias_ref[pl.ds(ks, KC), :]
        if c_pv is not None:
            vtc = jnp.concatenate([vt_ref[0, c_pv], ones_rows], axis=0)
        for hd in range(N_HEADS):
            if c_qk is not None:
                raw = qk_buf[0][hd] if raw_ready else jnp.dot(kpc, qpt_ref[0, hd],
                                                                preferred_element_type=F32)
                lg = raw + bias
                qk_buf[0][hd] = lg
                qk_buf[1][hd][0:1, :] = jnp.max(lg, axis=0, keepdims=True)
            if c_pv is not None:
                m_old = m_refs[hd][0:1, :]
                m_new = jnp.maximum(m_old, pv_buf[1][hd][0:1, :])
                alpha = jnp.exp2(m_old - m_new)
                p = jnp.exp2(pv_buf[0][hd] - m_new).astype(BF16)
                acc_refs[hd][...] = alpha * acc_refs[hd][...] + jnp.dot(
                    vtc, p, preferred_element_type=F32)
                m_refs[hd][0:1, :] = m_new

    n_steps = nkc - 1
    heads_step(0, buf_a, None, None, raw_ready=True)

    def quad_body(i, carry):
        heads_step(4 * i + 1, buf_b, 4 * i, buf_a)
        heads_step(4 * i + 2, buf_a, 4 * i + 1, buf_b)
        heads_step(4 * i + 3, buf_b, 4 * i + 2, buf_a)
        heads_step(4 * i + 4, buf_a, 4 * i + 3, buf_b)
        return carry

    lax.fori_loop(0, n_steps // 4, quad_body, 0)
    done = (n_steps // 4) * 4
    rem = n_steps - done

    @pl.when(rem == 0)
    def _():
        heads_step(None, None, done, buf_a)

    @pl.when(rem == 1)
    def _():
        heads_step(done + 1, buf_b, done, buf_a)
        heads_step(None, None, done + 1, buf_b)

    @pl.when(rem == 2)
    def _():
        heads_step(done + 1, buf_b, done, buf_a)
        heads_step(done + 2, buf_a, done + 1, buf_b)
        heads_step(None, None, done + 2, buf_a)

    @pl.when(rem == 3)
    def _():
        heads_step(done + 1, buf_b, done, buf_a)
        heads_step(done + 2, buf_a, done + 1, buf_b)
        heads_step(done + 3, buf_b, done + 2, buf_a)
        heads_step(None, None, done + 3, buf_b)

    for hd in range(N_HEADS):
        acc = acc_refs[hd][...]
        ot_ref[hd * HEAD_DIM:(hd + 1) * HEAD_DIM, :] = acc[0:HEAD_DIM] / acc[HEAD_DIM:HEAD_DIM + 1]
    yb_ref[0] = (ot_ref[...].T * saz_ref[0]).astype(BF16)


def _post_kernel(x_ref, ya_ref, yb_ref, ga_ref, gb_ref, p_ref,
                 wa_ref, wb_ref, wo_ref, pg_ref, wpg_ref, wpp_ref, fg_ref, out_ref, *, final_norm):
    merged = (ga_ref[0] * jnp.dot(ya_ref[0], wa_ref[...], preferred_element_type=F32)
              + gb_ref[0] * jnp.dot(yb_ref[0], wb_ref[...], preferred_element_type=F32))
    x1 = x_ref[0] + jnp.dot(merged.astype(BF16), wo_ref[...], preferred_element_type=F32)
    gate = _sigmoid(jnp.dot(_rms(x1, pg_ref[...]).astype(BF16), wpg_ref[...],
                            preferred_element_type=F32))
    x2 = x1 + gate * jnp.dot(p_ref[0].astype(BF16), wpp_ref[...], preferred_element_type=F32)
    out_ref[0] = _rms(x2, fg_ref[...]) if final_norm else x2


def _pack_w_in(w):
    o_in, o_z, o_q, o_k, o_v, o_az, o_iq, o_ik, o_iw, o_ga, o_gb = (
        0, 1024, 1536, 2048, 2112, 2176, 2688, 3200, 3264, 3272, 4296)
    pad = jnp.zeros((D_MODEL, LANES - IDX_DIM - N_IDX_HEADS), w.dtype)
    cols = [w[:, o_in:o_z], w[:, o_z:o_q], w[:, o_az:o_iq],
            w[:, o_ga:o_gb], w[:, o_gb:o_gb + D_MODEL], w[:, o_k:o_az],
            w[:, o_ik:o_ga], pad]
    w_tok = jnp.concatenate(cols, axis=1).astype(BF16)
    rows = [w[:, o_q:o_k], w[:, o_iq:o_ik], w[:, o_v:o_az]]
    w_feat = jnp.concatenate(rows, axis=1).T.astype(BF16)
    return w_tok, w_feat


def _layer(x, p, norm_g, w_in, conv_w, conv_b, ln_g, ln_b, w_a, w_b, w_o,
           ple_g, w_pg, w_pp, out_g, final_norm):
    B, L, D = x.shape
    n_keep = min(TOPK_MAX, L // 4)
    cp = functools.partial(pltpu.CompilerParams, vmem_limit_bytes=VMEM_LIMIT)
    row = lambda a: a.reshape(1, -1)

    tm1 = TM_PROJ
    w_tok, w_feat = _pack_w_in(w_in)
    tok = lambda w: pl.BlockSpec((1, tm1, w), lambda b, i: (b, i, 0))
    featm = lambda r: pl.BlockSpec((1, N_HEADS, r, tm1), lambda b, i: (b, 0, 0, i))
    const = lambda a: pl.BlockSpec(a.shape, lambda b, i: (0,) * a.ndim, pipeline_mode=pl.Buffered(1))
    f32s = lambda w: jax.ShapeDtypeStruct((B, L, w), F32)
    consts1 = [row(norm_g), w_tok, w_feat, conv_w, row(conv_b), row(ln_g), row(ln_b)]
    ya, saz, ga, gb, qpt, kp, vt, iqt, ik, iwt = pl.pallas_call(
        functools.partial(_proj_kernel, tm=tm1),
        grid=(B, L // tm1),
        in_specs=[tok(D)] + [const(a) for a in consts1],
        out_specs=[tok(D_CONV), tok(D_ATTN), tok(D), tok(D),
                   featm(LANES), tok(LANES),
                   pl.BlockSpec((1, tm1 // KC, HEAD_DIM, KC), lambda b, i: (b, i, 0, 0)),
                   featm(IDX_DIM), tok(IDX_DIM),
                   pl.BlockSpec((1, N_IDX_HEADS, tm1), lambda b, i: (b, 0, i))],
        out_shape=[jax.ShapeDtypeStruct((B, L, D_CONV), BF16), f32s(D_ATTN), f32s(D), f32s(D),
                   jax.ShapeDtypeStruct((B, N_HEADS, LANES, L), BF16),
                   jax.ShapeDtypeStruct((B, L, LANES), BF16),
                   jax.ShapeDtypeStruct((B, L // KC, HEAD_DIM, KC), BF16),
                   jax.ShapeDtypeStruct((B, N_IDX_HEADS, IDX_DIM, L), BF16),
                   jax.ShapeDtypeStruct((B, L, IDX_DIM), BF16),
                   jax.ShapeDtypeStruct((B, N_IDX_HEADS, L), F32)],
        scratch_shapes=[pltpu.VMEM((HALO + tm1, D_CONV), F32),
                        pltpu.VMEM((SUBLANES - 1, HALO + tm1 - SUBLANES, D_CONV), F32),
                        pltpu.VMEM((HALO, D_CONV), F32)],
        compiler_params=cp(dimension_semantics=("arbitrary", "arbitrary")),
        name="proj",
    )(x, *consts1)

    yb = pl.pallas_call(
        functools.partial(_attn_kernel, n_keep=n_keep),
        grid=(B, L // TQ),
        in_specs=[pl.BlockSpec((1, N_IDX_HEADS, IDX_DIM, TQ), lambda b, j: (b, 0, 0, j)),
                  pl.BlockSpec((1, L, IDX_DIM), lambda b, j: (b, 0, 0)),
                  pl.BlockSpec((1, N_IDX_HEADS, TQ), lambda b, j: (b, 0, j)),
                  pl.BlockSpec((1, N_HEADS, LANES, TQ), lambda b, j: (b, 0, 0, j)),
                  pl.BlockSpec((1, L, LANES), lambda b, j: (b, 0, 0)),
                  pl.BlockSpec((1, L // KC, HEAD_DIM, KC), lambda b, j: (b, 0, 0, 0)),
                  pl.BlockSpec((1, TQ, D_ATTN), lambda b, j: (b, j, 0))],
        out_specs=pl.BlockSpec((1, TQ, D_ATTN), lambda b, j: (b, j, 0)),
        out_shape=jax.ShapeDtypeStruct((B, L, D_ATTN), BF16),
        scratch_shapes=[pltpu.VMEM((L, TQ), F32),
                        pltpu.VMEM((L, TQ), BF16),
                        pltpu.VMEM((2, SUBLANES, TQ), jnp.int32),
                        pltpu.VMEM((L, TQ), F32),
                        pltpu.VMEM((D_ATTN, TQ), F32),
                        pltpu.VMEM((N_HEADS, KC, TQ), F32),
                        pltpu.VMEM((N_HEADS, KC, TQ), F32)]
                       + [pltpu.VMEM((SUBLANES, TQ), F32)] * (3 * N_HEADS)
                       + [pltpu.VMEM((N_ACC_ROWS, TQ), F32)] * N_HEADS,
        compiler_params=cp(dimension_semantics=("arbitrary", "arbitrary")),
        name="attn",
    )(iqt, ik, iwt, qpt, kp, vt, saz)

    tm3 = TM_POST
    tok = lambda w: pl.BlockSpec((1, tm3, w), lambda b, i: (b, i, 0))
    consts3 = [w_a.astype(BF16), w_b.astype(BF16), w_o.astype(BF16), row(ple_g),
               w_pg.astype(BF16), w_pp.astype(BF16), row(out_g)]
    return pl.pallas_call(
        functools.partial(_post_kernel, final_norm=final_norm),
        grid=(B, L // tm3),
        in_specs=[tok(D), tok(D_CONV), tok(D_ATTN), tok(D), tok(D), tok(D_PLE)]
                 + [const(a) for a in consts3],
        out_specs=tok(D),
        out_shape=jax.ShapeDtypeStruct((B, L, D), F32),
        compiler_params=cp(dimension_semantics=("arbitrary", "arbitrary")),
        name="post",
    )(x, ya, yb, ga, gb, p, *consts3)


def kernel(x, p, norm_g, w_in, conv_w, conv_b, conv_ln_g, conv_ln_b, w_a_out, w_b_out,
           w_o, ple_norm_g, w_ple_gate, w_ple_proj, final_g):
    depth = w_in.shape[0]
    for i in range(depth):
        x = _layer(x, p[i], norm_g[i], w_in[i], conv_w[i], conv_b[i], conv_ln_g[i],
                   conv_ln_b[i], w_a_out[i], w_b_out[i], w_o[i], ple_norm_g[i],
                   w_ple_gate[i], w_ple_proj[i], final_g, i == depth - 1)
    return x
```

```python
import functools
import math

import jax
import jax.numpy as jnp
import numpy as np
from jax import lax
from jax.experimental import pallas as pl
from jax.experimental.pallas import tpu as pltpu

D_MODEL = 1024
D_PLE = 256
D_CONV = 512
CONV_WIDTH = 31
N_HEADS = 8
HEAD_DIM = 64
D_ATTN = N_HEADS * HEAD_DIM
N_IDX_HEADS = 8
IDX_DIM = 64
TOPK_MAX = 256
EPS = 1e-6

LANES = 128
SUBLANES = 8
TQ = 256
KC = 256
assert TQ == KC
TM_PROJ = 512
TM_POST = 1024
HALO = 32
NEG = -1e30
FLT_MAX = float(np.finfo(np.float32).max)
FLT_TINY = float(np.finfo(np.float32).tiny)
BF16_ROWS = 16
WINDOW_BITS = 16
WINDOW = 2 ** WINDOW_BITS
KEY16_NEG_INF = -(2 ** 15) + 0x7F
LOG2E = math.log2(math.e)
POS_SHIFT = 6
POS_RADIX = 2 ** POS_SHIFT
N_ACC_ROWS = HEAD_DIM + BF16_ROWS
VMEM_LIMIT = 56 * 1024 * 1024

F32 = jnp.float32
BF16 = jnp.bfloat16

_C_VAL, _C_GATE, _C_Z, _C_AZ, _C_GA, _C_GB, _C_KV, _C_IKW, _C_END = (
    0, 512, 1024, 1536, 2048, 3072, 4096, 4224, 4352)
_R_Q, _R_IQ, _R_V, _R_END = 0, 512, 1024, 1088

_NT = (((1,), (1,)), ((), ()))


def _bf16_split3(a):
    a = np.float32(a)
    p1 = np.float32(a.astype(BF16))
    p2 = np.float32(np.float32(a - p1).astype(BF16))
    p3 = np.float32(a - p1 - p2)
    assert np.float32(np.float32(p3).astype(BF16)) == p3
    return float(p1), float(p2), float(p3)


def _sigmoid(x):
    return 1.0 / (1.0 + jnp.exp(-x))


def _silu(x):
    return x * _sigmoid(x)


def _rms(x, g):
    return x * lax.rsqrt(jnp.mean(x * x, axis=-1, keepdims=True) + EPS) * g


def _proj_kernel(x_ref, g_ref, w_ref, wt_ref, cw_ref, cb_ref, lng_ref, lnb_ref,
                 ya_ref, saz_ref, ga_ref, gb_ref,
                 qpt_ref, kp_ref, vt_ref, iqt_ref, ik_ref, iwt_ref,
                 ext_ref, es_ref, carry_ref, *, tm):
    i = pl.program_id(1)
    h = _rms(x_ref[0], g_ref[...]).astype(BF16)

    def mm(a, b):
        return jnp.dot(h, w_ref[:, a:b], preferred_element_type=F32)

    def mm_t(a, b):
        return lax.dot_general(wt_ref[a:b, :], h, _NT, preferred_element_type=F32)

    u = mm(_C_VAL, _C_GATE) * _sigmoid(mm(_C_GATE, _C_Z))
    prev = carry_ref[...]
    ext_ref[0:HALO, :] = jnp.where(i > 0, prev, jnp.zeros_like(prev))
    ext_ref[HALO:, :] = u
    carry_ref[...] = u[tm - HALO:, :]
    n_sh = HALO + tm - SUBLANES
    for r in range(1, SUBLANES):
        es_ref[r - 1] = ext_ref[r:r + n_sh, :]

    def w_conformer():
        base = HALO - (CONV_WIDTH - 1)
        c = jnp.zeros((tm, D_CONV), F32)
        for jj in range(CONV_WIDTH):
            row8, r = divmod(base + jj, SUBLANES)
            row = row8 * SUBLANES
            tap = ext_ref[row:row + tm, :] if r == 0 else es_ref[r - 1, row:row + tm, :]
            c = c + cw_ref[jj:jj + 1, :] * tap
        c = c + cb_ref[...]
        mu = jnp.mean(c, axis=-1, keepdims=True)
        cc = c - mu
        var = jnp.mean(cc * cc, axis=-1, keepdims=True)
        ln = cc * lax.rsqrt(var + EPS) * lng_ref[...] + lnb_ref[...]
        ya_ref[0] = (_silu(ln) * _silu(mm(_C_Z, _C_AZ))).astype(BF16)

    def w_saz():
        saz_ref[0] = _silu(mm(_C_AZ, _C_GA))

    def w_gate(ref, c0, half):
        def run():
            lo = half * (D_MODEL // 2)
            ref[0, :, lo:lo + D_MODEL // 2] = _sigmoid(mm(c0 + lo, c0 + lo + D_MODEL // 2))
        return run

    def w_keys():
        lane = lax.broadcasted_iota(jnp.int32, (tm, LANES), 1)
        pos = i * tm + lax.broadcasted_iota(jnp.int32, (tm, LANES), 0)
        col = lane - HEAD_DIM
        kext = jnp.where(col < 3, 1.0,
               jnp.where(col < 6, (pos >> POS_SHIFT).astype(F32),
               jnp.where(col < 9, (pos & (POS_RADIX - 1)).astype(F32), 0.0)))
        kp_ref[0] = jnp.where(col < 0, mm(_C_KV, _C_IKW), kext).astype(BF16)
        ikw = mm(_C_IKW, _C_END)
        ik_ref[0] = ikw[:, :IDX_DIM].astype(BF16)
        iwt_ref[0] = ikw.T[IDX_DIM:IDX_DIM + N_IDX_HEADS, :] * (N_IDX_HEADS ** -0.5)

    def w_queries():
        sub = lax.broadcasted_iota(jnp.int32, (BF16_ROWS, tm), 0)
        tpos = (i * tm + lax.broadcasted_iota(jnp.int32, (BF16_ROWS, tm), 1)).astype(F32)
        qt = mm_t(_R_Q, _R_IQ) * (LOG2E * HEAD_DIM ** -0.5)
        for hd in range(N_HEADS):
            a = np.float32(2.0 ** (-8.0 * (hd + 1) / N_HEADS) * LOG2E)
            a1, a2, a3 = _bf16_split3(a)
            v = tpos * float(-a)
            v1 = v.astype(BF16).astype(F32)
            v2 = (v - v1).astype(BF16).astype(F32)
            v3 = v - v1 - v2
            ext = jnp.where(sub == 0, v1, jnp.where(sub == 1, v2, jnp.where(sub == 2, v3,
                  jnp.where(sub == 3, POS_RADIX * a1, jnp.where(sub == 4, POS_RADIX * a2,
                  jnp.where(sub == 5, POS_RADIX * a3, jnp.where(sub == 6, a1,
                  jnp.where(sub == 7, a2, jnp.where(sub == 8, a3, 0.0)))))))))
            qpt_ref[0, hd, 0:HEAD_DIM, :] = qt[hd * HEAD_DIM:(hd + 1) * HEAD_DIM, :].astype(BF16)
            qpt_ref[0, hd, HEAD_DIM:HEAD_DIM + BF16_ROWS, :] = ext.astype(BF16)
            qpt_ref[0, hd, HEAD_DIM + BF16_ROWS:, :] = jnp.zeros((LANES - HEAD_DIM - BF16_ROWS, tm), BF16)

    def w_index_values():
        iqt = mm_t(_R_IQ, _R_V) * (IDX_DIM ** -0.5)
        for hd in range(N_IDX_HEADS):
            iqt_ref[0, hd] = iqt[hd * IDX_DIM:(hd + 1) * IDX_DIM, :].astype(BF16)
        vt = mm_t(_R_V, _R_END).astype(BF16)
        for c in range(tm // KC):
            vt_ref[0, c] = vt[:, c * KC:(c + 1) * KC]

    for work in (w_conformer, w_saz, w_gate(ga_ref, _C_GA, 0), w_gate(ga_ref, _C_GA, 1),
                 w_gate(gb_ref, _C_GB, 0), w_gate(gb_ref, _C_GB, 1), w_keys, w_queries,
                 w_index_values):
        work()


def _threshold_of_key(k, bits, n_subnormal):
    f = lax.bitcast_convert_type(bits, F32)
    pos = jnp.where(k < n_subnormal, jnp.where(k == 0, 0.0, FLT_TINY), f)
    neg = jnp.where(k >= -n_subnormal, 0.0, jnp.where(f != f, -jnp.inf, f))
    return jnp.where(k >= 0, pos, neg)


def _key_to_float(k):
    return _threshold_of_key(k, jnp.where(k >= 0, k, k ^ jnp.int32(0x7FFFFFFF)), 2 ** 23)


def _key16_to_float(k):
    return _threshold_of_key(k, jnp.left_shift(jnp.where(k >= 0, k, k ^ jnp.int32(0x7FFF)), 16), 2 ** 7)


def _attn_kernel(iqt_ref, ik_ref, iwt_ref, qpt_ref, kp_ref, vt_ref, saz_ref, yb_ref,
                 st_ref, rt_ref, key_ref, bias_ref, ot_ref, lga_ref, lgb_ref, *head_refs, n_keep):
    m_refs, mxa_refs, mxb_refs, acc_refs = (head_refs[g * N_HEADS:(g + 1) * N_HEADS] for g in range(4))
    j = pl.program_id(1)
    t0 = j * TQ
    nkc = j + 1

    tcol = t0 + lax.broadcasted_iota(jnp.int32, (KC, TQ), 1)
    srow0 = lax.broadcasted_iota(jnp.int32, (KC, TQ), 0)

    def score_chunk(c):
        ks = pl.multiple_of(c * KC, KC)
        kic = ik_ref[0, pl.ds(ks, KC), :]
        acc = jnp.zeros((KC, TQ), F32)
        for hd in range(N_IDX_HEADS):
            d = jnp.dot(kic, iqt_ref[0, hd], preferred_element_type=F32)
            acc = acc + iwt_ref[0, hd:hd + 1, :] * jnp.maximum(d, 0.0)
        masked = jnp.where((ks + srow0) <= tcol, acc, -jnp.inf)
        st_ref[pl.ds(ks, KC), :] = masked
        rt_ref[pl.ds(ks, KC), :] = masked.astype(BF16)

    def score_pair(i, carry):
        score_chunk(2 * i)
        score_chunk(2 * i + 1)
        return carry

    lax.fori_loop(0, nkc // 2, score_pair, 0)

    @pl.when(nkc % 2 == 1)
    def _():
        score_chunk(nkc - 1)

    n_part = 4

    def count_ge(thr8):
        def body(c, cnt):
            ks = pl.multiple_of(c * KC, KC)
            s = st_ref[pl.ds(ks, KC), :].reshape(KC // (SUBLANES * n_part), n_part, SUBLANES, TQ)
            return cnt + jnp.sum((s >= thr8[None, None]).astype(jnp.int32), axis=0)
        cnt = lax.fori_loop(0, nkc, body, jnp.zeros((n_part, SUBLANES, TQ), jnp.int32))
        cnt8 = jnp.sum(cnt, axis=0)
        return jnp.broadcast_to(jnp.sum(cnt8, axis=0, keepdims=True), (SUBLANES, TQ))

    def count_ge16(thr_bf):
        def chunk(c):
            ks = pl.multiple_of(c * KC, KC)
            r = rt_ref[pl.ds(ks, KC), :].reshape(KC // (BF16_ROWS * 2), 2, BF16_ROWS, TQ)
            w = jnp.where(r >= thr_bf[None, None], jnp.ones((), BF16), jnp.zeros((), BF16))
            part = w[0]
            for g in range(1, w.shape[0]):
                part = part + w[g]
            return part
        cnt = lax.fori_loop(0, nkc // 4,
                            lambda i, a: a + (chunk(4 * i) + chunk(4 * i + 1))
                            + (chunk(4 * i + 2) + chunk(4 * i + 3)),
                            jnp.zeros((2, BF16_ROWS, TQ), BF16))
        cnt = lax.fori_loop((nkc // 4) * 4, nkc, lambda c, a: a + chunk(c), cnt).astype(F32)
        tot = jnp.sum(cnt[0] + cnt[1], axis=0, keepdims=True)
        return jnp.broadcast_to(tot, (SUBLANES, TQ))

    def thr16(k16):
        f = _key16_to_float(k16)
        return jnp.concatenate([f] * (BF16_ROWS // SUBLANES), axis=0).astype(BF16)

    few_keys = nkc * KC <= n_keep
    k16 = jnp.where(count_ge16(thr16(jnp.zeros((SUBLANES, TQ), jnp.int32))) >= n_keep,
                    jnp.int32(0), jnp.int32(-(2 ** 15)))
    k16 = jnp.where(few_keys, jnp.int32(KEY16_NEG_INF), k16)

    def bit16_body(it, k):
        cand = k | jnp.left_shift(jnp.int32(1), 14 - it)
        return jnp.where(count_ge16(thr16(cand)) >= n_keep, cand, k)

    k16 = lax.fori_loop(0, jnp.where(few_keys, 0, 15), bit16_body, k16)
    hb_bits = lax.bitcast_convert_type(_key16_to_float(k16), jnp.int32)
    k_hb = hb_bits ^ ((hb_bits >> 31) & jnp.int32(0x7FFFFFFF))
    k_base = k_hb - WINDOW // 2
    k_base_wide = k_hb - WINDOW
    w_lo = jnp.maximum(_key_to_float(k_base), -FLT_MAX)
    w_hi = _key_to_float(k_base + WINDOW + 1)

    def gather_body(c, carry):
        m1, m2, m3, ovf, above = carry
        ks = pl.multiple_of(c * KC, KC)
        s = st_ref[pl.ds(ks, KC), :].reshape(KC // (SUBLANES * n_part), n_part, SUBLANES, TQ)
        for g in range(s.shape[0]):
            sg = s[g]
            ge_hi = sg >= w_hi[None]
            above = above + ge_hi.astype(jnp.int32)
            v = jnp.where(ge_hi, -jnp.inf, sg)
            t1 = jnp.minimum(m1, v)
            m1 = jnp.maximum(m1, v)
            t2 = jnp.minimum(m2, t1)
            m2 = jnp.maximum(m2, t1)
            ovf = jnp.maximum(ovf, jnp.minimum(m3, t2))
            m3 = jnp.maximum(m3, t2)
        return m1, m2, m3, ovf, above

    ninf = jnp.full((n_part, SUBLANES, TQ), -jnp.inf, F32)
    m1, m2, m3, ovf, above = lax.fori_loop(
        0, nkc, gather_body, (ninf, ninf, ninf, ninf, jnp.zeros((n_part, SUBLANES, TQ), jnp.int32)))
    gathered = jnp.concatenate([m1, m2, m3], axis=0)
    above8 = jnp.sum(above, axis=0)
    above8 = jnp.broadcast_to(jnp.sum(above8, axis=0, keepdims=True), (SUBLANES, TQ))

    def count_ge_gathered(thr8):
        c8 = jnp.sum((gathered >= thr8[None]).astype(jnp.int32), axis=0)
        return above8 + jnp.broadcast_to(jnp.sum(c8, axis=0, keepdims=True), (SUBLANES, TQ))

    def refine(count_fn, base, unroll=False):
        cnt0 = count_fn(_key_to_float(base))
        def body(it, carry):
            off, cnt_lo = carry
            cand = off | jnp.left_shift(jnp.int32(1), WINDOW_BITS - it)
            cnt = count_fn(_key_to_float(base + cand))
            ok = cnt >= n_keep
            return jnp.where(ok, cand, off), jnp.where(ok, cnt, cnt_lo)
        carry = (jnp.zeros((SUBLANES, TQ), jnp.int32), cnt0)
        if unroll:
            for it in range(WINDOW_BITS + 1):
                carry = body(it, carry)
        else:
            carry = lax.fori_loop(0, WINDOW_BITS + 1, body, carry)
        key_ref[0] = base + carry[0]
        key_ref[1] = carry[1]

    kp0 = kp_ref[0, 0:KC, :]
    for hd in range(N_HEADS):
        lga_ref[hd] = jnp.dot(kp0, qpt_ref[0, hd], preferred_element_type=F32)
    refine(count_ge_gathered, k_base, unroll=True)

    dropped = jnp.max(jnp.where(ovf >= w_lo[None], 1, 0))
    in_window = count_ge_gathered(w_lo)
    outside = jnp.where(in_window < n_keep, 1, jnp.where(above8 >= n_keep, 1, 0))
    outside = jnp.where(w_lo > -FLT_MAX, outside, 0)
    gathered_ok = (dropped == 0) & (jnp.max(outside) == 0)

    @pl.when(jnp.logical_not(gathered_ok))
    def _():
        refine(count_ge, k_base_wide)

    key = key_ref[0]
    cnt_lo = key_ref[1]
    lo8 = _key_to_float(key)
    hi8 = _key_to_float(key + 1)
    short = lo8 == -jnp.inf
    lo8 = jnp.where(short, -FLT_MAX, lo8)
    lo = lo8[0:1]

    has_tie = jnp.max(jnp.where(short, n_keep, cnt_lo)) > n_keep

    @pl.when(jnp.logical_not(has_tie))
    def _():
        def body(c, carry):
            ks = pl.multiple_of(c * KC, KC)
            bias_ref[pl.ds(ks, KC), :] = jnp.where(st_ref[pl.ds(ks, KC), :] >= lo, 0.0, NEG)
            return carry
        lax.fori_loop(0, nkc, body, 0)

    @pl.when(has_tie)
    def _():
        hi = hi8[0:1]
        key_ref[1] = count_ge_gathered(hi8)

        @pl.when(jnp.logical_not(gathered_ok))
        def _():
            key_ref[1] = count_ge(hi8)

        need = (n_keep - key_ref[1]).astype(F32)[0:1]
        tri = (lax.broadcasted_iota(jnp.int32, (KC, KC), 0)
               >= lax.broadcasted_iota(jnp.int32, (KC, KC), 1)).astype(BF16)

        def tie_chunk(c, taken):
            ks = pl.multiple_of(c * KC, KC)
            s = st_ref[pl.ds(ks, KC), :]
            ge_lo = s >= lo
            ge_hi = s >= hi
            tied = jnp.where(ge_lo, jnp.where(ge_hi, 0.0, 1.0), 0.0).astype(BF16)
            rank = taken + jnp.dot(tri, tied, preferred_element_type=F32)
            bias_ref[pl.ds(ks, KC), :] = jnp.where(
                ge_lo, jnp.where(ge_hi, 0.0, jnp.where(rank <= need, 0.0, NEG)), NEG)
            return rank[KC - 1:KC, :]

        taken = lax.fori_loop(0, nkc // 2,
                              lambda i, t: tie_chunk(2 * i + 1, tie_chunk(2 * i, t)),
                              jnp.zeros((1, TQ), F32))

        @pl.when(nkc % 2 == 1)
        def _():
            tie_chunk(nkc - 1, taken)

    for acc_ref in acc_refs:
        acc_ref[...] = jnp.zeros(acc_ref.shape, F32)
    ones_rows = (lax.broadcasted_iota(jnp.int32, (N_ACC_ROWS - HEAD_DIM, KC), 0) == 0).astype(BF16)

    for m_ref in m_refs:
        m_ref[...] = jnp.full(m_ref.shape, NEG, F32)
    buf_a = (lga_ref, mxa_refs)
    buf_b = (lgb_ref, mxb_refs)

    def heads_step(c_qk, qk_buf, c_pv, pv_buf, raw_ready=False):
        if c_qk is not None:
            ks = pl.multiple_of(c_qk * KC, KC)
            kpc = kp_ref[0, pl.ds(ks, KC), :]
            bias = bias_ref[pl.ds(ks, KC), :]
        if c_pv is not None:
            vtc = jnp.concatenate([vt_ref[0, c_pv], ones_rows], axis=0)
        for hd in range(N_HEADS):
            if c_qk is not None:
                raw = qk_buf[0][hd] if raw_ready else jnp.dot(kpc, qpt_ref[0, hd],
                                                                preferred_element_type=F32)
                lg = raw + bias
                qk_buf[0][hd] = lg
                qk_buf[1][hd][0:1, :] = jnp.max(lg, axis=0, keepdims=True)
            if c_pv is not None:
                m_old = m_refs[hd][0:1, :]
                m_new = jnp.maximum(m_old, pv_buf[1][hd][0:1, :])
                alpha = jnp.exp2(m_old - m_new)
                p = jnp.exp2(pv_buf[0][hd] - m_new).astype(BF16)
                acc_refs[hd][...] = alpha * acc_refs[hd][...] + jnp.dot(
                    vtc, p, preferred_element_type=F32)
                m_refs[hd][0:1, :] = m_new

    n_steps = nkc - 1
    heads_step(0, buf_a, None, None, raw_ready=True)

    def quad_body(i, carry):
        heads_step(4 * i + 1, buf_b, 4 * i, buf_a)
        heads_step(4 * i + 2, buf_a, 4 * i + 1, buf_b)
        heads_step(4 * i + 3, buf_b, 4 * i + 2, buf_a)
        heads_step(4 * i + 4, buf_a, 4 * i + 3, buf_b)
        return carry

    lax.fori_loop(0, n_steps // 4, quad_body, 0)
    done = (n_steps // 4) * 4
    rem = n_steps - done

    @pl.when(rem == 0)
    def _():
        heads_step(None, None, done, buf_a)

    @pl.when(rem == 1)
    def _():
        heads_step(done + 1, buf_b, done, buf_a)
        heads_step(None, None, done + 1, buf_b)

    @pl.when(rem == 2)
    def _():
        heads_step(done + 1, buf_b, done, buf_a)
        heads_step(done + 2, buf_a, done + 1, buf_b)
        heads_step(None, None, done + 2, buf_a)

    @pl.when(rem == 3)
    def _():
        heads_step(done + 1, buf_b, done, buf_a)
        heads_step(done + 2, buf_a, done + 1, buf_b)
        heads_step(done + 3, buf_b, done + 2, buf_a)
        heads_step(None, None, done + 3, buf_b)

    for hd in range(N_HEADS):
        acc = acc_refs[hd][...]
        ot_ref[hd * HEAD_DIM:(hd + 1) * HEAD_DIM, :] = acc[0:HEAD_DIM] / acc[HEAD_DIM:HEAD_DIM + 1]
    yb_ref[0] = (ot_ref[...].T * saz_ref[0]).astype(BF16)


def _post_kernel(x_ref, ya_ref, yb_ref, ga_ref, gb_ref, p_ref,
                 wa_ref, wb_ref, wo_ref, pg_ref, wpg_ref, wpp_ref, fg_ref, out_ref, *, final_norm):
    merged = (ga_ref[0] * jnp.dot(ya_ref[0], wa_ref[...], preferred_element_type=F32)
              + gb_ref[0] * jnp.dot(yb_ref[0], wb_ref[...], preferred_element_type=F32))
    x1 = x_ref[0] + jnp.dot(merged.astype(BF16), wo_ref[...], preferred_element_type=F32)
    gate = _sigmoid(jnp.dot(_rms(x1, pg_ref[...]).astype(BF16), wpg_ref[...],
                            preferred_element_type=F32))
    x2 = x1 + gate * jnp.dot(p_ref[0].astype(BF16), wpp_ref[...], preferred_element_type=F32)
    out_ref[0] = _rms(x2, fg_ref[...]) if final_norm else x2


def _pack_w_in(w):
    o_in, o_z, o_q, o_k, o_v, o_az, o_iq, o_ik, o_iw, o_ga, o_gb = (
        0, 1024, 1536, 2048, 2112, 2176, 2688, 3200, 3264, 3272, 4296)
    pad = jnp.zeros((D_MODEL, LANES - IDX_DIM - N_IDX_HEADS), w.dtype)
    cols = [w[:, o_in:o_z], w[:, o_z:o_q], w[:, o_az:o_iq],
            w[:, o_ga:o_gb], w[:, o_gb:o_gb + D_MODEL], w[:, o_k:o_az],
            w[:, o_ik:o_ga], pad]
    w_tok = jnp.concatenate(cols, axis=1).astype(BF16)
    rows = [w[:, o_q:o_k], w[:, o_iq:o_ik], w[:, o_v:o_az]]
    w_feat = jnp.concatenate(rows, axis=1).T.astype(BF16)
    return w_tok, w_feat


def _layer(x, p, norm_g, w_in, conv_w, conv_b, ln_g, ln_b, w_a, w_b, w_o,
           ple_g, w_pg, w_pp, out_g, final_norm):
    B, L, D = x.shape
    n_keep = min(TOPK_MAX, L // 4)
    cp = functools.partial(pltpu.CompilerParams, vmem_limit_bytes=VMEM_LIMIT)
    row = lambda a: a.reshape(1, -1)

    tm1 = TM_PROJ
    w_tok, w_feat = _pack_w_in(w_in)
    tok = lambda w: pl.BlockSpec((1, tm1, w), lambda b, i: (b, i, 0))
    featm = lambda r: pl.BlockSpec((1, N_HEADS, r, tm1), lambda b, i: (b, 0, 0, i))
    const = lambda a: pl.BlockSpec(a.shape, lambda b, i: (0,) * a.ndim, pipeline_mode=pl.Buffered(1))
    f32s = lambda w: jax.ShapeDtypeStruct((B, L, w), F32)
    consts1 = [row(norm_g), w_tok, w_feat, conv_w, row(conv_b), row(ln_g), row(ln_b)]
    ya, saz, ga, gb, qpt, kp, vt, iqt, ik, iwt = pl.pallas_call(
        functools.partial(_proj_kernel, tm=tm1),
        grid=(B, L // tm1),
        in_specs=[tok(D)] + [const(a) for a in consts1],
        out_specs=[tok(D_CONV), tok(D_ATTN), tok(D), tok(D),
                   featm(LANES), tok(LANES),
                   pl.BlockSpec((1, tm1 // KC, HEAD_DIM, KC), lambda b, i: (b, i, 0, 0)),
                   featm(IDX_DIM), tok(IDX_DIM),
                   pl.BlockSpec((1, N_IDX_HEADS, tm1), lambda b, i: (b, 0, i))],
        out_shape=[jax.ShapeDtypeStruct((B, L, D_CONV), BF16), f32s(D_ATTN), f32s(D), f32s(D),
                   jax.ShapeDtypeStruct((B, N_HEADS, LANES, L), BF16),
                   jax.ShapeDtypeStruct((B, L, LANES), BF16),
                   jax.ShapeDtypeStruct((B, L // KC, HEAD_DIM, KC), BF16),
                   jax.ShapeDtypeStruct((B, N_IDX_HEADS, IDX_DIM, L), BF16),
                   jax.ShapeDtypeStruct((B, L, IDX_DIM), BF16),
                   jax.ShapeDtypeStruct((B, N_IDX_HEADS, L), F32)],
        scratch_shapes=[pltpu.VMEM((HALO + tm1, D_CONV), F32),
                        pltpu.VMEM((SUBLANES - 1, HALO + tm1 - SUBLANES, D_CONV), F32),
                        pltpu.VMEM((HALO, D_CONV), F32)],
        compiler_params=cp(dimension_semantics=("arbitrary", "arbitrary")),
        name="proj",
    )(x, *consts1)

    yb = pl.pallas_call(
        functools.partial(_attn_kernel, n_keep=n_keep),
        grid=(B, L // TQ),
        in_specs=[pl.BlockSpec((1, N_IDX_HEADS, IDX_DIM, TQ), lambda b, j: (b, 0, 0, j)),
                  pl.BlockSpec((1, L, IDX_DIM), lambda b, j: (b, 0, 0)),
                  pl.BlockSpec((1, N_IDX_HEADS, TQ), lambda b, j: (b, 0, j)),
                  pl.BlockSpec((1, N_HEADS, LANES, TQ), lambda b, j: (b, 0, 0, j)),
                  pl.BlockSpec((1, L, LANES), lambda b, j: (b, 0, 0)),
                  pl.BlockSpec((1, L // KC, HEAD_DIM, KC), lambda b, j: (b, 0, 0, 0)),
                  pl.BlockSpec((1, TQ, D_ATTN), lambda b, j: (b, j, 0))],
        out_specs=pl.BlockSpec((1, TQ, D_ATTN), lambda b, j: (b, j, 0)),
        out_shape=jax.ShapeDtypeStruct((B, L, D_ATTN), BF16),
        scratch_shapes=[pltpu.VMEM((L, TQ), F32),
                        pltpu.VMEM((L, TQ), BF16),
                        pltpu.VMEM((2, SUBLANES, TQ), jnp.int32),
                        pltpu.VMEM((L, TQ), F32),
                        pltpu.VMEM((D_ATTN, TQ), F32),
                        pltpu.VMEM((N_HEADS, KC, TQ), F32),
                        pltpu.VMEM((N_HEADS, KC, TQ), F32)]
                       + [pltpu.VMEM((SUBLANES, TQ), F32)] * (3 * N_HEADS)
                       + [pltpu.VMEM((N_ACC_ROWS, TQ), F32)] * N_HEADS,
        compiler_params=cp(dimension_semantics=("arbitrary", "arbitrary")),
        name="attn",
    )(iqt, ik, iwt, qpt, kp, vt, saz)

    tm3 = TM_POST
    tok = lambda w: pl.BlockSpec((1, tm3, w), lambda b, i: (b, i, 0))
    consts3 = [w_a.astype(BF16), w_b.astype(BF16), w_o.astype(BF16), row(ple_g),
               w_pg.astype(BF16), w_pp.astype(BF16), row(out_g)]
    return pl.pallas_call(
        functools.partial(_post_kernel, final_norm=final_norm),
        grid=(B, L // tm3),
        in_specs=[tok(D), tok(D_CONV), tok(D_ATTN), tok(D), tok(D), tok(D_PLE)]
                 + [const(a) for a in consts3],
        out_specs=tok(D),
        out_shape=jax.ShapeDtypeStruct((B, L, D), F32),
        compiler_params=cp(dimension_semantics=("arbitrary", "arbitrary")),
        name="post",
    )(x, ya, yb, ga, gb, p, *consts3)


def kernel(x, p, norm_g, w_in, conv_w, conv_b, conv_ln_g, conv_ln_b, w_a_out, w_b_out,
           w_o, ple_norm_g, w_ple_gate, w_ple_proj, final_g):
    depth = w_in.shape[0]
    for i in range(depth):
        x = _layer(x, p[i], norm_g[i], w_in[i], conv_w[i], conv_b[i], conv_ln_g[i],
                   conv_ln_b[i], w_a_out[i], w_b_out[i], w_o[i], ple_norm_g[i],
                   w_ple_gate[i], w_ple_proj[i], final_g, i == depth - 1)
    return x
```

```python
import functools
import math

import jax
import jax.numpy as jnp
import numpy as np
from jax import lax
from jax.experimental import pallas as pl
from jax.experimental.pallas import tpu as pltpu

D_MODEL = 1024
D_PLE = 256
D_CONV = 512
CONV_WIDTH = 31
N_HEADS = 8
HEAD_DIM = 64
D_ATTN = N_HEADS * HEAD_DIM
N_IDX_HEADS = 8
IDX_DIM = 64
TOPK_MAX = 256
EPS = 1e-6

LANES = 128
SUBLANES = 8
TQ = 256
KC = 256
assert TQ == KC
TM_PROJ = 512
TM_POST = 1024
HALO = 32
NEG = -1e30
FLT_MAX = float(np.finfo(np.float32).max)
FLT_TINY = float(np.finfo(np.float32).tiny)
BF16_ROWS = 16
WINDOW_BITS = 16
WINDOW = 2 ** WINDOW_BITS
KEY16_NEG_INF = -(2 ** 15) + 0x7F
LOG2E = math.log2(math.e)
POS_SHIFT = 6
POS_RADIX = 2 ** POS_SHIFT
N_ACC_ROWS = HEAD_DIM + BF16_ROWS
VMEM_LIMIT = 56 * 1024 * 1024

F32 = jnp.float32
BF16 = jnp.bfloat16

_C_VAL, _C_GATE, _C_Z, _C_AZ, _C_GA, _C_GB, _C_KV, _C_IKW, _C_END = (
    0, 512, 1024, 1536, 2048, 3072, 4096, 4224, 4352)
_R_Q, _R_IQ, _R_V, _R_END = 0, 512, 1024, 1088

_NT = (((1,), (1,)), ((), ()))


def _bf16_split3(a):
    a = np.float32(a)
    p1 = np.float32(a.astype(BF16))
    p2 = np.float32(np.float32(a - p1).astype(BF16))
    p3 = np.float32(a - p1 - p2)
    assert np.float32(np.float32(p3).astype(BF16)) == p3
    return float(p1), float(p2), float(p3)


def _sigmoid(x):
    return 1.0 / (1.0 + jnp.exp(-x))


def _silu(x):
    return x * _sigmoid(x)


def _rms(x, g):
    return x * lax.rsqrt(jnp.mean(x * x, axis=-1, keepdims=True) + EPS) * g


def _proj_kernel(x_ref, g_ref, w_ref, wt_ref, cw_ref, cb_ref, lng_ref, lnb_ref,
                 ya_ref, saz_ref, ga_ref, gb_ref,
                 qpt_ref, kp_ref, vt_ref, iqt_ref, ik_ref, iwt_ref,
                 ext_ref, es_ref, carry_ref, *, tm):
    i = pl.program_id(1)
    h = _rms(x_ref[0], g_ref[...]).astype(BF16)

    def mm(a, b):
        return jnp.dot(h, w_ref[:, a:b], preferred_element_type=F32)

    def mm_t(a, b):
        return lax.dot_general(wt_ref[a:b, :], h, _NT, preferred_element_type=F32)

    u = mm(_C_VAL, _C_GATE) * _sigmoid(mm(_C_GATE, _C_Z))
    prev = carry_ref[...]
    ext_ref[0:HALO, :] = jnp.where(i > 0, prev, jnp.zeros_like(prev))
    ext_ref[HALO:, :] = u
    carry_ref[...] = u[tm - HALO:, :]
    n_sh = HALO + tm - SUBLANES
    for r in range(1, SUBLANES):
        es_ref[r - 1] = ext_ref[r:r + n_sh, :]

    def w_conformer():
        base = HALO - (CONV_WIDTH - 1)
        c = jnp.zeros((tm, D_CONV), F32)
        for jj in range(CONV_WIDTH):
            row8, r = divmod(base + jj, SUBLANES)
            row = row8 * SUBLANES
            tap = ext_ref[row:row + tm, :] if r == 0 else es_ref[r - 1, row:row + tm, :]
            c = c + cw_ref[jj:jj + 1, :] * tap
        c = c + cb_ref[...]
        mu = jnp.mean(c, axis=-1, keepdims=True)
        cc = c - mu
        var = jnp.mean(cc * cc, axis=-1, keepdims=True)
        ln = cc * lax.rsqrt(var + EPS) * lng_ref[...] + lnb_ref[...]
        ya_ref[0] = (_silu(ln) * _silu(mm(_C_Z, _C_AZ))).astype(BF16)

    def w_saz():
        saz_ref[0] = _silu(mm(_C_AZ, _C_GA))

    def w_gate(ref, c0, half):
        def run():
            lo = half * (D_MODEL // 2)
            ref[0, :, lo:lo + D_MODEL // 2] = _sigmoid(mm(c0 + lo, c0 + lo + D_MODEL // 2))
        return run

    def w_keys():
        lane = lax.broadcasted_iota(jnp.int32, (tm, LANES), 1)
        pos = i * tm + lax.broadcasted_iota(jnp.int32, (tm, LANES), 0)
        col = lane - HEAD_DIM
        kext = jnp.where(col < 3, 1.0,
               jnp.where(col < 6, (pos >> POS_SHIFT).astype(F32),
               jnp.where(col < 9, (pos & (POS_RADIX - 1)).astype(F32), 0.0)))
        kp_ref[0] = jnp.where(col < 0, mm(_C_KV, _C_IKW), kext).astype(BF16)
        ikw = mm(_C_IKW, _C_END)
        ik_ref[0] = ikw[:, :IDX_DIM].astype(BF16)
        iwt_ref[0] = ikw.T[IDX_DIM:IDX_DIM + N_IDX_HEADS, :] * (N_IDX_HEADS ** -0.5)

    def w_queries():
        sub = lax.broadcasted_iota(jnp.int32, (BF16_ROWS, tm), 0)
        tpos = (i * tm + lax.broadcasted_iota(jnp.int32, (BF16_ROWS, tm), 1)).astype(F32)
        qt = mm_t(_R_Q, _R_IQ) * (LOG2E * HEAD_DIM ** -0.5)
        for hd in range(N_HEADS):
            a = np.float32(2.0 ** (-8.0 * (hd + 1) / N_HEADS) * LOG2E)
            a1, a2, a3 = _bf16_split3(a)
            v = tpos * float(-a)
            v1 = v.astype(BF16).astype(F32)
            v2 = (v - v1).astype(BF16).astype(F32)
            v3 = v - v1 - v2
            ext = jnp.where(sub == 0, v1, jnp.where(sub == 1, v2, jnp.where(sub == 2, v3,
                  jnp.where(sub == 3, POS_RADIX * a1, jnp.where(sub == 4, POS_RADIX * a2,
                  jnp.where(sub == 5, POS_RADIX * a3, jnp.where(sub == 6, a1,
                  jnp.where(sub == 7, a2, jnp.where(sub == 8, a3, 0.0)))))))))
            qpt_ref[0, hd, 0:HEAD_DIM, :] = qt[hd * HEAD_DIM:(hd + 1) * HEAD_DIM, :].astype(BF16)
            qpt_ref[0, hd, HEAD_DIM:HEAD_DIM + BF16_ROWS, :] = ext.astype(BF16)
            qpt_ref[0, hd, HEAD_DIM + BF16_ROWS:, :] = jnp.zeros((LANES - HEAD_DIM - BF16_ROWS, tm), BF16)

    def w_index_values():
        iqt = mm_t(_R_IQ, _R_V) * (IDX_DIM ** -0.5)
        for hd in range(N_IDX_HEADS):
            iqt_ref[0, hd] = iqt[hd * IDX_DIM:(hd + 1) * IDX_DIM, :].astype(BF16)
        vt = mm_t(_R_V, _R_END).astype(BF16)
        for c in range(tm // KC):
            vt_ref[0, c] = vt[:, c * KC:(c + 1) * KC]

    for work in (w_conformer, w_saz, w_gate(ga_ref, _C_GA, 0), w_gate(ga_ref, _C_GA, 1),
                 w_gate(gb_ref, _C_GB, 0), w_gate(gb_ref, _C_GB, 1), w_keys, w_queries,
                 w_index_values):
        work()


def _threshold_of_key(k, shift, n_subnormal):
    k_inf = 0x7F800000 >> shift
    k = jnp.clip(k, -k_inf - 1, k_inf)
    magnitude_mask = jnp.int32((2 ** 31 - 1) >> shift)
    f = lax.bitcast_convert_type(jnp.left_shift(jnp.where(k >= 0, k, k ^ magnitude_mask), shift), F32)
    pos = jnp.where(k < n_subnormal, jnp.where(k == 0, 0.0, FLT_TINY), f)
    neg = jnp.where(k >= -n_subnormal, 0.0, f)
    return jnp.where(k >= 0, pos, neg)


def _key_to_float(k):
    return _threshold_of_key(k, 0, 2 ** 23)


def _key16_to_float(k):
    return _threshold_of_key(k, 16, 2 ** 7)


def _attn_kernel(iqt_ref, ik_ref, iwt_ref, qpt_ref, kp_ref, vt_ref, saz_ref, yb_ref,
                 st_ref, rt_ref, key_ref, bias_ref, ot_ref, lga_ref, lgb_ref, *head_refs, n_keep):
    m_refs, mxa_refs, mxb_refs, acc_refs = (head_refs[g * N_HEADS:(g + 1) * N_HEADS] for g in range(4))
    j = pl.program_id(1)
    t0 = j * TQ
    nkc = j + 1

    tcol = t0 + lax.broadcasted_iota(jnp.int32, (KC, TQ), 1)
    srow0 = lax.broadcasted_iota(jnp.int32, (KC, TQ), 0)

    def score_chunk(c):
        ks = pl.multiple_of(c * KC, KC)
        kic = ik_ref[0, pl.ds(ks, KC), :]
        acc = jnp.zeros((KC, TQ), F32)
        for hd in range(N_IDX_HEADS):
            d = jnp.dot(kic, iqt_ref[0, hd], preferred_element_type=F32)
            acc = acc + iwt_ref[0, hd:hd + 1, :] * jnp.maximum(d, 0.0)
        masked = jnp.where((ks + srow0) <= tcol, acc, -jnp.inf)
        st_ref[pl.ds(ks, KC), :] = masked
        rt_ref[pl.ds(ks, KC), :] = masked.astype(BF16)

    def score_pair(i, carry):
        score_chunk(2 * i)
        score_chunk(2 * i + 1)
        return carry

    lax.fori_loop(0, nkc // 2, score_pair, 0)

    @pl.when(nkc % 2 == 1)
    def _():
        score_chunk(nkc - 1)

    n_part = 4

    def count_ge(thr8):
        def body(c, cnt):
            ks = pl.multiple_of(c * KC, KC)
            s = st_ref[pl.ds(ks, KC), :].reshape(KC // (SUBLANES * n_part), n_part, SUBLANES, TQ)
            return cnt + jnp.sum((s >= thr8[None, None]).astype(jnp.int32), axis=0)
        cnt = lax.fori_loop(0, nkc, body, jnp.zeros((n_part, SUBLANES, TQ), jnp.int32))
        cnt8 = jnp.sum(cnt, axis=0)
        return jnp.broadcast_to(jnp.sum(cnt8, axis=0, keepdims=True), (SUBLANES, TQ))

    def count_ge16(thr_bf):
        def chunk(c):
            ks = pl.multiple_of(c * KC, KC)
            r = rt_ref[pl.ds(ks, KC), :].reshape(KC // (BF16_ROWS * 2), 2, BF16_ROWS, TQ)
            w = jnp.where(r >= thr_bf[None, None], jnp.ones((), BF16), jnp.zeros((), BF16))
            part = w[0]
            for g in range(1, w.shape[0]):
                part = part + w[g]
            return part
        cnt = lax.fori_loop(0, nkc // 4,
                            lambda i, a: a + (chunk(4 * i) + chunk(4 * i + 1))
                            + (chunk(4 * i + 2) + chunk(4 * i + 3)),
                            jnp.zeros((2, BF16_ROWS, TQ), BF16))
        cnt = lax.fori_loop((nkc // 4) * 4, nkc, lambda c, a: a + chunk(c), cnt).astype(F32)
        tot = jnp.sum(cnt[0] + cnt[1], axis=0, keepdims=True)
        return jnp.broadcast_to(tot, (SUBLANES, TQ))

    def thr16(k16):
        f = _key16_to_float(k16)
        return jnp.concatenate([f] * (BF16_ROWS // SUBLANES), axis=0).astype(BF16)

    few_keys = nkc * KC <= n_keep
    k16 = jnp.where(count_ge16(thr16(jnp.zeros((SUBLANES, TQ), jnp.int32))) >= n_keep,
                    jnp.int32(0), jnp.int32(-(2 ** 15)))
    k16 = jnp.where(few_keys, jnp.int32(KEY16_NEG_INF), k16)

    def bit16_body(it, k):
        cand = k | jnp.left_shift(jnp.int32(1), 14 - it)
        return jnp.where(count_ge16(thr16(cand)) >= n_keep, cand, k)

    k16 = lax.fori_loop(0, jnp.where(few_keys, 0, 15), bit16_body, k16)
    hb_bits = lax.bitcast_convert_type(_key16_to_float(k16), jnp.int32)
    k_hb = hb_bits ^ ((hb_bits >> 31) & jnp.int32(0x7FFFFFFF))
    k_base = k_hb - WINDOW // 2
    k_base_wide = k_hb - WINDOW
    w_lo = jnp.maximum(_key_to_float(k_base), -FLT_MAX)
    w_hi = _key_to_float(k_base + WINDOW + 1)

    def gather_body(c, carry):
        m1, m2, m3, ovf, above = carry
        ks = pl.multiple_of(c * KC, KC)
        s = st_ref[pl.ds(ks, KC), :].reshape(KC // (SUBLANES * n_part), n_part, SUBLANES, TQ)
        for g in range(s.shape[0]):
            sg = s[g]
            ge_hi = sg >= w_hi[None]
            above = above + ge_hi.astype(jnp.int32)
            v = jnp.where(ge_hi, -jnp.inf, sg)
            t1 = jnp.minimum(m1, v)
            m1 = jnp.maximum(m1, v)
            t2 = jnp.minimum(m2, t1)
            m2 = jnp.maximum(m2, t1)
            ovf = jnp.maximum(ovf, jnp.minimum(m3, t2))
            m3 = jnp.maximum(m3, t2)
        return m1, m2, m3, ovf, above

    ninf = jnp.full((n_part, SUBLANES, TQ), -jnp.inf, F32)
    m1, m2, m3, ovf, above = lax.fori_loop(
        0, nkc, gather_body, (ninf, ninf, ninf, ninf, jnp.zeros((n_part, SUBLANES, TQ), jnp.int32)))
    gathered = jnp.concatenate([m1, m2, m3], axis=0)
    above8 = jnp.sum(above, axis=0)
    above8 = jnp.broadcast_to(jnp.sum(above8, axis=0, keepdims=True), (SUBLANES, TQ))

    def count_ge_gathered(thr8):
        c8 = jnp.sum((gathered >= thr8[None]).astype(jnp.int32), axis=0)
        return above8 + jnp.broadcast_to(jnp.sum(c8, axis=0, keepdims=True), (SUBLANES, TQ))

    def refine(count_fn, base, unroll=False):
        cnt0 = count_fn(_key_to_float(base))
        def body(it, carry):
            off, cnt_lo = carry
            cand = off | jnp.left_shift(jnp.int32(1), WINDOW_BITS - it)
            cnt = count_fn(_key_to_float(base + cand))
            ok = cnt >= n_keep
            return jnp.where(ok, cand, off), jnp.where(ok, cnt, cnt_lo)
        carry = (jnp.zeros((SUBLANES, TQ), jnp.int32), cnt0)
        if unroll:
            for it in range(WINDOW_BITS + 1):
                carry = body(it, carry)
        else:
            carry = lax.fori_loop(0, WINDOW_BITS + 1, body, carry)
        key_ref[0] = base + carry[0]
        key_ref[1] = carry[1]

    kp0 = kp_ref[0, 0:KC, :]
    for hd in range(N_HEADS):
        lga_ref[hd] = jnp.dot(kp0, qpt_ref[0, hd], preferred_element_type=F32)
    refine(count_ge_gathered, k_base, unroll=True)

    dropped = jnp.max(jnp.where(ovf >= w_lo[None], 1, 0))
    in_window = count_ge_gathered(w_lo)
    outside = jnp.where(in_window < n_keep, 1, jnp.where(above8 >= n_keep, 1, 0))
    outside = jnp.where(w_lo > -FLT_MAX, outside, 0)
    gathered_ok = (dropped == 0) & (jnp.max(outside) == 0)

    @pl.when(jnp.logical_not(gathered_ok))
    def _():
        refine(count_ge, k_base_wide)

    key = key_ref[0]
    cnt_lo = key_ref[1]
    lo8 = _key_to_float(key)
    hi8 = _key_to_float(key + 1)
    short = lo8 == -jnp.inf
    lo8 = jnp.where(short, -FLT_MAX, lo8)
    lo = lo8[0:1]

    has_tie = jnp.max(jnp.where(short, n_keep, cnt_lo)) > n_keep

    @pl.when(jnp.logical_not(has_tie))
    def _():
        def body(c, carry):
            ks = pl.multiple_of(c * KC, KC)
            bias_ref[pl.ds(ks, KC), :] = jnp.where(st_ref[pl.ds(ks, KC), :] >= lo, 0.0, NEG)
            return carry
        lax.fori_loop(0, nkc, body, 0)

    @pl.when(has_tie)
    def _():
        hi = hi8[0:1]
        key_ref[1] = count_ge_gathered(hi8)

        @pl.when(jnp.logical_not(gathered_ok))
        def _():
            key_ref[1] = count_ge(hi8)

        need = (n_keep - key_ref[1]).astype(F32)[0:1]
        tri = (lax.broadcasted_iota(jnp.int32, (KC, KC), 0)
               >= lax.broadcasted_iota(jnp.int32, (KC, KC), 1)).astype(BF16)

        def tie_chunk(c, taken):
            ks = pl.multiple_of(c * KC, KC)
            s = st_ref[pl.ds(ks, KC), :]
            ge_lo = s >= lo
            ge_hi = s >= hi
            tied = jnp.where(ge_lo, jnp.where(ge_hi, 0.0, 1.0), 0.0).astype(BF16)
            rank = taken + jnp.dot(tri, tied, preferred_element_type=F32)
            bias_ref[pl.ds(ks, KC), :] = jnp.where(
                ge_lo, jnp.where(ge_hi, 0.0, jnp.where(rank <= need, 0.0, NEG)), NEG)
            return rank[KC - 1:KC, :]

        taken = lax.fori_loop(0, nkc // 2,
                              lambda i, t: tie_chunk(2 * i + 1, tie_chunk(2 * i, t)),
                              jnp.zeros((1, TQ), F32))

        @pl.when(nkc % 2 == 1)
        def _():
            tie_chunk(nkc - 1, taken)

    for acc_ref in acc_refs:
        acc_ref[...] = jnp.zeros(acc_ref.shape, F32)
    ones_rows = (lax.broadcasted_iota(jnp.int32, (N_ACC_ROWS - HEAD_DIM, KC), 0) == 0).astype(BF16)

    for m_ref in m_refs:
        m_ref[...] = jnp.full(m_ref.shape, NEG, F32)
    buf_a = (lga_ref, mxa_refs)
    buf_b = (lgb_ref, mxb_refs)

    def heads_step(c_qk, qk_buf, c_pv, pv_buf, raw_ready=False):
        if c_qk is not None:
            ks = pl.multiple_of(c_qk * KC, KC)
            kpc = kp_ref[0, pl.ds(ks, KC), :]
            bias = bias_ref[pl.ds(ks, KC), :]
        if c_pv is not None:
            vtc = jnp.concatenate([vt_ref[0, c_pv], ones_rows], axis=0)
        for hd in range(N_HEADS):
            if c_qk is not None:
                raw = qk_buf[0][hd] if raw_ready else jnp.dot(kpc, qpt_ref[0, hd],
                                                                preferred_element_type=F32)
                lg = raw + bias
                qk_buf[0][hd] = lg
                qk_buf[1][hd][0:1, :] = jnp.max(lg, axis=0, keepdims=True)
            if c_pv is not None:
                m_old = m_refs[hd][0:1, :]
                m_new = jnp.maximum(m_old, pv_buf[1][hd][0:1, :])
                alpha = jnp.exp2(m_old - m_new)
                p = jnp.exp2(pv_buf[0][hd] - m_new).astype(BF16)
                acc_refs[hd][...] = alpha * acc_refs[hd][...] + jnp.dot(
                    vtc, p, preferred_element_type=F32)
                m_refs[hd][0:1, :] = m_new

    n_steps = nkc - 1
    heads_step(0, buf_a, None, None, raw_ready=True)

    def quad_body(i, carry):
        heads_step(4 * i + 1, buf_b, 4 * i, buf_a)
        heads_step(4 * i + 2, buf_a, 4 * i + 1, buf_b)
        heads_step(4 * i + 3, buf_b, 4 * i + 2, buf_a)
        heads_step(4 * i + 4, buf_a, 4 * i + 3, buf_b)
        return carry

    lax.fori_loop(0, n_steps // 4, quad_body, 0)
    done = (n_steps // 4) * 4
    rem = n_steps - done

    @pl.when(rem == 0)
    def _():
        heads_step(None, None, done, buf_a)

    @pl.when(rem == 1)
    def _():
        heads_step(done + 1, buf_b, done, buf_a)
        heads_step(None, None, done + 1, buf_b)

    @pl.when(rem == 2)
    def _():
        heads_step(done + 1, buf_b, done, buf_a)
        heads_step(done + 2, buf_a, done + 1, buf_b)
        heads_step(None, None, done + 2, buf_a)

    @pl.when(rem == 3)
    def _():
        heads_step(done + 1, buf_b, done, buf_a)
        heads_step(done + 2, buf_a, done + 1, buf_b)
        heads_step(done + 3, buf_b, done + 2, buf_a)
        heads_step(None, None, done + 3, buf_b)

    for hd in range(N_HEADS):
        acc = acc_refs[hd][...]
        ot_ref[hd * HEAD_DIM:(hd + 1) * HEAD_DIM, :] = acc[0:HEAD_DIM] / acc[HEAD_DIM:HEAD_DIM + 1]
    yb_ref[0] = (ot_ref[...].T * saz_ref[0]).astype(BF16)


def _post_kernel(x_ref, ya_ref, yb_ref, ga_ref, gb_ref, p_ref,
                 wa_ref, wb_ref, wo_ref, pg_ref, wpg_ref, wpp_ref, fg_ref, out_ref, *, final_norm):
    merged = (ga_ref[0] * jnp.dot(ya_ref[0], wa_ref[...], preferred_element_type=F32)
              + gb_ref[0] * jnp.dot(yb_ref[0], wb_ref[...], preferred_element_type=F32))
    x1 = x_ref[0] + jnp.dot(merged.astype(BF16), wo_ref[...], preferred_element_type=F32)
    gate = _sigmoid(jnp.dot(_rms(x1, pg_ref[...]).astype(BF16), wpg_ref[...],
                            preferred_element_type=F32))
    x2 = x1 + gate * jnp.dot(p_ref[0].astype(BF16), wpp_ref[...], preferred_element_type=F32)
    out_ref[0] = _rms(x2, fg_ref[...]) if final_norm else x2


def _pack_w_in(w):
    o_in, o_z, o_q, o_k, o_v, o_az, o_iq, o_ik, o_iw, o_ga, o_gb = (
        0, 1024, 1536, 2048, 2112, 2176, 2688, 3200, 3264, 3272, 4296)
    pad = jnp.zeros((D_MODEL, LANES - IDX_DIM - N_IDX_HEADS), w.dtype)
    cols = [w[:, o_in:o_z], w[:, o_z:o_q], w[:, o_az:o_iq],
            w[:, o_ga:o_gb], w[:, o_gb:o_gb + D_MODEL], w[:, o_k:o_az],
            w[:, o_ik:o_ga], pad]
    w_tok = jnp.concatenate(cols, axis=1).astype(BF16)
    rows = [w[:, o_q:o_k], w[:, o_iq:o_ik], w[:, o_v:o_az]]
    w_feat = jnp.concatenate(rows, axis=1).T.astype(BF16)
    return w_tok, w_feat


def _layer(x, p, norm_g, w_in, conv_w, conv_b, ln_g, ln_b, w_a, w_b, w_o,
           ple_g, w_pg, w_pp, out_g, final_norm):
    B, L, D = x.shape
    n_keep = min(TOPK_MAX, L // 4)
    cp = functools.partial(pltpu.CompilerParams, vmem_limit_bytes=VMEM_LIMIT)
    row = lambda a: a.reshape(1, -1)

    tm1 = TM_PROJ
    w_tok, w_feat = _pack_w_in(w_in)
    tok = lambda w: pl.BlockSpec((1, tm1, w), lambda b, i: (b, i, 0))
    featm = lambda r: pl.BlockSpec((1, N_HEADS, r, tm1), lambda b, i: (b, 0, 0, i))
    const = lambda a: pl.BlockSpec(a.shape, lambda b, i: (0,) * a.ndim, pipeline_mode=pl.Buffered(1))
    f32s = lambda w: jax.ShapeDtypeStruct((B, L, w), F32)
    consts1 = [row(norm_g), w_tok, w_feat, conv_w, row(conv_b), row(ln_g), row(ln_b)]
    ya, saz, ga, gb, qpt, kp, vt, iqt, ik, iwt = pl.pallas_call(
        functools.partial(_proj_kernel, tm=tm1),
        grid=(B, L // tm1),
        in_specs=[tok(D)] + [const(a) for a in consts1],
        out_specs=[tok(D_CONV), tok(D_ATTN), tok(D), tok(D),
                   featm(LANES), tok(LANES),
                   pl.BlockSpec((1, tm1 // KC, HEAD_DIM, KC), lambda b, i: (b, i, 0, 0)),
                   featm(IDX_DIM), tok(IDX_DIM),
                   pl.BlockSpec((1, N_IDX_HEADS, tm1), lambda b, i: (b, 0, i))],
        out_shape=[jax.ShapeDtypeStruct((B, L, D_CONV), BF16), f32s(D_ATTN), f32s(D), f32s(D),
                   jax.ShapeDtypeStruct((B, N_HEADS, LANES, L), BF16),
                   jax.ShapeDtypeStruct((B, L, LANES), BF16),
                   jax.ShapeDtypeStruct((B, L // KC, HEAD_DIM, KC), BF16),
                   jax.ShapeDtypeStruct((B, N_IDX_HEADS, IDX_DIM, L), BF16),
                   jax.ShapeDtypeStruct((B, L, IDX_DIM), BF16),
                   jax.ShapeDtypeStruct((B, N_IDX_HEADS, L), F32)],
        scratch_shapes=[pltpu.VMEM((HALO + tm1, D_CONV), F32),
                        pltpu.VMEM((SUBLANES - 1, HALO + tm1 - SUBLANES, D_CONV), F32),
                        pltpu.VMEM((HALO, D_CONV), F32)],
        compiler_params=cp(dimension_semantics=("arbitrary", "arbitrary")),
        name="proj",
    )(x, *consts1)

    yb = pl.pallas_call(
        functools.partial(_attn_kernel, n_keep=n_keep),
        grid=(B, L // TQ),
        in_specs=[pl.BlockSpec((1, N_IDX_HEADS, IDX_DIM, TQ), lambda b, j: (b, 0, 0, j)),
                  pl.BlockSpec((1, L, IDX_DIM), lambda b, j: (b, 0, 0)),
                  pl.BlockSpec((1, N_IDX_HEADS, TQ), lambda b, j: (b, 0, j)),
                  pl.BlockSpec((1, N_HEADS, LANES, TQ), lambda b, j: (b, 0, 0, j)),
                  pl.BlockSpec((1, L, LANES), lambda b, j: (b, 0, 0)),
                  pl.BlockSpec((1, L // KC, HEAD_DIM, KC), lambda b, j: (b, 0, 0, 0)),
                  pl.BlockSpec((1, TQ, D_ATTN), lambda b, j: (b, j, 0))],
        out_specs=pl.BlockSpec((1, TQ, D_ATTN), lambda b, j: (b, j, 0)),
        out_shape=jax.ShapeDtypeStruct((B, L, D_ATTN), BF16),
        scratch_shapes=[pltpu.VMEM((L, TQ), F32),
                        pltpu.VMEM((L, TQ), BF16),
                        pltpu.VMEM((2, SUBLANES, TQ), jnp.int32),
                        pltpu.VMEM((L, TQ), F32),
                        pltpu.VMEM((D_ATTN, TQ), F32),
                        pltpu.VMEM((N_HEADS, KC, TQ), F32),
                        pltpu.VMEM((N_HEADS, KC, TQ), F32)]
                       + [pltpu.VMEM((SUBLANES, TQ), F32)] * (3 * N_HEADS)
                       + [pltpu.VMEM((N_ACC_ROWS, TQ), F32)] * N_HEADS,
        compiler_params=cp(dimension_semantics=("arbitrary", "arbitrary")),
        name="attn",
    )(iqt, ik, iwt, qpt, kp, vt, saz)

    tm3 = TM_POST
    tok = lambda w: pl.BlockSpec((1, tm3, w), lambda b, i: (b, i, 0))
    consts3 = [w_a.astype(BF16), w_b.astype(BF16), w_o.astype(BF16), row(ple_g),
               w_pg.astype(BF16), w_pp.astype(BF16), row(out_g)]
    return pl.pallas_call(
        functools.partial(_post_kernel, final_norm=final_norm),
        grid=(B, L // tm3),
        in_specs=[tok(D), tok(D_CONV), tok(D_ATTN), tok(D), tok(D), tok(D_PLE)]
                 + [const(a) for a in consts3],
        out_specs=tok(D),
        out_shape=jax.ShapeDtypeStruct((B, L, D), F32),
        compiler_params=cp(dimension_semantics=("arbitrary", "arbitrary")),
        name="post",
    )(x, ya, yb, ga, gb, p, *consts3)


def kernel(x, p, norm_g, w_in, conv_w, conv_b, conv_ln_g, conv_ln_b, w_a_out, w_b_out,
           w_o, ple_norm_g, w_ple_gate, w_ple_proj, final_g):
    depth = w_in.shape[0]
    for i in range(depth):
        x = _layer(x, p[i], norm_g[i], w_in[i], conv_w[i], conv_b[i], conv_ln_g[i],
                   conv_ln_b[i], w_a_out[i], w_b_out[i], w_o[i], ple_norm_g[i],
                   w_ple_gate[i], w_ple_proj[i], final_g, i == depth - 1)
    return x
```
